```python
import math
import jax, jax.numpy as jnp
from jax import lax
import numpy as np

D_MODEL = 1024
BATCH = 8
SEQ = 2048
DEPTH = 2

N_META = 16
CHUNK = 128
D_MIX = D_MODEL
RET_HEADS = 4
RET_DK = 128
RET_DV = 128
RET_WIDTH = RET_HEADS * RET_DV
RET_THETA = 10000.0
DIFF_HEADS = 4
DIFF_D = 64
DIFF_DV = 2 * DIFF_D
DIFF_WIDTH = DIFF_HEADS * DIFF_DV
ROPE_THETA = 500000.0
ROPE_DIMS = DIFF_D // 4
D_FF = 2816
EPS = 1e-6

IN_SIZES = (
    RET_HEADS * RET_DK,
    RET_HEADS * RET_DK,
    RET_WIDTH,
    RET_WIDTH,
    DIFF_HEADS * 2 * DIFF_D,
    DIFF_HEADS * 2 * DIFF_D,
    DIFF_WIDTH,
)
D_IN = sum(IN_SIZES)

kernel_name = "hymba_retnet_diffattn_macaron"


def rmsnorm(x, g):
    xf = x.astype(jnp.float32)
    y = xf * lax.rsqrt(jnp.mean(xf * xf, axis=-1, keepdims=True) + EPS)
    return (y * g.astype(jnp.float32)).astype(x.dtype)


def swiglu(x, w_gate, w_up, w_down):
    return (jax.nn.silu(x @ w_gate) * (x @ w_up)) @ w_down


def retention_rotate(x, pos):
    d = x.shape[-1]
    angle = RET_THETA ** (-jnp.linspace(0.0, 1.0, d // 2, dtype=jnp.float32))
    freqs = pos[:, None] * angle[None, :]
    cos = jnp.cos(freqs)[None, :, None, :]
    sin = jnp.sin(freqs)[None, :, None, :]
    x1, x2 = x[..., 0::2], x[..., 1::2]
    out = jnp.stack([x1 * cos - x2 * sin, x1 * sin + x2 * cos], axis=-1)
    return out.reshape(x.shape)


def partial_rope(x, pos):
    r = ROPE_DIMS
    inv = ROPE_THETA ** (-jnp.arange(0, r, 2, dtype=jnp.float32) / r)
    freqs = pos[:, None] * inv[None, :]
    emb = jnp.concatenate([freqs, freqs], axis=-1)
    cos = jnp.cos(emb)[None, :, None, None, :]
    sin = jnp.sin(emb)[None, :, None, None, :]
    xr, xp = x[..., :r], x[..., r:]
    rot = jnp.concatenate([-xr[..., r // 2:], xr[..., :r // 2]], axis=-1)
    return jnp.concatenate([xr * cos + rot * sin, xp], axis=-1)


def retention_group(q, k, v, g, gn_w):
    B, T = q.shape[0], q.shape[1]
    f32 = jnp.float32
    pos = jnp.arange(T, dtype=f32)
    q = retention_rotate(q.reshape(B, T, RET_HEADS, RET_DK).astype(f32), pos)
    k = retention_rotate(k.reshape(B, T, RET_HEADS, RET_DK).astype(f32), pos) * (RET_DK ** -0.5)
    v = v.reshape(B, T, RET_HEADS, RET_DV).astype(f32)

    n_chunks = -(-T // CHUNK)
    pad = n_chunks * CHUNK - T

    def to_chunks(t):
        t = jnp.pad(t, ((0, 0), (pad, 0), (0, 0), (0, 0)))
        return t.reshape(B, n_chunks, CHUNK, t.shape[2], t.shape[3]).transpose(0, 3, 1, 2, 4)

    qc, kc, vc = to_chunks(q), to_chunks(k), to_chunks(v)

    log_gamma = jnp.log(1.0 - 2.0 ** (-5.0 - jnp.arange(RET_HEADS, dtype=f32)))
    idx = jnp.arange(CHUNK, dtype=f32)
    rel = idx[:, None] - idx[None, :]
    decay_intra = jnp.where(rel >= 0, jnp.exp(log_gamma[:, None, None] * jnp.maximum(rel, 0.0)), 0.0)
    k_decay = jnp.exp(log_gamma[:, None] * (CHUNK - 1 - idx)[None, :])
    q_decay = jnp.exp(log_gamma[:, None] * (idx + 1.0)[None, :])
    chunk_decay = jnp.exp(log_gamma * CHUNK)

    scores = jnp.einsum('bhnid,bhnjd->bhnij', qc, kc) * decay_intra[None, :, None]
    o_intra = jnp.einsum('bhnij,bhnje->bhnie', scores, vc)

    chunk_state = jnp.einsum('bhnjd,bhnje,hj->bhnde', kc, vc, k_decay)

    def step(R, s):
        return R * chunk_decay[None, :, None, None] + s, R

    R0 = jnp.zeros((B, RET_HEADS, RET_DK, RET_DV), f32)
    _, R_prev = lax.scan(step, R0, jnp.moveaxis(chunk_state, 2, 0))
    R_prev = jnp.moveaxis(R_prev, 0, 2)
    o_cross = jnp.einsum('bhnid,bhnde,hi->bhnie', qc, R_prev, q_decay)

    o = (o_intra + o_cross).transpose(0, 2, 3, 1, 4).reshape(B, n_chunks * CHUNK, RET_HEADS, RET_DV)[:, pad:]
    mu = jnp.mean(o, axis=-1, keepdims=True)
    var = jnp.mean((o - mu) ** 2, axis=-1, keepdims=True)
    o = ((o - mu) * lax.rsqrt(var + EPS)).reshape(B, T, RET_WIDTH) * gn_w.astype(f32)
    return (jax.nn.silu(g.astype(f32)) * o).astype(g.dtype)


def diff_attention_group(q, k, v, subln_w, lq1, lk1, lq2, lk2, lambda_init):
    B, T = q.shape[0], q.shape[1]
    f32 = jnp.float32
    pos = jnp.arange(T, dtype=f32)
    q = partial_rope(q.reshape(B, T, DIFF_HEADS, 2, DIFF_D).astype(f32), pos)
    k = partial_rope(k.reshape(B, T, DIFF_HEADS, 2, DIFF_D).astype(f32), pos)
    v = v.reshape(B, T, DIFF_HEADS, DIFF_DV).astype(f32)

    lam = (jnp.exp(jnp.sum(lq1.astype(f32) * lk1.astype(f32)))
           - jnp.exp(jnp.sum(lq2.astype(f32) * lk2.astype(f32))) + lambda_init)

    n_blocks = -(-T // CHUNK)
    pad = n_blocks * CHUNK - T
    q = jnp.pad(q, ((0, 0), (pad, 0), (0, 0), (0, 0), (0, 0))).transpose(0, 2, 3, 1, 4)
    k = jnp.pad(k, ((0, 0), (pad, 0), (0, 0), (0, 0), (0, 0))).transpose(0, 2, 3, 1, 4)
    v = jnp.pad(v, ((0, 0), (pad, 0), (0, 0), (0, 0))).transpose(0, 2, 1, 3)
    scale = DIFF_D ** -0.5

    outs = []
    for nb in range(n_blocks):
        n_keys = (nb + 1) * CHUNK
        qb = q[:, :, :, nb * CHUNK:(nb + 1) * CHUNK]
        s = jnp.einsum('bhmid,bhmjd->bhmij', qb, k[:, :, :, :n_keys]) * scale
        qi = nb * CHUNK + jnp.arange(CHUNK)
        kj = jnp.arange(n_keys)
        allowed = (kj[None, :] <= qi[:, None]) & (kj[None, :] >= pad)
        p = jax.nn.softmax(jnp.where(allowed, s, -1e30), axis=-1)
        a = p[:, :, 0] - lam * p[:, :, 1]
        outs.append(jnp.einsum('bhij,bhje->bhie', a, v[:, :, :n_keys]))
    o = jnp.concatenate(outs, axis=2)[:, :, pad:].transpose(0, 2, 1, 3)
    o = o * lax.rsqrt(jnp.mean(o * o, axis=-1, keepdims=True) + EPS) * subln_w.astype(f32)
    o = o * (1.0 - lambda_init)
    return o.reshape(B, T, DIFF_WIDTH).astype(q.dtype)


def setup_inputs(seed: int = 0) -> dict:
    key = jax.random.key(seed)
    ks = jax.random.split(key, 24)
    f32 = jnp.float32
    nrm = lambda k, shape, s: jax.random.normal(k, shape, f32) * s
    gain = lambda k, shape: 1.0 + 0.1 * jax.random.normal(k, shape, f32)
    return {
        "x": jax.random.normal(ks[0], (BATCH, SEQ, D_MODEL), f32),
        "meta_tokens": nrm(ks[1], (N_META, D_MODEL), 1.0),
        "ffn1_norm": gain(ks[2], (DEPTH, D_MODEL)),
        "ffn1_w_gate": nrm(ks[3], (DEPTH, D_MODEL, D_FF), D_MODEL ** -0.5),
        "ffn1_w_up": nrm(ks[4], (DEPTH, D_MODEL, D_FF), D_MODEL ** -0.5),
        "ffn1_w_down": nrm(ks[5], (DEPTH, D_FF, D_MODEL), D_FF ** -0.5),
        "mix_norm": gain(ks[6], (DEPTH, D_MODEL)),
        "w_in": nrm(ks[7], (DEPTH, D_MODEL, D_IN), D_MODEL ** -0.5),
        "ret_gn_w": gain(ks[8], (DEPTH, RET_WIDTH)),
        "diff_subln_w": gain(ks[9], (DEPTH, DIFF_DV)),
        "diff_lambda_q1": nrm(ks[10], (DEPTH, DIFF_D), 0.1),
        "diff_lambda_k1": nrm(ks[11], (DEPTH, DIFF_D), 0.1),
        "diff_lambda_q2": nrm(ks[12], (DEPTH, DIFF_D), 0.1),
        "diff_lambda_k2": nrm(ks[13], (DEPTH, DIFF_D), 0.1),
        "w_out": nrm(ks[14], (DEPTH, D_MIX, D_MODEL), D_MIX ** -0.5),
        "ffn2_norm": gain(ks[15], (DEPTH, D_MODEL)),
        "ffn2_w_gate": nrm(ks[16], (DEPTH, D_MODEL, D_FF), D_MODEL ** -0.5),
        "ffn2_w_up": nrm(ks[17], (DEPTH, D_MODEL, D_FF), D_MODEL ** -0.5),
        "ffn2_w_down": nrm(ks[18], (DEPTH, D_FF, D_MODEL), D_FF ** -0.5),
        "final_norm": gain(ks[19], (D_MODEL,)),
    }


def reference(x, meta_tokens, ffn1_norm, ffn1_w_gate, ffn1_w_up, ffn1_w_down,
              mix_norm, w_in, ret_gn_w, diff_subln_w, diff_lambda_q1, diff_lambda_k1,
              diff_lambda_q2, diff_lambda_k2, w_out, ffn2_norm, ffn2_w_gate,
              ffn2_w_up, ffn2_w_down, final_norm):
    B = x.shape[0]
    meta = jnp.broadcast_to(meta_tokens.astype(x.dtype)[None], (B, N_META, D_MODEL))
    h = jnp.concatenate([meta, x], axis=1)
    offs = np.cumsum((0,) + IN_SIZES)

    for l in range(DEPTH):
        h = h + 0.5 * swiglu(rmsnorm(h, ffn1_norm[l]), ffn1_w_gate[l], ffn1_w_up[l], ffn1_w_down[l])

        proj = rmsnorm(h, mix_norm[l]) @ w_in[l]
        parts = [proj[..., int(offs[i]):int(offs[i + 1])] for i in range(len(IN_SIZES))]
        rq, rk, rv, rg, dq, dk, dv = parts
        lambda_init = 0.8 - 0.6 * math.exp(-0.3 * l)
        y_ret = retention_group(rq, rk, rv, rg, ret_gn_w[l])
        y_diff = diff_attention_group(dq, dk, dv, diff_subln_w[l], diff_lambda_q1[l],
                                      diff_lambda_k1[l], diff_lambda_q2[l], diff_lambda_k2[l],
                                      lambda_init)
        h = h + jnp.concatenate([y_ret, y_diff], axis=-1) @ w_out[l]

        h = h + 0.5 * swiglu(rmsnorm(h, ffn2_norm[l]), ffn2_w_gate[l], ffn2_w_up[l], ffn2_w_down[l])

    h = rmsnorm(h, final_norm)
    return h[:, N_META:]
```

```python
import functools
import math

import numpy as np
import jax
import jax.numpy as jnp
from jax import lax
from jax.experimental import pallas as pl
from jax.experimental.pallas import tpu as pltpu

D_MODEL = 1024
BATCH = 8
SEQ = 2048
DEPTH = 2
N_META = 16
CHUNK = 128
RET_HEADS = 4
RET_DK = 128
RET_WIDTH = 512
RET_THETA = 10000.0
DIFF_HEADS = 4
DIFF_D = 64
DIFF_DV = 128
DIFF_WIDTH = 512
ROPE_THETA = 500000.0
ROPE_DIMS = 16
D_FF = 2816
EPS = 1e-6
D_IN = 3584

T_REAL = SEQ + N_META
N_CHUNKS = -(-T_REAL // CHUNK)
T_PAD = N_CHUNKS * CHUNK
PAD = T_PAD - T_REAL
ROWS = BATCH * T_PAD
ROW_TILE = T_PAD // 4
N_TAB = 10
LANES = 128
VMEM_LIMIT = 56 * 1024 * 1024
NEG = -1e30

F32 = jnp.float32
BF16 = jnp.bfloat16


def _rms(x, g):
    return x * lax.rsqrt(jnp.mean(x * x, axis=-1, keepdims=True) + EPS) * g


def _resident(shape):
    return pl.BlockSpec(shape, lambda *_: (0,) * len(shape), pipeline_mode=pl.Buffered(1))


def _ffn_kernel(*refs, mix, final):
    refs = list(refs)
    h_ref = refs.pop(0)
    x = h_ref[...]
    if mix:
        yr_ref, yd_ref, wo_ref = refs.pop(0), refs.pop(0), refs.pop(0)
        x = (x + jnp.dot(yr_ref[...], wo_ref[:RET_WIDTH, :], preferred_element_type=F32)
             + jnp.dot(yd_ref[...], wo_ref[RET_WIDTH:, :], preferred_element_type=F32))
    g_ref, wg_ref, wu_ref, wd_ref = refs.pop(0), refs.pop(0), refs.pop(0), refs.pop(0)
    xn = _rms(x, g_ref[...]).astype(BF16)
    gate = jnp.dot(xn, wg_ref[...], preferred_element_type=F32)
    up = jnp.dot(xn, wu_ref[...], preferred_element_type=F32)
    act = (jax.nn.silu(gate) * up).astype(BF16)
    y = x + 0.5 * jnp.dot(act, wd_ref[...], preferred_element_type=F32)
    if final:
        fn_ref = refs.pop(0)
        y = _rms(y, fn_ref[...])
    o_ref = refs.pop(0)
    o_ref[...] = y


def _ffn(h, norm_g, wg, wu, wd, mix=None, final_g=None):
    row = lambda i: (i, 0)
    args = [h]
    specs = [pl.BlockSpec((ROW_TILE, D_MODEL), row)]
    if mix is not None:
        y_ret, y_diff, w_out = mix
        args += [y_ret, y_diff, w_out]
        specs += [pl.BlockSpec((ROW_TILE, RET_WIDTH), row),
                  pl.BlockSpec((ROW_TILE, DIFF_WIDTH), row),
                  _resident((D_MODEL, D_MODEL))]
    args += [norm_g, wg, wu, wd]
    specs += [_resident((1, D_MODEL)), _resident((D_MODEL, D_FF)),
              _resident((D_MODEL, D_FF)), _resident((D_FF, D_MODEL))]
    if final_g is not None:
        args.append(final_g)
        specs.append(_resident((1, D_MODEL)))
    return pl.pallas_call(
        functools.partial(_ffn_kernel, mix=mix is not None, final=final_g is not None),
        grid=(ROWS // ROW_TILE,),
        in_specs=specs,
        out_specs=pl.BlockSpec((ROW_TILE, D_MODEL), row),
        out_shape=jax.ShapeDtypeStruct((ROWS, D_MODEL), F32),
        compiler_params=pltpu.CompilerParams(
            dimension_semantics=("parallel",), vmem_limit_bytes=VMEM_LIMIT),
        name="ffn",
    )(*args)


def _inproj_kernel(h_ref, g_ref, w_ref, tab_ref, o_ref):
    xn = _rms(h_ref[...], g_ref[...]).astype(BF16)
    p = jnp.dot(xn, w_ref[...], preferred_element_type=F32)

    def tab(i):
        return tab_ref[:, i * LANES:(i + 1) * LANES]

    for base, t0 in ((0, 0), (RET_WIDTH, 2)):
        for hd in range(RET_HEADS):
            sl = slice(base + hd * LANES, base + (hd + 1) * LANES)
            xs = p[:, sl]
            o_ref[:, sl] = (xs * tab(t0) + pltpu.roll(xs, 64, 1) * tab(t0 + 1)).astype(BF16)
    o_ref[:, 2 * RET_WIDTH:4 * RET_WIDTH] = p[:, 2 * RET_WIDTH:4 * RET_WIDTH].astype(BF16)
    dq0 = 4 * RET_WIDTH
    for base, t0 in ((dq0, 4), (dq0 + DIFF_WIDTH, 7)):
        for hd in range(DIFF_HEADS):
            sl = slice(base + hd * LANES, base + (hd + 1) * LANES)
            xs = p[:, sl]
            o_ref[:, sl] = (xs * tab(t0) + pltpu.roll(xs, 8, 1) * tab(t0 + 1)
                            + pltpu.roll(xs, LANES - 8, 1) * tab(t0 + 2)).astype(BF16)
    o_ref[:, dq0 + 2 * DIFF_WIDTH:] = p[:, dq0 + 2 * DIFF_WIDTH:].astype(BF16)


def _inproj(h, norm_g, w_in, tab):
    return pl.pallas_call(
        _inproj_kernel,
        grid=(ROWS // ROW_TILE,),
        in_specs=[pl.BlockSpec((ROW_TILE, D_MODEL), lambda i: (i, 0)),
                  _resident((1, D_MODEL)),
                  _resident((D_MODEL, D_IN)),
                  pl.BlockSpec((ROW_TILE, N_TAB * LANES), lambda i: (i % (T_PAD // ROW_TILE), 0))],
        out_specs=pl.BlockSpec((ROW_TILE, D_IN), lambda i: (i, 0)),
        out_shape=jax.ShapeDtypeStruct((ROWS, D_IN), BF16),
        compiler_params=pltpu.CompilerParams(
            dimension_semantics=("parallel",), vmem_limit_bytes=VMEM_LIMIT),
        name="inproj",
    )(h, norm_g, w_in, tab)


def _retention_kernel(cd_ref, q_ref, k_ref, v_ref, g_ref, dintra_ref, qdec_ref, kdec_ref,
                      gnw_ref, o_ref, state_ref):
    @pl.when(pl.program_id(0) == 0)
    def _():
        state_ref[...] = jnp.zeros_like(state_ref)

    nt = (((1,), (1,)), ((), ()))
    tn = (((0,), (0,)), ((), ()))

    def per_batch(b, carry):
        for hd in range(RET_HEADS):
            sl = slice(hd * LANES, (hd + 1) * LANES)
            q = q_ref[b, 0, :, sl]
            k = k_ref[b, 0, :, sl]
            v = v_ref[b, 0, :, sl]
            s = lax.dot_general(q, k, nt, preferred_element_type=F32) * dintra_ref[hd]
            qd = (q.astype(F32) * qdec_ref[hd]).astype(BF16)
            kd = (k.astype(F32) * kdec_ref[hd]).astype(BF16)
            r = state_ref[b, hd]
            o = (jnp.dot(s.astype(BF16), v, preferred_element_type=F32)
                 + jnp.dot(qd, r.astype(BF16), preferred_element_type=F32))
            state_ref[b, hd] = r * cd_ref[hd] + lax.dot_general(kd, v, tn, preferred_element_type=F32)
            mu = jnp.mean(o, axis=-1, keepdims=True)
            d = o - mu
            var = jnp.mean(d * d, axis=-1, keepdims=True)
            on = d * lax.rsqrt(var + EPS) * gnw_ref[:, sl]
            o_ref[b, 0, :, sl] = (jax.nn.silu(g_ref[b, 0, :, sl].astype(F32)) * on).astype(BF16)
        return carry

    lax.fori_loop(0, BATCH, per_batch, 0)


def _retention(proj4, cd, dintra, qdec, kdec, gn_w):
    blk = lambda c: pl.BlockSpec((BATCH, 1, CHUNK, RET_WIDTH), lambda n, c=c: (0, n, 0, c))
    const3 = pl.BlockSpec((RET_HEADS, CHUNK, LANES), lambda n: (0, 0, 0))
    return pl.pallas_call(
        _retention_kernel,
        grid=(N_CHUNKS,),
        in_specs=[pl.BlockSpec(memory_space=pltpu.SMEM),
                  blk(0), blk(1), blk(2), blk(3), const3, const3, const3,
                  pl.BlockSpec((1, RET_WIDTH), lambda n: (0, 0))],
        out_specs=pl.BlockSpec((BATCH, 1, CHUNK, RET_WIDTH), lambda n: (0, n, 0, 0)),
        out_shape=jax.ShapeDtypeStruct((BATCH, N_CHUNKS, CHUNK, RET_WIDTH), BF16),
        scratch_shapes=[pltpu.VMEM((BATCH, RET_HEADS, RET_DK, LANES), F32)],
        compiler_params=pltpu.CompilerParams(
            dimension_semantics=("arbitrary",), vmem_limit_bytes=VMEM_LIMIT),
        name="retention",
    )(cd, proj4, proj4, proj4, proj4, dintra, qdec, kdec, gn_w)


def _diffattn_kernel(q_ref, k_ref, v_ref, w_ref, lq1_ref, lk1_ref, lq2_ref, lk2_ref,
                     o_ref, s_ref, *, lambda_init):
    lam = (jnp.exp(jnp.sum(lq1_ref[...] * lk1_ref[...], axis=-1, keepdims=True))
           - jnp.exp(jnp.sum(lq2_ref[...] * lk2_ref[...], axis=-1, keepdims=True))
           + lambda_init)
    nt = (((1,), (1,)), ((), ()))
    lane = lax.broadcasted_iota(jnp.int32, (CHUNK, LANES), 1)
    row2 = lax.broadcasted_iota(jnp.int32, (2 * CHUNK, LANES), 0) % CHUNK
    col2 = lax.broadcasted_iota(jnp.int32, (2 * CHUNK, LANES), 1)
    row1 = lax.broadcasted_iota(jnp.int32, (CHUNK, LANES), 0)

    for hd in range(DIFF_HEADS):
        sl = slice(hd * LANES, (hd + 1) * LANES)

        def per_qblock(i, carry, sl=sl):
            r0 = pl.multiple_of(i * CHUNK, CHUNK)
            q = q_ref[0, pl.ds(r0, CHUNK), sl]
            zero = jnp.zeros_like(q)
            qq = jnp.concatenate([jnp.where(lane < DIFF_D, q, zero),
                                  jnp.where(lane >= DIFF_D, q, zero)], axis=0)
            qpos = r0 + row2

            def scores(j, m):
                c0 = pl.multiple_of(j * CHUNK, CHUNK)
                kb = k_ref[0, pl.ds(c0, CHUNK), sl]
                s = lax.dot_general(qq, kb, nt, preferred_element_type=F32)
                kpos = c0 + col2
                s = jnp.where((kpos <= qpos) & (kpos >= PAD), s, NEG)
                s_ref[j] = s
                return jnp.maximum(m, s)

            m = lax.fori_loop(0, i + 1, scores, jnp.full((2 * CHUNK, LANES), NEG, F32))
            mrow = jnp.max(m, axis=-1, keepdims=True)

            def expo(j, l):
                p = jnp.exp2(s_ref[j] - mrow)
                s_ref[j] = p
                return l + p

            l = lax.fori_loop(0, i + 1, expo, jnp.zeros((2 * CHUNK, LANES), F32))
            lrow = jnp.sum(l, axis=-1, keepdims=True)
            c_pos = 1.0 / lrow[:CHUNK]
            c_neg = lam / lrow[CHUNK:]

            def pv(j, acc):
                c0 = pl.multiple_of(j * CHUNK, CHUNK)
                p = s_ref[j]
                a = (p[:CHUNK] * c_pos - p[CHUNK:] * c_neg).astype(BF16)
                return acc + jnp.dot(a, v_ref[0, pl.ds(c0, CHUNK), sl], preferred_element_type=F32)

            o = lax.fori_loop(0, i + 1, pv, jnp.zeros((CHUNK, LANES), F32))
            o = o * lax.rsqrt(jnp.mean(o * o, axis=-1, keepdims=True) + EPS) * w_ref[...]
            o = o * (1.0 - lambda_init)
            o = jnp.where(r0 + row1 >= PAD, o, 0.0)
            o_ref[0, pl.ds(r0, CHUNK), sl] = o.astype(BF16)
            return carry

        lax.fori_loop(0, N_CHUNKS, per_qblock, 0)


def _diffattn(proj3, subln_w, lq1, lk1, lq2, lk2, lambda_init):
    blk = lambda c: pl.BlockSpec((1, T_PAD, DIFF_WIDTH), lambda b, c=c: (b, 0, c))
    vec = lambda n: pl.BlockSpec((1, n), lambda b: (0, 0))
    return pl.pallas_call(
        functools.partial(_diffattn_kernel, lambda_init=lambda_init),
        grid=(BATCH,),
        in_specs=[blk(4), blk(5), blk(6), vec(DIFF_DV), vec(DIFF_D), vec(DIFF_D), vec(DIFF_D),
                  vec(DIFF_D)],
        out_specs=pl.BlockSpec((1, T_PAD, DIFF_WIDTH), lambda b: (b, 0, 0)),
        out_shape=jax.ShapeDtypeStruct((BATCH, T_PAD, DIFF_WIDTH), BF16),
        scratch_shapes=[pltpu.VMEM((N_CHUNKS, 2 * CHUNK, LANES), F32)],
        compiler_params=pltpu.CompilerParams(
            dimension_semantics=("parallel",), vmem_limit_bytes=VMEM_LIMIT),
        name="diffattn",
    )(proj3, proj3, proj3, subln_w, lq1, lk1, lq2, lk2)


def _rotary_tables():
    pos = jnp.arange(T_PAD, dtype=F32) - float(PAD)
    angle = RET_THETA ** (-jnp.linspace(0.0, 1.0, RET_DK // 2, dtype=F32))
    fr = pos[:, None] * angle[None, :]
    c, s = jnp.cos(fr), jnp.sin(fr)
    cos_r = jnp.concatenate([c, c], axis=-1)
    sin_r = jnp.concatenate([-s, s], axis=-1)
    ks = RET_DK ** -0.5
    inv = ROPE_THETA ** (-jnp.arange(0, ROPE_DIMS, 2, dtype=F32) / ROPE_DIMS)
    fq = pos[:, None] * inv[None, :]
    emb = jnp.concatenate([fq, fq], axis=-1)
    ce, se = jnp.cos(emb), jnp.sin(emb)
    half = ROPE_DIMS // 2
    rest = DIFF_D - ROPE_DIMS
    ones = jnp.ones((T_PAD, rest), F32)
    zeros = lambda n: jnp.zeros((T_PAD, n), F32)
    two = lambda a: jnp.concatenate([a, a], axis=-1)
    c_d = two(jnp.concatenate([ce, ones], axis=-1))
    s_lo = two(jnp.concatenate([zeros(half), se[:, half:], zeros(rest)], axis=-1))
    s_hi = two(jnp.concatenate([-se[:, :half], zeros(half + rest)], axis=-1))
    qs = (DIFF_D ** -0.5) * math.log2(math.e)
    return jnp.concatenate([cos_r, sin_r, cos_r * ks, sin_r * ks,
                            c_d * qs, s_lo * qs, s_hi * qs, c_d, s_lo, s_hi], axis=-1)


def _retention_consts():
    log_gamma = jnp.log(1.0 - 2.0 ** (-5.0 - jnp.arange(RET_HEADS, dtype=F32)))
    idx = jnp.arange(CHUNK, dtype=F32)
    rel = idx[:, None] - idx[None, :]
    dintra = jnp.where(rel >= 0, jnp.exp(log_gamma[:, None, None] * jnp.maximum(rel, 0.0)), 0.0)
    k_decay = jnp.exp(log_gamma[:, None] * (CHUNK - 1 - idx)[None, :])
    q_decay = jnp.exp(log_gamma[:, None] * (idx + 1.0)[None, :])
    cd = jnp.exp(log_gamma * CHUNK)
    bc = lambda a: jnp.broadcast_to(a[:, :, None], (RET_HEADS, CHUNK, LANES))
    return cd, dintra, bc(q_decay), bc(k_decay)


def _win_column_order():
    evens_then_odds = np.concatenate([np.arange(0, RET_DK, 2), np.arange(1, RET_DK, 2)])
    idx = np.arange(D_IN)
    for c in range(2 * RET_HEADS * RET_DK):
        idx[c] = (c // RET_DK) * RET_DK + evens_then_odds[c % RET_DK]
    return idx


def kernel(x, meta_tokens, ffn1_norm, ffn1_w_gate, ffn1_w_up, ffn1_w_down, mix_norm, w_in, ret_gn_w, diff_subln_w, diff_lambda_q1, diff_lambda_k1, diff_lambda_q2, diff_lambda_k2, w_out, ffn2_norm, ffn2_w_gate, ffn2_w_up, ffn2_w_down, final_norm):
    meta = jnp.broadcast_to(meta_tokens.astype(F32)[None], (BATCH, N_META, D_MODEL))
    h = jnp.concatenate([jnp.zeros((BATCH, PAD, D_MODEL), F32), meta, x.astype(F32)], axis=1)
    h = h.reshape(ROWS, D_MODEL)

    tab = _rotary_tables()
    cd, dintra, qdec, kdec = _retention_consts()
    col_order = _win_column_order()
    vec = lambda a: a.astype(F32).reshape(1, -1)

    for l in range(DEPTH):
        lambda_init = 0.8 - 0.6 * math.exp(-0.3 * l)
        h = _ffn(h, vec(ffn1_norm[l]), ffn1_w_gate[l].astype(BF16), ffn1_w_up[l].astype(BF16),
                 ffn1_w_down[l].astype(BF16))
        proj = _inproj(h, vec(mix_norm[l]), w_in[l][:, col_order].astype(BF16), tab)
        y_ret = _retention(proj.reshape(BATCH, N_CHUNKS, CHUNK, D_IN), cd, dintra, qdec, kdec,
                           vec(ret_gn_w[l]))
        y_diff = _diffattn(proj.reshape(BATCH, T_PAD, D_IN), vec(diff_subln_w[l]),
                           vec(diff_lambda_q1[l]), vec(diff_lambda_k1[l]),
                           vec(diff_lambda_q2[l]), vec(diff_lambda_k2[l]), lambda_init)
        h = _ffn(h, vec(ffn2_norm[l]), ffn2_w_gate[l].astype(BF16), ffn2_w_up[l].astype(BF16),
                 ffn2_w_down[l].astype(BF16),
                 mix=(y_ret.reshape(ROWS, RET_WIDTH), y_diff.reshape(ROWS, DIFF_WIDTH),
                      w_out[l].astype(BF16)),
                 final_g=vec(final_norm) if l == DEPTH - 1 else None)

    return h.reshape(BATCH, T_PAD, D_MODEL)[:, PAD + N_META:].astype(x.dtype)
```

```python
import functools
import math

import numpy as np
import jax
import jax.numpy as jnp
from jax import lax
from jax.experimental import pallas as pl
from jax.experimental.pallas import tpu as pltpu

D_MODEL = 1024
BATCH = 8
SEQ = 2048
DEPTH = 2
N_META = 16
CHUNK = 128
RET_HEADS = 4
RET_DK = 128
RET_WIDTH = 512
RET_THETA = 10000.0
DIFF_HEADS = 4
DIFF_D = 64
DIFF_DV = 128
DIFF_WIDTH = 512
ROPE_THETA = 500000.0
ROPE_DIMS = 16
D_FF = 2816
EPS = 1e-6
D_IN = 3584

T_REAL = SEQ + N_META
N_CHUNKS = -(-T_REAL // CHUNK)
T_PAD = N_CHUNKS * CHUNK
PAD = T_PAD - T_REAL
ROWS = BATCH * T_PAD
ROW_TILE = T_PAD // 4
N_TAB = 10
LANES = 128
VMEM_LIMIT = 56 * 1024 * 1024
NEG = -1e30

F32 = jnp.float32
BF16 = jnp.bfloat16


def _rms(x, g):
    return x * lax.rsqrt(jnp.mean(x * x, axis=-1, keepdims=True) + EPS) * g


def _resident(shape):
    return pl.BlockSpec(shape, lambda *_: (0,) * len(shape), pipeline_mode=pl.Buffered(1))


def _ffn_kernel(*refs, mix, final):
    refs = list(refs)
    h_ref = refs.pop(0)
    x = h_ref[...]
    if mix:
        yr_ref, yd_ref, wo_ref = refs.pop(0), refs.pop(0), refs.pop(0)
        x = (x + jnp.dot(yr_ref[...], wo_ref[:RET_WIDTH, :], preferred_element_type=F32)
             + jnp.dot(yd_ref[...], wo_ref[RET_WIDTH:, :], preferred_element_type=F32))
    g_ref, wg_ref, wu_ref, wd_ref = refs.pop(0), refs.pop(0), refs.pop(0), refs.pop(0)
    xn = _rms(x, g_ref[...]).astype(BF16)
    gate = jnp.dot(xn, wg_ref[...], preferred_element_type=F32)
    up = jnp.dot(xn, wu_ref[...], preferred_element_type=F32)
    act = (jax.nn.silu(gate) * up).astype(BF16)
    y = x + 0.5 * jnp.dot(act, wd_ref[...], preferred_element_type=F32)
    if final:
        fn_ref = refs.pop(0)
        y = _rms(y, fn_ref[...])
    o_ref = refs.pop(0)
    o_ref[...] = y


def _ffn(h, norm_g, wg, wu, wd, mix=None, final_g=None):
    row = lambda i: (i, 0)
    args = [h]
    specs = [pl.BlockSpec((ROW_TILE, D_MODEL), row)]
    if mix is not None:
        y_ret, y_diff, w_out = mix
        args += [y_ret, y_diff, w_out]
        specs += [pl.BlockSpec((ROW_TILE, RET_WIDTH), row),
                  pl.BlockSpec((ROW_TILE, DIFF_WIDTH), row),
                  _resident((D_MODEL, D_MODEL))]
    args += [norm_g, wg, wu, wd]
    specs += [_resident((1, D_MODEL)), _resident((D_MODEL, D_FF)),
              _resident((D_MODEL, D_FF)), _resident((D_FF, D_MODEL))]
    if final_g is not None:
        args.append(final_g)
        specs.append(_resident((1, D_MODEL)))
    return pl.pallas_call(
        functools.partial(_ffn_kernel, mix=mix is not None, final=final_g is not None),
        grid=(ROWS // ROW_TILE,),
        in_specs=specs,
        out_specs=pl.BlockSpec((ROW_TILE, D_MODEL), row),
        out_shape=jax.ShapeDtypeStruct((ROWS, D_MODEL), F32),
        compiler_params=pltpu.CompilerParams(
            dimension_semantics=("parallel",), vmem_limit_bytes=VMEM_LIMIT),
        name="ffn",
    )(*args)


def _inproj_kernel(h_ref, g_ref, w_ref, tab_ref, o_ref):
    xn = _rms(h_ref[...], g_ref[...]).astype(BF16)
    p = jnp.dot(xn, w_ref[...], preferred_element_type=F32)

    def tab(i):
        return tab_ref[:, i * LANES:(i + 1) * LANES]

    for base, t0 in ((0, 0), (RET_WIDTH, 2)):
        for hd in range(RET_HEADS):
            sl = slice(base + hd * LANES, base + (hd + 1) * LANES)
            xs = p[:, sl]
            o_ref[:, sl] = (xs * tab(t0) + pltpu.roll(xs, 64, 1) * tab(t0 + 1)).astype(BF16)
    o_ref[:, 2 * RET_WIDTH:4 * RET_WIDTH] = p[:, 2 * RET_WIDTH:4 * RET_WIDTH].astype(BF16)
    dq0 = 4 * RET_WIDTH
    for base, t0 in ((dq0, 4), (dq0 + DIFF_WIDTH, 7)):
        for hd in range(DIFF_HEADS):
            sl = slice(base + hd * LANES, base + (hd + 1) * LANES)
            xs = p[:, sl]
            o_ref[:, sl] = (xs * tab(t0) + pltpu.roll(xs, 8, 1) * tab(t0 + 1)
                            + pltpu.roll(xs, LANES - 8, 1) * tab(t0 + 2)).astype(BF16)
    o_ref[:, dq0 + 2 * DIFF_WIDTH:] = p[:, dq0 + 2 * DIFF_WIDTH:].astype(BF16)


def _inproj(h, norm_g, w_in, tab):
    return pl.pallas_call(
        _inproj_kernel,
        grid=(ROWS // ROW_TILE,),
        in_specs=[pl.BlockSpec((ROW_TILE, D_MODEL), lambda i: (i, 0)),
                  _resident((1, D_MODEL)),
                  _resident((D_MODEL, D_IN)),
                  pl.BlockSpec((ROW_TILE, N_TAB * LANES), lambda i: (i % (T_PAD // ROW_TILE), 0))],
        out_specs=pl.BlockSpec((ROW_TILE, D_IN), lambda i: (i, 0)),
        out_shape=jax.ShapeDtypeStruct((ROWS, D_IN), BF16),
        compiler_params=pltpu.CompilerParams(
            dimension_semantics=("parallel",), vmem_limit_bytes=VMEM_LIMIT),
        name="inproj",
    )(h, norm_g, w_in, tab)


def _retention_kernel(cd_ref, q_ref, k_ref, v_ref, g_ref, dintra_ref, qdec_ref, kdec_ref,
                      gnw_ref, o_ref, state_ref):
    @pl.when(pl.program_id(0) == 0)
    def _():
        state_ref[...] = jnp.zeros_like(state_ref)

    nt = (((1,), (1,)), ((), ()))
    tn = (((0,), (0,)), ((), ()))

    def per_batch(b, carry):
        for hd in range(RET_HEADS):
            sl = slice(hd * LANES, (hd + 1) * LANES)
            q = q_ref[b, 0, :, sl]
            k = k_ref[b, 0, :, sl]
            v = v_ref[b, 0, :, sl]
            s = lax.dot_general(q, k, nt, preferred_element_type=F32) * dintra_ref[hd]
            qd = (q.astype(F32) * qdec_ref[hd]).astype(BF16)
            kd = (k.astype(F32) * kdec_ref[hd]).astype(BF16)
            r = state_ref[b, hd]
            o = (jnp.dot(s.astype(BF16), v, preferred_element_type=F32)
                 + jnp.dot(qd, r.astype(BF16), preferred_element_type=F32))
            state_ref[b, hd] = r * cd_ref[hd] + lax.dot_general(kd, v, tn, preferred_element_type=F32)
            mu = jnp.mean(o, axis=-1, keepdims=True)
            d = o - mu
            var = jnp.mean(d * d, axis=-1, keepdims=True)
            on = d * lax.rsqrt(var + EPS) * gnw_ref[:, sl]
            o_ref[b, 0, :, sl] = (jax.nn.silu(g_ref[b, 0, :, sl].astype(F32)) * on).astype(BF16)
        return carry

    lax.fori_loop(0, BATCH, per_batch, 0)


def _retention(proj4, cd, dintra, qdec, kdec, gn_w):
    blk = lambda c: pl.BlockSpec((BATCH, 1, CHUNK, RET_WIDTH), lambda n, c=c: (0, n, 0, c))
    const3 = pl.BlockSpec((RET_HEADS, CHUNK, LANES), lambda n: (0, 0, 0))
    return pl.pallas_call(
        _retention_kernel,
        grid=(N_CHUNKS,),
        in_specs=[pl.BlockSpec(memory_space=pltpu.SMEM),
                  blk(0), blk(1), blk(2), blk(3), const3, const3, const3,
                  pl.BlockSpec((1, RET_WIDTH), lambda n: (0, 0))],
        out_specs=pl.BlockSpec((BATCH, 1, CHUNK, RET_WIDTH), lambda n: (0, n, 0, 0)),
        out_shape=jax.ShapeDtypeStruct((BATCH, N_CHUNKS, CHUNK, RET_WIDTH), BF16),
        scratch_shapes=[pltpu.VMEM((BATCH, RET_HEADS, RET_DK, LANES), F32)],
        compiler_params=pltpu.CompilerParams(
            dimension_semantics=("arbitrary",), vmem_limit_bytes=VMEM_LIMIT),
        name="retention",
    )(cd, proj4, proj4, proj4, proj4, dintra, qdec, kdec, gn_w)


def _diffattn_kernel(q_ref, k_ref, v_ref, w_ref, lq1_ref, lk1_ref, lq2_ref, lk2_ref,
                     o_ref, *, lambda_init):
    lam = (jnp.exp(jnp.sum(lq1_ref[...] * lk1_ref[...], axis=-1, keepdims=True))
           - jnp.exp(jnp.sum(lq2_ref[...] * lk2_ref[...], axis=-1, keepdims=True))
           + lambda_init)
    nt = (((1,), (1,)), ((), ()))
    lane = lax.broadcasted_iota(jnp.int32, (CHUNK, LANES), 1)
    row2 = lax.broadcasted_iota(jnp.int32, (2 * CHUNK, LANES), 0) % CHUNK
    col2 = lax.broadcasted_iota(jnp.int32, (2 * CHUNK, LANES), 1)
    row1 = lax.broadcasted_iota(jnp.int32, (CHUNK, LANES), 0)

    for i in range(N_CHUNKS):
        nk = (i + 1) * CHUNK
        q = q_ref[0, i * CHUNK:nk, :]
        zero = jnp.zeros_like(q)
        qq = jnp.concatenate([jnp.where(lane < DIFF_D, q, zero),
                              jnp.where(lane >= DIFF_D, q, zero)], axis=0)
        s = lax.dot_general(qq, k_ref[0, :nk, :], nt, preferred_element_type=F32)
        diag = jnp.where(col2 <= row2, s[:, i * CHUNK:], NEG)
        if i == 0:
            s = jnp.where(col2 >= PAD, diag, NEG)
        else:
            first = jnp.where(col2 >= PAD, s[:, :CHUNK], NEG)
            middle = [s[:, CHUNK:i * CHUNK]] if i > 1 else []
            s = jnp.concatenate([first] + middle + [diag], axis=1)
        p = jnp.exp2(s - jnp.max(s, axis=-1, keepdims=True))
        lrow = jnp.sum(p, axis=-1, keepdims=True)
        a = (p[:CHUNK] * (1.0 / lrow[:CHUNK]) - p[CHUNK:] * (lam / lrow[CHUNK:])).astype(BF16)
        o = jnp.dot(a, v_ref[0, :nk, :], preferred_element_type=F32)
        o = o * lax.rsqrt(jnp.mean(o * o, axis=-1, keepdims=True) + EPS) * w_ref[...]
        o = o * (1.0 - lambda_init)
        if i == 0:
            o = jnp.where(row1 >= PAD, o, 0.0)
        o_ref[0, i * CHUNK:nk, :] = o.astype(BF16)


def _diffattn(proj3, subln_w, lq1, lk1, lq2, lk2, lambda_init):
    first = 4 * RET_WIDTH // LANES
    blk = lambda c: pl.BlockSpec((1, T_PAD, LANES), lambda b, hd, c=c: (b, 0, first + c + hd))
    vec = lambda n: pl.BlockSpec((1, n), lambda b, hd: (0, 0))
    return pl.pallas_call(
        functools.partial(_diffattn_kernel, lambda_init=lambda_init),
        grid=(BATCH, DIFF_HEADS),
        in_specs=[blk(0), blk(DIFF_HEADS), blk(2 * DIFF_HEADS), vec(DIFF_DV), vec(DIFF_D),
                  vec(DIFF_D), vec(DIFF_D), vec(DIFF_D)],
        out_specs=pl.BlockSpec((1, T_PAD, LANES), lambda b, hd: (b, 0, hd)),
        out_shape=jax.ShapeDtypeStruct((BATCH, T_PAD, DIFF_WIDTH), BF16),
        compiler_params=pltpu.CompilerParams(
            dimension_semantics=("parallel", "parallel"), vmem_limit_bytes=VMEM_LIMIT),
        name="diffattn",
    )(proj3, proj3, proj3, subln_w, lq1, lk1, lq2, lk2)


def _rotary_tables():
    pos = jnp.arange(T_PAD, dtype=F32) - float(PAD)
    angle = RET_THETA ** (-jnp.linspace(0.0, 1.0, RET_DK // 2, dtype=F32))
    fr = pos[:, None] * angle[None, :]
    c, s = jnp.cos(fr), jnp.sin(fr)
    cos_r = jnp.concatenate([c, c], axis=-1)
    sin_r = jnp.concatenate([-s, s], axis=-1)
    ks = RET_DK ** -0.5
    inv = ROPE_THETA ** (-jnp.arange(0, ROPE_DIMS, 2, dtype=F32) / ROPE_DIMS)
    fq = pos[:, None] * inv[None, :]
    emb = jnp.concatenate([fq, fq], axis=-1)
    ce, se = jnp.cos(emb), jnp.sin(emb)
    half = ROPE_DIMS // 2
    rest = DIFF_D - ROPE_DIMS
    ones = jnp.ones((T_PAD, rest), F32)
    zeros = lambda n: jnp.zeros((T_PAD, n), F32)
    two = lambda a: jnp.concatenate([a, a], axis=-1)
    c_d = two(jnp.concatenate([ce, ones], axis=-1))
    s_lo = two(jnp.concatenate([zeros(half), se[:, half:], zeros(rest)], axis=-1))
    s_hi = two(jnp.concatenate([-se[:, :half], zeros(half + rest)], axis=-1))
    qs = (DIFF_D ** -0.5) * math.log2(math.e)
    return jnp.concatenate([cos_r, sin_r, cos_r * ks, sin_r * ks,
                            c_d * qs, s_lo * qs, s_hi * qs, c_d, s_lo, s_hi], axis=-1)


def _retention_consts():
    log_gamma = jnp.log(1.0 - 2.0 ** (-5.0 - jnp.arange(RET_HEADS, dtype=F32)))
    idx = jnp.arange(CHUNK, dtype=F32)
    rel = idx[:, None] - idx[None, :]
    dintra = jnp.where(rel >= 0, jnp.exp(log_gamma[:, None, None] * jnp.maximum(rel, 0.0)), 0.0)
    k_decay = jnp.exp(log_gamma[:, None] * (CHUNK - 1 - idx)[None, :])
    q_decay = jnp.exp(log_gamma[:, None] * (idx + 1.0)[None, :])
    cd = jnp.exp(log_gamma * CHUNK)
    bc = lambda a: jnp.broadcast_to(a[:, :, None], (RET_HEADS, CHUNK, LANES))
    return cd, dintra, bc(q_decay), bc(k_decay)


def _win_column_order():
    evens_then_odds = np.concatenate([np.arange(0, RET_DK, 2), np.arange(1, RET_DK, 2)])
    idx = np.arange(D_IN)
    for c in range(2 * RET_HEADS * RET_DK):
        idx[c] = (c // RET_DK) * RET_DK + evens_then_odds[c % RET_DK]
    return idx


def kernel(x, meta_tokens, ffn1_norm, ffn1_w_gate, ffn1_w_up, ffn1_w_down, mix_norm, w_in, ret_gn_w, diff_subln_w, diff_lambda_q1, diff_lambda_k1, diff_lambda_q2, diff_lambda_k2, w_out, ffn2_norm, ffn2_w_gate, ffn2_w_up, ffn2_w_down, final_norm):
    meta = jnp.broadcast_to(meta_tokens.astype(F32)[None], (BATCH, N_META, D_MODEL))
    h = jnp.concatenate([jnp.zeros((BATCH, PAD, D_MODEL), F32), meta, x.astype(F32)], axis=1)
    h = h.reshape(ROWS, D_MODEL)

    tab = _rotary_tables()
    cd, dintra, qdec, kdec = _retention_consts()
    col_order = _win_column_order()
    vec = lambda a: a.astype(F32).reshape(1, -1)

    for l in range(DEPTH):
        lambda_init = 0.8 - 0.6 * math.exp(-0.3 * l)
        h = _ffn(h, vec(ffn1_norm[l]), ffn1_w_gate[l].astype(BF16), ffn1_w_up[l].astype(BF16),
                 ffn1_w_down[l].astype(BF16))
        proj = _inproj(h, vec(mix_norm[l]), w_in[l][:, col_order].astype(BF16), tab)
        y_ret = _retention(proj.reshape(BATCH, N_CHUNKS, CHUNK, D_IN), cd, dintra, qdec, kdec,
                           vec(ret_gn_w[l]))
        y_diff = _diffattn(proj.reshape(BATCH, T_PAD, D_IN), vec(diff_subln_w[l]),
                           vec(diff_lambda_q1[l]), vec(diff_lambda_k1[l]),
                           vec(diff_lambda_q2[l]), vec(diff_lambda_k2[l]), lambda_init)
        h = _ffn(h, vec(ffn2_norm[l]), ffn2_w_gate[l].astype(BF16), ffn2_w_up[l].astype(BF16),
                 ffn2_w_down[l].astype(BF16),
                 mix=(y_ret.reshape(ROWS, RET_WIDTH), y_diff.reshape(ROWS, DIFF_WIDTH),
                      w_out[l].astype(BF16)),
                 final_g=vec(final_norm) if l == DEPTH - 1 else None)

    return h.reshape(BATCH, T_PAD, D_MODEL)[:, PAD + N_META:].astype(x.dtype)
```

```python
import functools
import math

import numpy as np
import jax
import jax.numpy as jnp
from jax import lax
from jax.experimental import pallas as pl
from jax.experimental.pallas import tpu as pltpu

D_MODEL = 1024
BATCH = 8
SEQ = 2048
DEPTH = 2
N_META = 16
CHUNK = 128
RET_HEADS = 4
RET_DK = 128
RET_WIDTH = 512
RET_THETA = 10000.0
DIFF_HEADS = 4
DIFF_D = 64
DIFF_DV = 128
DIFF_WIDTH = 512
ROPE_THETA = 500000.0
ROPE_DIMS = 16
D_FF = 2816
EPS = 1e-6
D_IN = 3584

T_REAL = SEQ + N_META
N_CHUNKS = -(-T_REAL // CHUNK)
T_PAD = N_CHUNKS * CHUNK
PAD = T_PAD - T_REAL
ROWS = BATCH * T_PAD
ROW_TILE = T_PAD // 4
N_TAB = 10
LANES = 128
VMEM_LIMIT = 56 * 1024 * 1024
NEG = -1e30

F32 = jnp.float32
BF16 = jnp.bfloat16


def _rms(x, g):
    return x * lax.rsqrt(jnp.mean(x * x, axis=-1, keepdims=True) + EPS) * g


def _resident(shape):
    return pl.BlockSpec(shape, lambda *_: (0,) * len(shape), pipeline_mode=pl.Buffered(1))


def _ffn_kernel(*refs, mix, final):
    refs = list(refs)
    h_ref = refs.pop(0)
    x = h_ref[...]
    if mix:
        yr_ref, yd_ref, wo_ref = refs.pop(0), refs.pop(0), refs.pop(0)
        x = (x + jnp.dot(yr_ref[...], wo_ref[:RET_WIDTH, :], preferred_element_type=F32)
             + jnp.dot(yd_ref[...], wo_ref[RET_WIDTH:, :], preferred_element_type=F32))
    g_ref, wg_ref, wu_ref, wd_ref = refs.pop(0), refs.pop(0), refs.pop(0), refs.pop(0)
    xn = _rms(x, g_ref[...]).astype(BF16)
    gate = jnp.dot(xn, wg_ref[...], preferred_element_type=F32)
    up = jnp.dot(xn, wu_ref[...], preferred_element_type=F32)
    act = (jax.nn.silu(gate) * up).astype(BF16)
    y = x + 0.5 * jnp.dot(act, wd_ref[...], preferred_element_type=F32)
    if final:
        fn_ref = refs.pop(0)
        y = _rms(y, fn_ref[...])
    o_ref = refs.pop(0)
    o_ref[...] = y


def _ffn(h, norm_g, wg, wu, wd, mix=None, final_g=None):
    row = lambda i: (i, 0)
    args = [h]
    specs = [pl.BlockSpec((ROW_TILE, D_MODEL), row)]
    if mix is not None:
        y_ret, y_diff, w_out = mix
        args += [y_ret, y_diff, w_out]
        specs += [pl.BlockSpec((ROW_TILE, RET_WIDTH), row),
                  pl.BlockSpec((ROW_TILE, DIFF_WIDTH), row),
                  _resident((D_MODEL, D_MODEL))]
    args += [norm_g, wg, wu, wd]
    specs += [_resident((1, D_MODEL)), _resident((D_MODEL, D_FF)),
              _resident((D_MODEL, D_FF)), _resident((D_FF, D_MODEL))]
    if final_g is not None:
        args.append(final_g)
        specs.append(_resident((1, D_MODEL)))
    return pl.pallas_call(
        functools.partial(_ffn_kernel, mix=mix is not None, final=final_g is not None),
        grid=(ROWS // ROW_TILE,),
        in_specs=specs,
        out_specs=pl.BlockSpec((ROW_TILE, D_MODEL), row),
        out_shape=jax.ShapeDtypeStruct((ROWS, D_MODEL), F32),
        compiler_params=pltpu.CompilerParams(
            dimension_semantics=("parallel",), vmem_limit_bytes=VMEM_LIMIT),
        name="ffn",
    )(*args)


def _inproj_kernel(h_ref, g_ref, w_ref, tab_ref, o_ref):
    xn = _rms(h_ref[...], g_ref[...]).astype(BF16)
    p = jnp.dot(xn, w_ref[...], preferred_element_type=F32)

    def tab(i):
        return tab_ref[:, i * LANES:(i + 1) * LANES]

    for base, t0 in ((0, 0), (RET_WIDTH, 2)):
        for hd in range(RET_HEADS):
            sl = slice(base + hd * LANES, base + (hd + 1) * LANES)
            xs = p[:, sl]
            o_ref[:, sl] = (xs * tab(t0) + pltpu.roll(xs, 64, 1) * tab(t0 + 1)).astype(BF16)
    o_ref[:, 2 * RET_WIDTH:4 * RET_WIDTH] = p[:, 2 * RET_WIDTH:4 * RET_WIDTH].astype(BF16)
    dq0 = 4 * RET_WIDTH
    for base, t0 in ((dq0, 4), (dq0 + DIFF_WIDTH, 7)):
        for hd in range(DIFF_HEADS):
            sl = slice(base + hd * LANES, base + (hd + 1) * LANES)
            xs = p[:, sl]
            o_ref[:, sl] = (xs * tab(t0) + pltpu.roll(xs, 8, 1) * tab(t0 + 1)
                            + pltpu.roll(xs, LANES - 8, 1) * tab(t0 + 2)).astype(BF16)
    o_ref[:, dq0 + 2 * DIFF_WIDTH:] = p[:, dq0 + 2 * DIFF_WIDTH:].astype(BF16)


def _inproj(h, norm_g, w_in, tab):
    return pl.pallas_call(
        _inproj_kernel,
        grid=(ROWS // ROW_TILE,),
        in_specs=[pl.BlockSpec((ROW_TILE, D_MODEL), lambda i: (i, 0)),
                  _resident((1, D_MODEL)),
                  _resident((D_MODEL, D_IN)),
                  pl.BlockSpec((ROW_TILE, N_TAB * LANES), lambda i: (i % (T_PAD // ROW_TILE), 0))],
        out_specs=pl.BlockSpec((ROW_TILE, D_IN), lambda i: (i, 0)),
        out_shape=jax.ShapeDtypeStruct((ROWS, D_IN), BF16),
        compiler_params=pltpu.CompilerParams(
            dimension_semantics=("parallel",), vmem_limit_bytes=VMEM_LIMIT),
        name="inproj",
    )(h, norm_g, w_in, tab)


def _retention_kernel(cd_ref, q_ref, k_ref, v_ref, g_ref, dintra_ref, qdec_ref, kdec_ref,
                      gnw_ref, o_ref, state_ref):
    @pl.when(pl.program_id(0) == 0)
    def _():
        state_ref[...] = jnp.zeros_like(state_ref)

    nt = (((1,), (1,)), ((), ()))
    tn = (((0,), (0,)), ((), ()))

    for b in range(BATCH):
        for hd in range(RET_HEADS):
            sl = slice(hd * LANES, (hd + 1) * LANES)
            q = q_ref[b, 0, :, sl]
            k = k_ref[b, 0, :, sl]
            v = v_ref[b, 0, :, sl]
            s = lax.dot_general(q, k, nt, preferred_element_type=F32) * dintra_ref[hd]
            qd = (q.astype(F32) * qdec_ref[hd]).astype(BF16)
            kd = (k.astype(F32) * kdec_ref[hd]).astype(BF16)
            r = state_ref[b, hd]
            o = (jnp.dot(s.astype(BF16), v, preferred_element_type=F32)
                 + jnp.dot(qd, r.astype(BF16), preferred_element_type=F32))
            state_ref[b, hd] = r * cd_ref[hd] + lax.dot_general(kd, v, tn, preferred_element_type=F32)
            mu = jnp.mean(o, axis=-1, keepdims=True)
            d = o - mu
            var = jnp.mean(d * d, axis=-1, keepdims=True)
            on = d * lax.rsqrt(var + EPS) * gnw_ref[:, sl]
            o_ref[b, 0, :, sl] = (jax.nn.silu(g_ref[b, 0, :, sl].astype(F32)) * on).astype(BF16)


def _retention(proj4, cd, dintra, qdec, kdec, gn_w):
    blk = lambda c: pl.BlockSpec((BATCH, 1, CHUNK, RET_WIDTH), lambda n, c=c: (0, n, 0, c))
    const3 = pl.BlockSpec((RET_HEADS, CHUNK, LANES), lambda n: (0, 0, 0))
    return pl.pallas_call(
        _retention_kernel,
        grid=(N_CHUNKS,),
        in_specs=[pl.BlockSpec(memory_space=pltpu.SMEM),
                  blk(0), blk(1), blk(2), blk(3), const3, const3, const3,
                  pl.BlockSpec((1, RET_WIDTH), lambda n: (0, 0))],
        out_specs=pl.BlockSpec((BATCH, 1, CHUNK, RET_WIDTH), lambda n: (0, n, 0, 0)),
        out_shape=jax.ShapeDtypeStruct((BATCH, N_CHUNKS, CHUNK, RET_WIDTH), BF16),
        scratch_shapes=[pltpu.VMEM((BATCH, RET_HEADS, RET_DK, LANES), F32)],
        compiler_params=pltpu.CompilerParams(
            dimension_semantics=("arbitrary",), vmem_limit_bytes=VMEM_LIMIT),
        name="retention",
    )(cd, proj4, proj4, proj4, proj4, dintra, qdec, kdec, gn_w)


_QUERY_BLOCKS = ((0, CHUNK),) + tuple((r, 2 * CHUNK) for r in range(CHUNK, T_PAD, 2 * CHUNK))


def _diffattn_kernel(q_ref, k_ref, v_ref, w_ref, lq1_ref, lk1_ref, lq2_ref, lk2_ref,
                     o_ref, *, lambda_init):
    lam = (jnp.exp(jnp.sum(lq1_ref[...] * lk1_ref[...], axis=-1, keepdims=True))
           - jnp.exp(jnp.sum(lq2_ref[...] * lk2_ref[...], axis=-1, keepdims=True))
           + lambda_init)
    nt = (((1,), (1,)), ((), ()))
    tn = (((0,), (0,)), ((), ()))

    for r0, nq in _QUERY_BLOCKS:
        nk = r0 + nq
        q = q_ref[0, r0:nk, :]
        lane = lax.broadcasted_iota(jnp.int32, (nq, LANES), 1)
        zero = jnp.zeros_like(q)
        qq = jnp.concatenate([jnp.where(lane < DIFF_D, q, zero),
                              jnp.where(lane >= DIFF_D, q, zero)], axis=0)
        s = lax.dot_general(k_ref[0, :nk, :], qq, nt, preferred_element_type=F32)
        krow = lax.broadcasted_iota(jnp.int32, (nq, 2 * nq), 0)
        qcol = lax.broadcasted_iota(jnp.int32, (nq, 2 * nq), 1) % nq
        diag = jnp.where(krow <= qcol, s[r0:, :], NEG)
        if r0 == 0:
            s = jnp.where(krow >= PAD, diag, NEG)
        else:
            prow = lax.broadcasted_iota(jnp.int32, (CHUNK, 2 * nq), 0)
            first = jnp.where(prow >= PAD, s[:CHUNK, :], NEG)
            middle = [s[CHUNK:r0, :]] if r0 > CHUNK else []
            s = jnp.concatenate([first] + middle + [diag], axis=0)
        p = jnp.exp2(s - jnp.max(s, axis=0, keepdims=True))
        p_pos, p_neg = p[:, :nq], p[:, nq:]
        a = (p_pos * (1.0 / jnp.sum(p_pos, axis=0, keepdims=True))
             - p_neg * (lam / jnp.sum(p_neg, axis=0, keepdims=True))).astype(BF16)
        o = lax.dot_general(v_ref[0, :nk, :], a, tn, preferred_element_type=F32).T
        o = o * lax.rsqrt(jnp.mean(o * o, axis=-1, keepdims=True) + EPS) * w_ref[...]
        o = o * (1.0 - lambda_init)
        if r0 == 0:
            orow = lax.broadcasted_iota(jnp.int32, (nq, LANES), 0)
            o = jnp.where(orow >= PAD, o, 0.0)
        o_ref[0, r0:nk, :] = o.astype(BF16)


def _diffattn(proj3, subln_w, lq1, lk1, lq2, lk2, lambda_init):
    first = 4 * RET_WIDTH // LANES
    blk = lambda c: pl.BlockSpec((1, T_PAD, LANES), lambda b, hd, c=c: (b, 0, first + c + hd))
    vec = lambda n: pl.BlockSpec((1, n), lambda b, hd: (0, 0))
    return pl.pallas_call(
        functools.partial(_diffattn_kernel, lambda_init=lambda_init),
        grid=(BATCH, DIFF_HEADS),
        in_specs=[blk(0), blk(DIFF_HEADS), blk(2 * DIFF_HEADS), vec(DIFF_DV), vec(DIFF_D),
                  vec(DIFF_D), vec(DIFF_D), vec(DIFF_D)],
        out_specs=pl.BlockSpec((1, T_PAD, LANES), lambda b, hd: (b, 0, hd)),
        out_shape=jax.ShapeDtypeStruct((BATCH, T_PAD, DIFF_WIDTH), BF16),
        compiler_params=pltpu.CompilerParams(
            dimension_semantics=("parallel", "parallel"), vmem_limit_bytes=VMEM_LIMIT),
        name="diffattn",
    )(proj3, proj3, proj3, subln_w, lq1, lk1, lq2, lk2)


def _rotary_tables():
    pos = jnp.arange(T_PAD, dtype=F32) - float(PAD)
    angle = RET_THETA ** (-jnp.linspace(0.0, 1.0, RET_DK // 2, dtype=F32))
    fr = pos[:, None] * angle[None, :]
    c, s = jnp.cos(fr), jnp.sin(fr)
    cos_r = jnp.concatenate([c, c], axis=-1)
    sin_r = jnp.concatenate([-s, s], axis=-1)
    ks = RET_DK ** -0.5
    inv = ROPE_THETA ** (-jnp.arange(0, ROPE_DIMS, 2, dtype=F32) / ROPE_DIMS)
    fq = pos[:, None] * inv[None, :]
    emb = jnp.concatenate([fq, fq], axis=-1)
    ce, se = jnp.cos(emb), jnp.sin(emb)
    half = ROPE_DIMS // 2
    rest = DIFF_D - ROPE_DIMS
    ones = jnp.ones((T_PAD, rest), F32)
    zeros = lambda n: jnp.zeros((T_PAD, n), F32)
    two = lambda a: jnp.concatenate([a, a], axis=-1)
    c_d = two(jnp.concatenate([ce, ones], axis=-1))
    s_lo = two(jnp.concatenate([zeros(half), se[:, half:], zeros(rest)], axis=-1))
    s_hi = two(jnp.concatenate([-se[:, :half], zeros(half + rest)], axis=-1))
    qs = (DIFF_D ** -0.5) * math.log2(math.e)
    return jnp.concatenate([cos_r, sin_r, cos_r * ks, sin_r * ks,
                            c_d * qs, s_lo * qs, s_hi * qs, c_d, s_lo, s_hi], axis=-1)


def _retention_consts():
    log_gamma = jnp.log(1.0 - 2.0 ** (-5.0 - jnp.arange(RET_HEADS, dtype=F32)))
    idx = jnp.arange(CHUNK, dtype=F32)
    rel = idx[:, None] - idx[None, :]
    dintra = jnp.where(rel >= 0, jnp.exp(log_gamma[:, None, None] * jnp.maximum(rel, 0.0)), 0.0)
    k_decay = jnp.exp(log_gamma[:, None] * (CHUNK - 1 - idx)[None, :])
    q_decay = jnp.exp(log_gamma[:, None] * (idx + 1.0)[None, :])
    cd = jnp.exp(log_gamma * CHUNK)
    bc = lambda a: jnp.broadcast_to(a[:, :, None], (RET_HEADS, CHUNK, LANES))
    return cd, dintra, bc(q_decay), bc(k_decay)


def _win_column_order():
    evens_then_odds = np.concatenate([np.arange(0, RET_DK, 2), np.arange(1, RET_DK, 2)])
    idx = np.arange(D_IN)
    for c in range(2 * RET_HEADS * RET_DK):
        idx[c] = (c // RET_DK) * RET_DK + evens_then_odds[c % RET_DK]
    return idx


def kernel(x, meta_tokens, ffn1_norm, ffn1_w_gate, ffn1_w_up, ffn1_w_down, mix_norm, w_in, ret_gn_w, diff_subln_w, diff_lambda_q1, diff_lambda_k1, diff_lambda_q2, diff_lambda_k2, w_out, ffn2_norm, ffn2_w_gate, ffn2_w_up, ffn2_w_down, final_norm):
    meta = jnp.broadcast_to(meta_tokens.astype(F32)[None], (BATCH, N_META, D_MODEL))
    h = jnp.concatenate([jnp.zeros((BATCH, PAD, D_MODEL), F32), meta, x.astype(F32)], axis=1)
    h = h.reshape(ROWS, D_MODEL)

    tab = _rotary_tables()
    cd, dintra, qdec, kdec = _retention_consts()
    col_order = _win_column_order()
    vec = lambda a: a.astype(F32).reshape(1, -1)

    for l in range(DEPTH):
        lambda_init = 0.8 - 0.6 * math.exp(-0.3 * l)
        h = _ffn(h, vec(ffn1_norm[l]), ffn1_w_gate[l].astype(BF16), ffn1_w_up[l].astype(BF16),
                 ffn1_w_down[l].astype(BF16))
        proj = _inproj(h, vec(mix_norm[l]), w_in[l][:, col_order].astype(BF16), tab)
        y_ret = _retention(proj.reshape(BATCH, N_CHUNKS, CHUNK, D_IN), cd, dintra, qdec, kdec,
                           vec(ret_gn_w[l]))
        y_diff = _diffattn(proj.reshape(BATCH, T_PAD, D_IN), vec(diff_subln_w[l]),
                           vec(diff_lambda_q1[l]), vec(diff_lambda_k1[l]),
                           vec(diff_lambda_q2[l]), vec(diff_lambda_k2[l]), lambda_init)
        h = _ffn(h, vec(ffn2_norm[l]), ffn2_w_gate[l].astype(BF16), ffn2_w_up[l].astype(BF16),
                 ffn2_w_down[l].astype(BF16),
                 mix=(y_ret.reshape(ROWS, RET_WIDTH), y_diff.reshape(ROWS, DIFF_WIDTH),
                      w_out[l].astype(BF16)),
                 final_g=vec(final_norm) if l == DEPTH - 1 else None)

    return h.reshape(BATCH, T_PAD, D_MODEL)[:, PAD + N_META:].astype(x.dtype)
```

```python
import functools
import math

import numpy as np
import jax
import jax.numpy as jnp
from jax import lax
from jax.experimental import pallas as pl
from jax.experimental.pallas import tpu as pltpu

D_MODEL = 1024
BATCH = 8
SEQ = 2048
DEPTH = 2
N_META = 16
CHUNK = 128
RET_HEADS = 4
RET_DK = 128
RET_WIDTH = 512
RET_THETA = 10000.0
DIFF_HEADS = 4
DIFF_D = 64
DIFF_DV = 128
DIFF_WIDTH = 512
ROPE_THETA = 500000.0
ROPE_DIMS = 16
D_FF = 2816
EPS = 1e-6
D_IN = 3584

T_REAL = SEQ + N_META
N_CHUNKS = -(-T_REAL // CHUNK)
T_PAD = N_CHUNKS * CHUNK
PAD = T_PAD - T_REAL
ROWS = BATCH * T_PAD
ROW_TILE = T_PAD // 4
N_TAB = 10
LANES = 128
VMEM_LIMIT = 56 * 1024 * 1024
NEG = -1e30

F32 = jnp.float32
BF16 = jnp.bfloat16


def _rms(x, g):
    return x * lax.rsqrt(jnp.mean(x * x, axis=-1, keepdims=True) + EPS) * g


def _resident(shape):
    return pl.BlockSpec(shape, lambda *_: (0,) * len(shape), pipeline_mode=pl.Buffered(1))


def _ffn_kernel(*refs, mix, final):
    refs = list(refs)
    h_ref = refs.pop(0)
    x = h_ref[...]
    if mix:
        yr_ref, yd_ref, wo_ref = refs.pop(0), refs.pop(0), refs.pop(0)
        x = (x + jnp.dot(yr_ref[...], wo_ref[:RET_WIDTH, :], preferred_element_type=F32)
             + jnp.dot(yd_ref[...], wo_ref[RET_WIDTH:, :], preferred_element_type=F32))
    g_ref, wg_ref, wu_ref, wd_ref = refs.pop(0), refs.pop(0), refs.pop(0), refs.pop(0)
    xn = _rms(x, g_ref[...]).astype(BF16)
    gate = jnp.dot(xn, wg_ref[...], preferred_element_type=F32)
    up = jnp.dot(xn, wu_ref[...], preferred_element_type=F32)
    act = (jax.nn.silu(gate) * up).astype(BF16)
    y = x + 0.5 * jnp.dot(act, wd_ref[...], preferred_element_type=F32)
    if final:
        fn_ref = refs.pop(0)
        y = _rms(y, fn_ref[...])
    o_ref = refs.pop(0)
    o_ref[...] = y


def _ffn(h, norm_g, wg, wu, wd, mix=None, final_g=None):
    row = lambda i: (i, 0)
    args = [h]
    specs = [pl.BlockSpec((ROW_TILE, D_MODEL), row)]
    if mix is not None:
        y_ret, y_diff, w_out = mix
        args += [y_ret, y_diff, w_out]
        specs += [pl.BlockSpec((ROW_TILE, RET_WIDTH), row),
                  pl.BlockSpec((ROW_TILE, DIFF_WIDTH), row),
                  _resident((D_MODEL, D_MODEL))]
    args += [norm_g, wg, wu, wd]
    specs += [_resident((1, D_MODEL)), _resident((D_MODEL, D_FF)),
              _resident((D_MODEL, D_FF)), _resident((D_FF, D_MODEL))]
    if final_g is not None:
        args.append(final_g)
        specs.append(_resident((1, D_MODEL)))
    return pl.pallas_call(
        functools.partial(_ffn_kernel, mix=mix is not None, final=final_g is not None),
        grid=(ROWS // ROW_TILE,),
        in_specs=specs,
        out_specs=pl.BlockSpec((ROW_TILE, D_MODEL), row),
        out_shape=jax.ShapeDtypeStruct((ROWS, D_MODEL), F32),
        compiler_params=pltpu.CompilerParams(
            dimension_semantics=("parallel",), vmem_limit_bytes=VMEM_LIMIT),
        name="ffn",
    )(*args)


def _inproj_kernel(h_ref, g_ref, w_ref, tab_ref, o_ref):
    xn = _rms(h_ref[...], g_ref[...]).astype(BF16)
    p = jnp.dot(xn, w_ref[...], preferred_element_type=F32)

    def tab(i):
        return tab_ref[:, i * LANES:(i + 1) * LANES]

    for base, t0 in ((0, 0), (RET_WIDTH, 2)):
        for hd in range(RET_HEADS):
            sl = slice(base + hd * LANES, base + (hd + 1) * LANES)
            xs = p[:, sl]
            o_ref[:, sl] = (xs * tab(t0) + pltpu.roll(xs, 64, 1) * tab(t0 + 1)).astype(BF16)
    o_ref[:, 2 * RET_WIDTH:4 * RET_WIDTH] = p[:, 2 * RET_WIDTH:4 * RET_WIDTH].astype(BF16)
    dq0 = 4 * RET_WIDTH
    for base, t0 in ((dq0, 4), (dq0 + DIFF_WIDTH, 7)):
        for hd in range(DIFF_HEADS):
            sl = slice(base + hd * LANES, base + (hd + 1) * LANES)
            xs = p[:, sl]
            o_ref[:, sl] = (xs * tab(t0) + pltpu.roll(xs, 8, 1) * tab(t0 + 1)
                            + pltpu.roll(xs, LANES - 8, 1) * tab(t0 + 2)).astype(BF16)
    o_ref[:, dq0 + 2 * DIFF_WIDTH:] = p[:, dq0 + 2 * DIFF_WIDTH:].astype(BF16)


def _inproj(h, norm_g, w_in, tab):
    return pl.pallas_call(
        _inproj_kernel,
        grid=(ROWS // ROW_TILE,),
        in_specs=[pl.BlockSpec((ROW_TILE, D_MODEL), lambda i: (i, 0)),
                  _resident((1, D_MODEL)),
                  _resident((D_MODEL, D_IN)),
                  pl.BlockSpec((ROW_TILE, N_TAB * LANES), lambda i: (i % (T_PAD // ROW_TILE), 0))],
        out_specs=pl.BlockSpec((ROW_TILE, D_IN), lambda i: (i, 0)),
        out_shape=jax.ShapeDtypeStruct((ROWS, D_IN), BF16),
        compiler_params=pltpu.CompilerParams(
            dimension_semantics=("parallel",), vmem_limit_bytes=VMEM_LIMIT),
        name="inproj",
    )(h, norm_g, w_in, tab)


def _retention_kernel(cd_ref, q_ref, k_ref, v_ref, g_ref, dintra_ref, qdec_ref, kdec_ref,
                      gnw_ref, o_ref, state_ref):
    @pl.when(pl.program_id(0) == 0)
    def _():
        state_ref[...] = jnp.zeros_like(state_ref)

    nt = (((1,), (1,)), ((), ()))
    tn = (((0,), (0,)), ((), ()))

    for b in range(BATCH):
        for hd in range(RET_HEADS):
            sl = slice(hd * LANES, (hd + 1) * LANES)
            q = q_ref[b, 0, :, sl]
            k = k_ref[b, 0, :, sl]
            v = v_ref[b, 0, :, sl]
            s = lax.dot_general(q, k, nt, preferred_element_type=F32) * dintra_ref[hd]
            qd = (q.astype(F32) * qdec_ref[hd]).astype(BF16)
            kd = (k.astype(F32) * kdec_ref[hd]).astype(BF16)
            r = state_ref[b, hd]
            o = (jnp.dot(s.astype(BF16), v, preferred_element_type=F32)
                 + jnp.dot(qd, r.astype(BF16), preferred_element_type=F32))
            state_ref[b, hd] = r * cd_ref[hd] + lax.dot_general(kd, v, tn, preferred_element_type=F32)
            mu = jnp.mean(o, axis=-1, keepdims=True)
            d = o - mu
            var = jnp.mean(d * d, axis=-1, keepdims=True)
            on = d * lax.rsqrt(var + EPS) * gnw_ref[:, sl]
            o_ref[b, 0, :, sl] = (jax.nn.silu(g_ref[b, 0, :, sl].astype(F32)) * on).astype(BF16)


def _retention(proj4, cd, dintra, qdec, kdec, gn_w):
    blk = lambda c: pl.BlockSpec((BATCH, 1, CHUNK, RET_WIDTH), lambda n, c=c: (0, n, 0, c))
    const3 = pl.BlockSpec((RET_HEADS, CHUNK, LANES), lambda n: (0, 0, 0))
    return pl.pallas_call(
        _retention_kernel,
        grid=(N_CHUNKS,),
        in_specs=[pl.BlockSpec(memory_space=pltpu.SMEM),
                  blk(0), blk(1), blk(2), blk(3), const3, const3, const3,
                  pl.BlockSpec((1, RET_WIDTH), lambda n: (0, 0))],
        out_specs=pl.BlockSpec((BATCH, 1, CHUNK, RET_WIDTH), lambda n: (0, n, 0, 0)),
        out_shape=jax.ShapeDtypeStruct((BATCH, N_CHUNKS, CHUNK, RET_WIDTH), BF16),
        scratch_shapes=[pltpu.VMEM((BATCH, RET_HEADS, RET_DK, LANES), F32)],
        compiler_params=pltpu.CompilerParams(
            dimension_semantics=("arbitrary",), vmem_limit_bytes=VMEM_LIMIT),
        name="retention",
    )(cd, proj4, proj4, proj4, proj4, dintra, qdec, kdec, gn_w)


_QUERY_BLOCKS = ((0, CHUNK),) + tuple((r, 2 * CHUNK) for r in range(CHUNK, T_PAD, 2 * CHUNK))


def _key_tiles(nk):
    return ((0, CHUNK),) + tuple((r, 2 * CHUNK) for r in range(CHUNK, nk, 2 * CHUNK))


def _diffattn_kernel(q_ref, k_ref, v_ref, w_ref, lq1_ref, lk1_ref, lq2_ref, lk2_ref,
                     o_ref, s_ref, p_ref, *, lambda_init):
    lam = (jnp.exp(jnp.sum(lq1_ref[...] * lk1_ref[...], axis=-1, keepdims=True))
           - jnp.exp(jnp.sum(lq2_ref[...] * lk2_ref[...], axis=-1, keepdims=True))
           + lambda_init)
    nt = (((1,), (1,)), ((), ()))
    tn = (((0,), (0,)), ((), ()))

    def fold(state, key, value, op):
        state[key] = value if key not in state else op(state[key], value)

    def score_tiles(blk, state):
        r0, nq = _QUERY_BLOCKS[blk]
        nk, slot = r0 + nq, blk % 2
        q = q_ref[0, r0:nk, :]
        lane = lax.broadcasted_iota(jnp.int32, (nq, LANES), 1)
        zero = jnp.zeros_like(q)
        qq = jnp.concatenate([jnp.where(lane < DIFF_D, q, zero),
                              jnp.where(lane >= DIFF_D, q, zero)], axis=0)

        def tile(t0, tk):
            s = lax.dot_general(k_ref[0, t0:t0 + tk, :], qq, nt, preferred_element_type=F32)
            krow = lax.broadcasted_iota(jnp.int32, (tk, 2 * nq), 0)
            if t0 + tk == nk:
                qcol = lax.broadcasted_iota(jnp.int32, (tk, 2 * nq), 1) % nq
                s = jnp.where(krow <= qcol, s, NEG)
            if t0 == 0:
                s = jnp.where(krow >= PAD, s, NEG)
            s_ref[slot, t0:t0 + tk, :2 * nq] = s
            fold(state, "max", jnp.max(s, axis=0, keepdims=True), jnp.maximum)

        return [functools.partial(tile, t0, tk) for t0, tk in _key_tiles(nk)]

    def prob_tiles(blk, smax, state):
        r0, nq = _QUERY_BLOCKS[blk]
        nk, slot = r0 + nq, blk % 2

        def tile(t0, tk):
            p = jnp.exp2(s_ref[slot, t0:t0 + tk, :2 * nq] - smax)
            fold(state, "pos", jnp.sum(p[:, :nq], axis=0, keepdims=True), jnp.add)
            fold(state, "neg", jnp.sum(p[:, nq:], axis=0, keepdims=True), jnp.add)
            p_ref[slot, t0:t0 + tk, :2 * nq] = p.astype(BF16)

        return [functools.partial(tile, t0, tk) for t0, tk in _key_tiles(nk)]

    def finish(blk, sums):
        r0, nq = _QUERY_BLOCKS[blk]
        nk, slot = r0 + nq, blk % 2
        ot = lax.dot_general(v_ref[0, :nk, :], p_ref[slot, :nk, :2 * nq], tn,
                             preferred_element_type=F32)
        o = (ot[:, :nq] * (1.0 / sums["pos"]) - ot[:, nq:] * (lam / sums["neg"])).T
        o = o * lax.rsqrt(jnp.mean(o * o, axis=-1, keepdims=True) + EPS) * w_ref[...]
        o = o * (1.0 - lambda_init)
        if r0 == 0:
            orow = lax.broadcasted_iota(jnp.int32, (nq, LANES), 0)
            o = jnp.where(orow >= PAD, o, 0.0)
        o_ref[0, r0:nk, :] = o.astype(BF16)

    score_state = {}
    for run in score_tiles(0, score_state):
        run()
    for blk in range(len(_QUERY_BLOCKS)):
        smax, sums, score_state = score_state["max"], {}, {}
        ahead = score_tiles(blk + 1, score_state) if blk + 1 < len(_QUERY_BLOCKS) else []
        current = prob_tiles(blk, smax, sums)
        for n in range(max(len(ahead), len(current))):
            for runs in (current, ahead):
                if n < len(runs):
                    runs[n]()
        finish(blk, sums)


def _diffattn(proj3, subln_w, lq1, lk1, lq2, lk2, lambda_init):
    first = 4 * RET_WIDTH // LANES
    blk = lambda c: pl.BlockSpec((1, T_PAD, LANES), lambda b, hd, c=c: (b, 0, first + c + hd))
    vec = lambda n: pl.BlockSpec((1, n), lambda b, hd: (0, 0))
    return pl.pallas_call(
        functools.partial(_diffattn_kernel, lambda_init=lambda_init),
        grid=(BATCH, DIFF_HEADS),
        in_specs=[blk(0), blk(DIFF_HEADS), blk(2 * DIFF_HEADS), vec(DIFF_DV), vec(DIFF_D),
                  vec(DIFF_D), vec(DIFF_D), vec(DIFF_D)],
        out_specs=pl.BlockSpec((1, T_PAD, LANES), lambda b, hd: (b, 0, hd)),
        out_shape=jax.ShapeDtypeStruct((BATCH, T_PAD, DIFF_WIDTH), BF16),
        scratch_shapes=[pltpu.VMEM((2, T_PAD, 4 * CHUNK), F32),
                        pltpu.VMEM((2, T_PAD, 4 * CHUNK), BF16)],
        compiler_params=pltpu.CompilerParams(
            dimension_semantics=("parallel", "parallel"), vmem_limit_bytes=VMEM_LIMIT),
        name="diffattn",
    )(proj3, proj3, proj3, subln_w, lq1, lk1, lq2, lk2)


def _rotary_tables():
    pos = jnp.arange(T_PAD, dtype=F32) - float(PAD)
    angle = RET_THETA ** (-jnp.linspace(0.0, 1.0, RET_DK // 2, dtype=F32))
    fr = pos[:, None] * angle[None, :]
    c, s = jnp.cos(fr), jnp.sin(fr)
    cos_r = jnp.concatenate([c, c], axis=-1)
    sin_r = jnp.concatenate([-s, s], axis=-1)
    ks = RET_DK ** -0.5
    inv = ROPE_THETA ** (-jnp.arange(0, ROPE_DIMS, 2, dtype=F32) / ROPE_DIMS)
    fq = pos[:, None] * inv[None, :]
    emb = jnp.concatenate([fq, fq], axis=-1)
    ce, se = jnp.cos(emb), jnp.sin(emb)
    half = ROPE_DIMS // 2
    rest = DIFF_D - ROPE_DIMS
    ones = jnp.ones((T_PAD, rest), F32)
    zeros = lambda n: jnp.zeros((T_PAD, n), F32)
    two = lambda a: jnp.concatenate([a, a], axis=-1)
    c_d = two(jnp.concatenate([ce, ones], axis=-1))
    s_lo = two(jnp.concatenate([zeros(half), se[:, half:], zeros(rest)], axis=-1))
    s_hi = two(jnp.concatenate([-se[:, :half], zeros(half + rest)], axis=-1))
    qs = (DIFF_D ** -0.5) * math.log2(math.e)
    return jnp.concatenate([cos_r, sin_r, cos_r * ks, sin_r * ks,
                            c_d * qs, s_lo * qs, s_hi * qs, c_d, s_lo, s_hi], axis=-1)


def _retention_consts():
    log_gamma = jnp.log(1.0 - 2.0 ** (-5.0 - jnp.arange(RET_HEADS, dtype=F32)))
    idx = jnp.arange(CHUNK, dtype=F32)
    rel = idx[:, None] - idx[None, :]
    dintra = jnp.where(rel >= 0, jnp.exp(log_gamma[:, None, None] * jnp.maximum(rel, 0.0)), 0.0)
    k_decay = jnp.exp(log_gamma[:, None] * (CHUNK - 1 - idx)[None, :])
    q_decay = jnp.exp(log_gamma[:, None] * (idx + 1.0)[None, :])
    cd = jnp.exp(log_gamma * CHUNK)
    bc = lambda a: jnp.broadcast_to(a[:, :, None], (RET_HEADS, CHUNK, LANES))
    return cd, dintra, bc(q_decay), bc(k_decay)


def _win_column_order():
    evens_then_odds = np.concatenate([np.arange(0, RET_DK, 2), np.arange(1, RET_DK, 2)])
    idx = np.arange(D_IN)
    for c in range(2 * RET_HEADS * RET_DK):
        idx[c] = (c // RET_DK) * RET_DK + evens_then_odds[c % RET_DK]
    return idx


def kernel(x, meta_tokens, ffn1_norm, ffn1_w_gate, ffn1_w_up, ffn1_w_down, mix_norm, w_in, ret_gn_w, diff_subln_w, diff_lambda_q1, diff_lambda_k1, diff_lambda_q2, diff_lambda_k2, w_out, ffn2_norm, ffn2_w_gate, ffn2_w_up, ffn2_w_down, final_norm):
    meta = jnp.broadcast_to(meta_tokens.astype(F32)[None], (BATCH, N_META, D_MODEL))
    h = jnp.concatenate([jnp.zeros((BATCH, PAD, D_MODEL), F32), meta, x.astype(F32)], axis=1)
    h = h.reshape(ROWS, D_MODEL)

    tab = _rotary_tables()
    cd, dintra, qdec, kdec = _retention_consts()
    col_order = _win_column_order()
    vec = lambda a: a.astype(F32).reshape(1, -1)

    for l in range(DEPTH):
        lambda_init = 0.8 - 0.6 * math.exp(-0.3 * l)
        h = _ffn(h, vec(ffn1_norm[l]), ffn1_w_gate[l].astype(BF16), ffn1_w_up[l].astype(BF16),
                 ffn1_w_down[l].astype(BF16))
        proj = _inproj(h, vec(mix_norm[l]), w_in[l][:, col_order].astype(BF16), tab)
        y_ret = _retention(proj.reshape(BATCH, N_CHUNKS, CHUNK, D_IN), cd, dintra, qdec, kdec,
                           vec(ret_gn_w[l]))
        y_diff = _diffattn(proj.reshape(BATCH, T_PAD, D_IN), vec(diff_subln_w[l]),
                           vec(diff_lambda_q1[l]), vec(diff_lambda_k1[l]),
                           vec(diff_lambda_q2[l]), vec(diff_lambda_k2[l]), lambda_init)
        h = _ffn(h, vec(ffn2_norm[l]), ffn2_w_gate[l].astype(BF16), ffn2_w_up[l].astype(BF16),
                 ffn2_w_down[l].astype(BF16),
                 mix=(y_ret.reshape(ROWS, RET_WIDTH), y_diff.reshape(ROWS, DIFF_WIDTH),
                      w_out[l].astype(BF16)),
                 final_g=vec(final_norm) if l == DEPTH - 1 else None)

    return h.reshape(BATCH, T_PAD, D_MODEL)[:, PAD + N_META:].astype(x.dtype)
```

```python
import functools
import math

import numpy as np
import jax
import jax.numpy as jnp
from jax import lax
from jax.experimental import pallas as pl
from jax.experimental.pallas import tpu as pltpu

D_MODEL = 1024
BATCH = 8
SEQ = 2048
DEPTH = 2
N_META = 16
CHUNK = 128
RET_HEADS = 4
RET_DK = 128
RET_WIDTH = 512
RET_THETA = 10000.0
DIFF_HEADS = 4
DIFF_D = 64
DIFF_DV = 128
DIFF_WIDTH = 512
ROPE_THETA = 500000.0
ROPE_DIMS = 16
D_FF = 2816
EPS = 1e-6
D_IN = 3584

T_REAL = SEQ + N_META
N_CHUNKS = -(-T_REAL // CHUNK)
T_PAD = N_CHUNKS * CHUNK
PAD = T_PAD - T_REAL
ROWS = BATCH * T_PAD
ROW_TILE = T_PAD // 4
N_ROW_TILES = ROWS // ROW_TILE
N_WCHUNK = 8
N_TAB = 12
LANES = 128
VMEM_LIMIT = 56 * 1024 * 1024
NEG = -1e30

F32 = jnp.float32
BF16 = jnp.bfloat16


def _rms(x, g):
    return x * lax.rsqrt(jnp.mean(x * x, axis=-1, keepdims=True) + EPS) * g


def _whole(shape):
    return pl.BlockSpec(shape, lambda *_: (0,) * len(shape))


def _row_tile(width):
    return pl.BlockSpec((ROW_TILE, width), lambda i: (jnp.maximum(i - N_WCHUNK, 0), 0))


def _weight_chunk(layer, rows, cols):
    return pl.BlockSpec((None, rows // N_WCHUNK, cols),
                        lambda i: (layer, jnp.minimum(i, N_WCHUNK - 1), 0))


def _stage_weight(i, src_ref, dst_ref):
    rows = src_ref.shape[0]
    dst_ref[pl.ds(pl.multiple_of(i * rows, 16), rows), :] = src_ref[...].astype(BF16)


def _ffn_kernel(*refs, layer, mix, final):
    refs = list(refs)
    h_ref = refs.pop(0)
    if mix:
        yr_ref, yd_ref, wo_ref = refs.pop(0), refs.pop(0), refs.pop(0)
    g_ref, wg_ref, wu_ref, wd_ref = refs.pop(0), refs.pop(0), refs.pop(0), refs.pop(0)
    if final:
        fn_ref = refs.pop(0)
    o_ref = refs.pop(0)
    wg_s, wu_s, wd_s = refs.pop(0), refs.pop(0), refs.pop(0)
    if mix:
        wo_s = refs.pop(0)
    i = pl.program_id(0)

    @pl.when(i < N_WCHUNK)
    def _():
        _stage_weight(i, wg_ref, wg_s)
        _stage_weight(i, wu_ref, wu_s)
        _stage_weight(i, wd_ref, wd_s)
        if mix:
            _stage_weight(i, wo_ref, wo_s)

    @pl.when(i >= N_WCHUNK)
    def _():
        x = h_ref[...]
        if mix:
            x = (x + jnp.dot(yr_ref[...], wo_s[:RET_WIDTH, :], preferred_element_type=F32)
                 + jnp.dot(yd_ref[...], wo_s[RET_WIDTH:, :], preferred_element_type=F32))
        xn = _rms(x, g_ref[layer:layer + 1, :]).astype(BF16)
        gate = jnp.dot(xn, wg_s[...], preferred_element_type=F32)
        up = jnp.dot(xn, wu_s[...], preferred_element_type=F32)
        act = (jax.nn.silu(gate) * up).astype(BF16)
        y = x + 0.5 * jnp.dot(act, wd_s[...], preferred_element_type=F32)
        if final:
            y = _rms(y, fn_ref[...])
        o_ref[...] = y


def _ffn(layer, h, norm_g, wg, wu, wd, mix=None, final_g=None):
    args = [h]
    specs = [_row_tile(D_MODEL)]
    scratch = [pltpu.VMEM((D_MODEL, D_FF), BF16), pltpu.VMEM((D_MODEL, D_FF), BF16),
               pltpu.VMEM((D_FF, D_MODEL), BF16)]
    if mix is not None:
        y_ret, y_diff, w_out = mix
        args += [y_ret, y_diff, w_out]
        specs += [_row_tile(RET_WIDTH), _row_tile(DIFF_WIDTH),
                  _weight_chunk(layer, D_MODEL, D_MODEL)]
        scratch.append(pltpu.VMEM((D_MODEL, D_MODEL), BF16))
    args += [norm_g, wg, wu, wd]
    specs += [_whole((DEPTH, D_MODEL)), _weight_chunk(layer, D_MODEL, D_FF),
              _weight_chunk(layer, D_MODEL, D_FF), _weight_chunk(layer, D_FF, D_MODEL)]
    if final_g is not None:
        args.append(final_g)
        specs.append(_whole((1, D_MODEL)))
    return pl.pallas_call(
        functools.partial(_ffn_kernel, layer=layer, mix=mix is not None,
                          final=final_g is not None),
        grid=(N_WCHUNK + N_ROW_TILES,),
        in_specs=specs,
        out_specs=_row_tile(D_MODEL),
        out_shape=jax.ShapeDtypeStruct((ROWS, D_MODEL), F32),
        scratch_shapes=scratch,
        compiler_params=pltpu.CompilerParams(
            dimension_semantics=("arbitrary",), vmem_limit_bytes=VMEM_LIMIT),
        name="ffn",
    )(*args)


_ROTATIONS = (
    (0, 0, LANES - 1, 1),
    (RET_WIDTH, 3, LANES - 1, 1),
    (4 * RET_WIDTH, 6, 8, LANES - 8),
    (4 * RET_WIDTH + DIFF_WIDTH, 9, 8, LANES - 8),
)


def _inproj_kernel(h_ref, g_ref, w_ref, tab_ref, o_ref, w_s, *, layer):
    i = pl.program_id(0)

    @pl.when(i < N_WCHUNK)
    def _():
        _stage_weight(i, w_ref, w_s)

    @pl.when(i >= N_WCHUNK)
    def _():
        xn = _rms(h_ref[...], g_ref[layer:layer + 1, :]).astype(BF16)
        p = jnp.dot(xn, w_s[...], preferred_element_type=F32)

        def tab(t):
            return tab_ref[:, t * LANES:(t + 1) * LANES]

        for base, t0, roll_a, roll_b in _ROTATIONS:
            for hd in range(RET_HEADS):
                sl = slice(base + hd * LANES, base + (hd + 1) * LANES)
                xs = p[:, sl]
                o_ref[:, sl] = (xs * tab(t0) + pltpu.roll(xs, roll_a, 1) * tab(t0 + 1)
                                + pltpu.roll(xs, roll_b, 1) * tab(t0 + 2)).astype(BF16)
        for lo, hi in ((2 * RET_WIDTH, 4 * RET_WIDTH), (D_IN - DIFF_WIDTH, D_IN)):
            o_ref[:, lo:hi] = p[:, lo:hi].astype(BF16)


def _inproj(layer, h, norm_g, w_in, tab):
    tiles_per_batch = T_PAD // ROW_TILE
    return pl.pallas_call(
        functools.partial(_inproj_kernel, layer=layer),
        grid=(N_WCHUNK + N_ROW_TILES,),
        in_specs=[_row_tile(D_MODEL), _whole((DEPTH, D_MODEL)),
                  _weight_chunk(layer, D_MODEL, D_IN),
                  pl.BlockSpec((ROW_TILE, N_TAB * LANES),
                               lambda i: (jnp.maximum(i - N_WCHUNK, 0) % tiles_per_batch, 0))],
        out_specs=_row_tile(D_IN),
        out_shape=jax.ShapeDtypeStruct((ROWS, D_IN), BF16),
        scratch_shapes=[pltpu.VMEM((D_MODEL, D_IN), BF16)],
        compiler_params=pltpu.CompilerParams(
            dimension_semantics=("arbitrary",), vmem_limit_bytes=VMEM_LIMIT),
        name="inproj",
    )(h, norm_g, w_in, tab)


def _retention_kernel(cd_ref, q_ref, k_ref, v_ref, g_ref, dintra_ref, qdec_ref, kdec_ref,
                      gnw_ref, o_ref, state_ref, *, layer):
    @pl.when(pl.program_id(0) == 0)
    def _():
        state_ref[...] = jnp.zeros_like(state_ref)

    nt = (((1,), (1,)), ((), ()))
    tn = (((0,), (0,)), ((), ()))

    for b in range(BATCH):
        for hd in range(RET_HEADS):
            sl = slice(hd * LANES, (hd + 1) * LANES)
            q = q_ref[b, 0, :, sl]
            k = k_ref[b, 0, :, sl]
            v = v_ref[b, 0, :, sl]
            s = lax.dot_general(q, k, nt, preferred_element_type=F32) * dintra_ref[hd]
            qd = (q.astype(F32) * qdec_ref[hd]).astype(BF16)
            kd = (k.astype(F32) * kdec_ref[hd]).astype(BF16)
            r = state_ref[b, hd]
            o = (jnp.dot(s.astype(BF16), v, preferred_element_type=F32)
                 + jnp.dot(qd, r.astype(BF16), preferred_element_type=F32))
            state_ref[b, hd] = r * cd_ref[hd] + lax.dot_general(kd, v, tn, preferred_element_type=F32)
            mu = jnp.mean(o, axis=-1, keepdims=True)
            d = o - mu
            var = jnp.mean(d * d, axis=-1, keepdims=True)
            on = d * lax.rsqrt(var + EPS) * gnw_ref[layer:layer + 1, sl]
            o_ref[b, 0, :, sl] = (jax.nn.silu(g_ref[b, 0, :, sl].astype(F32)) * on).astype(BF16)


def _retention(layer, proj4, cd, dintra, qdec, kdec, gn_w):
    blk = lambda c: pl.BlockSpec((BATCH, 1, CHUNK, RET_WIDTH), lambda n, c=c: (0, n, 0, c))
    const3 = _whole((RET_HEADS, CHUNK, LANES))
    return pl.pallas_call(
        functools.partial(_retention_kernel, layer=layer),
        grid=(N_CHUNKS,),
        in_specs=[pl.BlockSpec(memory_space=pltpu.SMEM),
                  blk(0), blk(1), blk(2), blk(3), const3, const3, const3,
                  _whole((DEPTH, RET_WIDTH))],
        out_specs=pl.BlockSpec((BATCH, 1, CHUNK, RET_WIDTH), lambda n: (0, n, 0, 0)),
        out_shape=jax.ShapeDtypeStruct((BATCH, N_CHUNKS, CHUNK, RET_WIDTH), BF16),
        scratch_shapes=[pltpu.VMEM((BATCH, RET_HEADS, RET_DK, LANES), F32)],
        compiler_params=pltpu.CompilerParams(
            dimension_semantics=("arbitrary",), vmem_limit_bytes=VMEM_LIMIT),
        name="retention",
    )(cd, proj4, proj4, proj4, proj4, dintra, qdec, kdec, gn_w)


_QUERY_BLOCKS = ((0, CHUNK),) + tuple((r, 2 * CHUNK) for r in range(CHUNK, T_PAD, 2 * CHUNK))


def _key_tiles(nk):
    return ((0, CHUNK),) + tuple((r, 2 * CHUNK) for r in range(CHUNK, nk, 2 * CHUNK))


def _diffattn_kernel(q_ref, k_ref, v_ref, w_ref, lq1_ref, lk1_ref, lq2_ref, lk2_ref,
                     o_ref, s_ref, p_ref, *, layer, lambda_init):
    row = slice(layer, layer + 1)
    lam = (jnp.exp(jnp.sum(lq1_ref[row, :] * lk1_ref[row, :], axis=-1, keepdims=True))
           - jnp.exp(jnp.sum(lq2_ref[row, :] * lk2_ref[row, :], axis=-1, keepdims=True))
           + lambda_init)
    nt = (((1,), (1,)), ((), ()))
    tn = (((0,), (0,)), ((), ()))

    def fold(state, key, value, op):
        state[key] = value if key not in state else op(state[key], value)

    def score_tiles(blk, state):
        r0, nq = _QUERY_BLOCKS[blk]
        nk, slot = r0 + nq, blk % 2
        q = q_ref[0, r0:nk, :]
        lane = lax.broadcasted_iota(jnp.int32, (nq, LANES), 1)
        zero = jnp.zeros_like(q)
        qq = jnp.concatenate([jnp.where(lane < DIFF_D, q, zero),
                              jnp.where(lane >= DIFF_D, q, zero)], axis=0)

        def tile(t0, tk):
            s = lax.dot_general(k_ref[0, t0:t0 + tk, :], qq, nt, preferred_element_type=F32)
            krow = lax.broadcasted_iota(jnp.int32, (tk, 2 * nq), 0)
            if t0 + tk == nk:
                qcol = lax.broadcasted_iota(jnp.int32, (tk, 2 * nq), 1) % nq
                s = jnp.where(krow <= qcol, s, NEG)
            if t0 == 0:
                s = jnp.where(krow >= PAD, s, NEG)
            s_ref[slot, t0:t0 + tk, :2 * nq] = s
            fold(state, "max", jnp.max(s, axis=0, keepdims=True), jnp.maximum)

        return [functools.partial(tile, t0, tk) for t0, tk in _key_tiles(nk)]

    def prob_tiles(blk, smax, state):
        r0, nq = _QUERY_BLOCKS[blk]
        nk, slot = r0 + nq, blk % 2

        def tile(t0, tk):
            p = jnp.exp2(s_ref[slot, t0:t0 + tk, :2 * nq] - smax)
            fold(state, "pos", jnp.sum(p[:, :nq], axis=0, keepdims=True), jnp.add)
            fold(state, "neg", jnp.sum(p[:, nq:], axis=0, keepdims=True), jnp.add)
            p_ref[slot, t0:t0 + tk, :2 * nq] = p.astype(BF16)

        return [functools.partial(tile, t0, tk) for t0, tk in _key_tiles(nk)]

    def finish(blk, sums):
        r0, nq = _QUERY_BLOCKS[blk]
        nk, slot = r0 + nq, blk % 2
        ot = lax.dot_general(v_ref[0, :nk, :], p_ref[slot, :nk, :2 * nq], tn,
                             preferred_element_type=F32)
        o = (ot[:, :nq] * (1.0 / sums["pos"]) - ot[:, nq:] * (lam / sums["neg"])).T
        o = o * lax.rsqrt(jnp.mean(o * o, axis=-1, keepdims=True) + EPS) * w_ref[row, :]
        o = o * (1.0 - lambda_init)
        if r0 == 0:
            orow = lax.broadcasted_iota(jnp.int32, (nq, LANES), 0)
            o = jnp.where(orow >= PAD, o, 0.0)
        o_ref[0, r0:nk, :] = o.astype(BF16)

    score_state = {}
    for run in score_tiles(0, score_state):
        run()
    for blk in range(len(_QUERY_BLOCKS)):
        smax, sums, score_state = score_state["max"], {}, {}
        ahead = score_tiles(blk + 1, score_state) if blk + 1 < len(_QUERY_BLOCKS) else []
        current = prob_tiles(blk, smax, sums)
        for n in range(max(len(ahead), len(current))):
            for runs in (current, ahead):
                if n < len(runs):
                    runs[n]()
        finish(blk, sums)


def _diffattn(layer, proj3, subln_w, lq1, lk1, lq2, lk2):
    lambda_init = 0.8 - 0.6 * math.exp(-0.3 * layer)
    first = 4 * RET_WIDTH // LANES
    blk = lambda c: pl.BlockSpec((1, T_PAD, LANES), lambda b, hd, c=c: (b, 0, first + c + hd))
    return pl.pallas_call(
        functools.partial(_diffattn_kernel, layer=layer, lambda_init=lambda_init),
        grid=(BATCH, DIFF_HEADS),
        in_specs=[blk(0), blk(DIFF_HEADS), blk(2 * DIFF_HEADS), _whole((DEPTH, DIFF_DV)),
                  _whole((DEPTH, DIFF_D)), _whole((DEPTH, DIFF_D)), _whole((DEPTH, DIFF_D)),
                  _whole((DEPTH, DIFF_D))],
        out_specs=pl.BlockSpec((1, T_PAD, LANES), lambda b, hd: (b, 0, hd)),
        out_shape=jax.ShapeDtypeStruct((BATCH, T_PAD, DIFF_WIDTH), BF16),
        scratch_shapes=[pltpu.VMEM((2, T_PAD, 4 * CHUNK), F32),
                        pltpu.VMEM((2, T_PAD, 4 * CHUNK), BF16)],
        compiler_params=pltpu.CompilerParams(
            dimension_semantics=("parallel", "parallel"), vmem_limit_bytes=VMEM_LIMIT),
        name="diffattn",
    )(proj3, proj3, proj3, subln_w, lq1, lk1, lq2, lk2)


def _rotary_tables():
    f32 = np.float32
    pos = np.arange(T_PAD, dtype=f32) - f32(PAD)
    angle = (f32(RET_THETA) ** (-np.linspace(0.0, 1.0, RET_DK // 2, dtype=f32))).astype(f32)
    fr = pos[:, None] * angle[None, :]
    c, s = np.cos(fr), np.sin(fr)
    zero = np.zeros_like(s)
    cos_i = np.repeat(c, 2, axis=-1)
    sin_even = np.stack([-s, zero], axis=-1).reshape(T_PAD, RET_DK)
    sin_odd = np.stack([zero, s], axis=-1).reshape(T_PAD, RET_DK)
    ks = f32(RET_DK ** -0.5)
    inv = (f32(ROPE_THETA) ** (-np.arange(0, ROPE_DIMS, 2, dtype=f32) / f32(ROPE_DIMS))).astype(f32)
    fq = pos[:, None] * inv[None, :]
    emb = np.concatenate([fq, fq], axis=-1)
    ce, se = np.cos(emb), np.sin(emb)
    half, rest = ROPE_DIMS // 2, DIFF_D - ROPE_DIMS
    zeros = lambda n: np.zeros((T_PAD, n), f32)
    two = lambda a: np.concatenate([a, a], axis=-1)
    c_d = two(np.concatenate([ce, np.ones((T_PAD, rest), f32)], axis=-1))
    s_lo = two(np.concatenate([zeros(half), se[:, half:], zeros(rest)], axis=-1))
    s_hi = two(np.concatenate([-se[:, :half], zeros(half + rest)], axis=-1))
    qs = f32((DIFF_D ** -0.5) * math.log2(math.e))
    tabs = [cos_i, sin_even, sin_odd, cos_i * ks, sin_even * ks, sin_odd * ks,
            c_d * qs, s_lo * qs, s_hi * qs, c_d, s_lo, s_hi]
    return np.concatenate(tabs, axis=-1).astype(f32)


def _retention_consts():
    f32 = np.float32
    log_gamma = np.log(f32(1.0) - f32(2.0) ** (f32(-5.0) - np.arange(RET_HEADS, dtype=f32))).astype(f32)
    idx = np.arange(CHUNK, dtype=f32)
    rel = idx[:, None] - idx[None, :]
    dintra = np.where(rel >= 0, np.exp(log_gamma[:, None, None] * np.maximum(rel, f32(0.0))), f32(0.0))
    k_decay = np.exp(log_gamma[:, None] * (CHUNK - 1 - idx)[None, :])
    q_decay = np.exp(log_gamma[:, None] * (idx + f32(1.0))[None, :])
    cd = np.exp(log_gamma * f32(CHUNK))
    bc = lambda a: np.ascontiguousarray(np.broadcast_to(a[:, :, None], (RET_HEADS, CHUNK, LANES)))
    return cd.astype(f32), dintra.astype(f32), bc(q_decay).astype(f32), bc(k_decay).astype(f32)


def kernel(x, meta_tokens, ffn1_norm, ffn1_w_gate, ffn1_w_up, ffn1_w_down, mix_norm, w_in, ret_gn_w, diff_subln_w, diff_lambda_q1, diff_lambda_k1, diff_lambda_q2, diff_lambda_k2, w_out, ffn2_norm, ffn2_w_gate, ffn2_w_up, ffn2_w_down, final_norm):
    meta = jnp.broadcast_to(meta_tokens.astype(F32)[None], (BATCH, N_META, D_MODEL))
    h = jnp.concatenate([jnp.zeros((BATCH, PAD, D_MODEL), F32), meta, x.astype(F32)], axis=1)
    h = h.reshape(ROWS, D_MODEL)

    tab = jnp.asarray(_rotary_tables())
    cd, dintra, qdec, kdec = (jnp.asarray(a) for a in _retention_consts())

    for l in range(DEPTH):
        h = _ffn(l, h, ffn1_norm, ffn1_w_gate, ffn1_w_up, ffn1_w_down)
        proj = _inproj(l, h, mix_norm, w_in, tab)
        y_ret = _retention(l, proj.reshape(BATCH, N_CHUNKS, CHUNK, D_IN), cd, dintra, qdec, kdec,
                           ret_gn_w)
        y_diff = _diffattn(l, proj.reshape(BATCH, T_PAD, D_IN), diff_subln_w, diff_lambda_q1,
                           diff_lambda_k1, diff_lambda_q2, diff_lambda_k2)
        h = _ffn(l, h, ffn2_norm, ffn2_w_gate, ffn2_w_up, ffn2_w_down,
                 mix=(y_ret.reshape(ROWS, RET_WIDTH), y_diff.reshape(ROWS, DIFF_WIDTH), w_out),
                 final_g=final_norm.reshape(1, D_MODEL) if l == DEPTH - 1 else None)

    return h.reshape(BATCH, T_PAD, D_MODEL)[:, PAD + N_META:].astype(x.dtype)
```

```python
import functools
import math

import numpy as np
import jax
import jax.numpy as jnp
from jax import lax
from jax.experimental import pallas as pl
from jax.experimental.pallas import tpu as pltpu

D_MODEL = 1024
BATCH = 8
SEQ = 2048
DEPTH = 2
N_META = 16
CHUNK = 128
RET_HEADS = 4
RET_DK = 128
RET_WIDTH = 512
RET_THETA = 10000.0
DIFF_HEADS = 4
DIFF_D = 64
DIFF_DV = 128
DIFF_WIDTH = 512
ROPE_THETA = 500000.0
ROPE_DIMS = 16
D_FF = 2816
EPS = 1e-6
D_IN = 3584

PAD = CHUNK - N_META
FRAME_CHUNKS = SEQ // CHUNK
T_PAD = CHUNK + SEQ
FRAME_ROWS = BATCH * SEQ
ROW_TILE = 512
N_ROW_TILES = FRAME_ROWS // ROW_TILE
N_WCHUNK = 8
N_TAB = 12
LANES = 128
VMEM_LIMIT = 56 * 1024 * 1024
NEG = -1e30

F32 = jnp.float32
BF16 = jnp.bfloat16


def _rms(x, g):
    return x * lax.rsqrt(jnp.mean(x * x, axis=-1, keepdims=True) + EPS) * g


def _whole(shape):
    return pl.BlockSpec(shape, lambda *_: (0,) * len(shape))


def _tile_index(i):
    return jnp.clip(i - N_WCHUNK, 0, N_ROW_TILES - 1)


def _row_tile(width):
    return pl.BlockSpec((ROW_TILE, width), lambda i: (_tile_index(i), 0))


def _weight_chunk(layer, rows, cols):
    return pl.BlockSpec((None, rows // N_WCHUNK, cols),
                        lambda i: (layer, jnp.minimum(i, N_WCHUNK - 1), 0))


def _stage_weight(i, src_ref, dst_ref):
    rows = src_ref.shape[0]
    dst_ref[pl.ds(pl.multiple_of(i * rows, 16), rows), :] = src_ref[...].astype(BF16)


def _dense_steps(i, stage, frames, meta):
    pl.when(i < N_WCHUNK)(stage)
    pl.when((i >= N_WCHUNK) & (i < N_WCHUNK + N_ROW_TILES))(frames)
    if meta is not None:
        pl.when(i == N_WCHUNK + N_ROW_TILES)(meta)


def _ffn_kernel(*refs, layer, mix, final):
    refs = list(refs)
    take = lambda n: [refs.pop(0) for _ in range(n)]
    with_meta = not final
    h_refs = take(2 if with_meta else 1)
    if mix:
        yr_refs, yd_refs = take(len(h_refs)), take(len(h_refs))
        (wo_ref,) = take(1)
    g_ref, wg_ref, wu_ref, wd_ref = take(4)
    if final:
        (fn_ref,) = take(1)
    o_refs = take(len(h_refs))
    wg_s, wu_s, wd_s = take(3)
    if mix:
        (wo_s,) = take(1)
    i = pl.program_id(0)

    def stage():
        _stage_weight(i, wg_ref, wg_s)
        _stage_weight(i, wu_ref, wu_s)
        _stage_weight(i, wd_ref, wd_s)
        if mix:
            _stage_weight(i, wo_ref, wo_s)

    def rows(which):
        x = h_refs[which][...]
        if mix:
            x = (x + jnp.dot(yr_refs[which][...], wo_s[:RET_WIDTH, :], preferred_element_type=F32)
                 + jnp.dot(yd_refs[which][...], wo_s[RET_WIDTH:, :], preferred_element_type=F32))
        xn = _rms(x, g_ref[layer:layer + 1, :]).astype(BF16)
        gate = jnp.dot(xn, wg_s[...], preferred_element_type=F32)
        up = jnp.dot(xn, wu_s[...], preferred_element_type=F32)
        act = (jax.nn.silu(gate) * up).astype(BF16)
        y = x + 0.5 * jnp.dot(act, wd_s[...], preferred_element_type=F32)
        if final:
            y = _rms(y, fn_ref[...])
        o_refs[which][...] = y

    _dense_steps(i, stage, functools.partial(rows, 0),
                 functools.partial(rows, 1) if with_meta else None)


def _ffn(layer, h, norm_g, wg, wu, wd, mix=None, final_g=None):
    with_meta = final_g is None
    pair = lambda width: [_row_tile(width)] + ([_whole((CHUNK, width))] if with_meta else [])
    keep = lambda arrays: list(arrays) if with_meta else [arrays[0]]
    args, specs = keep(h), pair(D_MODEL)
    scratch = [pltpu.VMEM((D_MODEL, D_FF), BF16), pltpu.VMEM((D_MODEL, D_FF), BF16),
               pltpu.VMEM((D_FF, D_MODEL), BF16)]
    if mix is not None:
        y_ret, y_diff, w_out = mix
        args += keep(y_ret) + keep(y_diff) + [w_out]
        specs += pair(RET_WIDTH) + pair(DIFF_WIDTH) + [_weight_chunk(layer, D_MODEL, D_MODEL)]
        scratch.append(pltpu.VMEM((D_MODEL, D_MODEL), BF16))
    args += [norm_g, wg, wu, wd]
    specs += [_whole((DEPTH, D_MODEL)), _weight_chunk(layer, D_MODEL, D_FF),
              _weight_chunk(layer, D_MODEL, D_FF), _weight_chunk(layer, D_FF, D_MODEL)]
    if final_g is not None:
        args.append(final_g)
        specs.append(_whole((1, D_MODEL)))
    out_shape = [jax.ShapeDtypeStruct((FRAME_ROWS, D_MODEL), F32)]
    if with_meta:
        out_shape.append(jax.ShapeDtypeStruct((CHUNK, D_MODEL), F32))
    return pl.pallas_call(
        functools.partial(_ffn_kernel, layer=layer, mix=mix is not None,
                          final=final_g is not None),
        grid=(N_WCHUNK + N_ROW_TILES + int(with_meta),),
        in_specs=specs,
        out_specs=pair(D_MODEL),
        out_shape=out_shape,
        scratch_shapes=scratch,
        compiler_params=pltpu.CompilerParams(
            dimension_semantics=("arbitrary",), vmem_limit_bytes=VMEM_LIMIT),
        name="ffn",
    )(*args)


_ROTATIONS = (
    (0, 0, LANES - 1, 1),
    (RET_WIDTH, 3, LANES - 1, 1),
    (4 * RET_WIDTH, 6, 8, LANES - 8),
    (4 * RET_WIDTH + DIFF_WIDTH, 9, 8, LANES - 8),
)


def _inproj_kernel(hf_ref, hm_ref, g_ref, w_ref, tabf_ref, tabm_ref, of_ref, om_ref, w_s, *,
                   layer):
    i = pl.program_id(0)

    def stage():
        _stage_weight(i, w_ref, w_s)

    def rows(h_ref, tab_ref, o_ref):
        xn = _rms(h_ref[...], g_ref[layer:layer + 1, :]).astype(BF16)
        p = jnp.dot(xn, w_s[...], preferred_element_type=F32)

        def tab(t):
            return tab_ref[:, t * LANES:(t + 1) * LANES]

        for base, t0, roll_a, roll_b in _ROTATIONS:
            for hd in range(RET_HEADS):
                sl = slice(base + hd * LANES, base + (hd + 1) * LANES)
                xs = p[:, sl]
                o_ref[:, sl] = (xs * tab(t0) + pltpu.roll(xs, roll_a, 1) * tab(t0 + 1)
                                + pltpu.roll(xs, roll_b, 1) * tab(t0 + 2)).astype(BF16)
        for lo, hi in ((2 * RET_WIDTH, 4 * RET_WIDTH), (D_IN - DIFF_WIDTH, D_IN)):
            o_ref[:, lo:hi] = p[:, lo:hi].astype(BF16)

    _dense_steps(i, stage, functools.partial(rows, hf_ref, tabf_ref, of_ref),
                 functools.partial(rows, hm_ref, tabm_ref, om_ref))


def _inproj(layer, h, norm_g, w_in, tab_frames, tab_meta):
    tiles_per_batch = SEQ // ROW_TILE
    return pl.pallas_call(
        functools.partial(_inproj_kernel, layer=layer),
        grid=(N_WCHUNK + N_ROW_TILES + 1,),
        in_specs=[_row_tile(D_MODEL), _whole((CHUNK, D_MODEL)), _whole((DEPTH, D_MODEL)),
                  _weight_chunk(layer, D_MODEL, D_IN),
                  pl.BlockSpec((ROW_TILE, N_TAB * LANES),
                               lambda i: (_tile_index(i) % tiles_per_batch, 0)),
                  _whole((CHUNK, N_TAB * LANES))],
        out_specs=[_row_tile(D_IN), _whole((CHUNK, D_IN))],
        out_shape=[jax.ShapeDtypeStruct((FRAME_ROWS, D_IN), BF16),
                   jax.ShapeDtypeStruct((CHUNK, D_IN), BF16)],
        scratch_shapes=[pltpu.VMEM((D_MODEL, D_IN), BF16)],
        compiler_params=pltpu.CompilerParams(
            dimension_semantics=("arbitrary",), vmem_limit_bytes=VMEM_LIMIT),
        name="inproj",
    )(h[0], h[1], norm_g, w_in, tab_frames, tab_meta)


def _retention_kernel(cd_ref, qf_ref, kf_ref, vf_ref, gf_ref, qm_ref, km_ref, vm_ref, gm_ref,
                      dintra_ref, qdec_ref, kdec_ref, gnw_ref, of_ref, om_ref, state_ref, *,
                      layer):
    nt = (((1,), (1,)), ((), ()))
    tn = (((0,), (0,)), ((), ()))
    n = pl.program_id(0)

    def unit(hd, q, k, v, g, r_prev):
        sl = slice(hd * LANES, (hd + 1) * LANES)
        s = lax.dot_general(q, k, nt, preferred_element_type=F32) * dintra_ref[hd]
        o = jnp.dot(s.astype(BF16), v, preferred_element_type=F32)
        if r_prev is not None:
            qd = (q.astype(F32) * qdec_ref[hd]).astype(BF16)
            o = o + jnp.dot(qd, r_prev.astype(BF16), preferred_element_type=F32)
        kd = (k.astype(F32) * kdec_ref[hd]).astype(BF16)
        inc = lax.dot_general(kd, v, tn, preferred_element_type=F32)
        mu = jnp.mean(o, axis=-1, keepdims=True)
        d = o - mu
        var = jnp.mean(d * d, axis=-1, keepdims=True)
        on = d * lax.rsqrt(var + EPS) * gnw_ref[layer:layer + 1, sl]
        return (jax.nn.silu(g.astype(F32)) * on).astype(BF16), inc

    @pl.when(n == 0)
    def _():
        for hd in range(RET_HEADS):
            sl = slice(hd * LANES, (hd + 1) * LANES)
            y, inc = unit(hd, qm_ref[:, sl], km_ref[:, sl], vm_ref[:, sl], gm_ref[:, sl], None)
            om_ref[:, sl] = y
            for b in range(BATCH):
                state_ref[b, hd] = inc

    @pl.when(n > 0)
    def _():
        for b in range(BATCH):
            for hd in range(RET_HEADS):
                sl = slice(hd * LANES, (hd + 1) * LANES)
                r = state_ref[b, hd]
                y, inc = unit(hd, qf_ref[b, 0, :, sl], kf_ref[b, 0, :, sl], vf_ref[b, 0, :, sl],
                              gf_ref[b, 0, :, sl], r)
                of_ref[b, 0, :, sl] = y
                state_ref[b, hd] = r * cd_ref[hd] + inc


def _retention(layer, proj, cd, dintra, qdec, kdec, gn_w):
    proj_frames, proj_meta = proj
    frames4 = proj_frames.reshape(BATCH, FRAME_CHUNKS, CHUNK, D_IN)
    chunk = lambda n: jnp.maximum(n - 1, 0)
    fblk = lambda c: pl.BlockSpec((BATCH, 1, CHUNK, RET_WIDTH), lambda n, c=c: (0, chunk(n), 0, c))
    mblk = lambda c: pl.BlockSpec((CHUNK, RET_WIDTH), lambda n, c=c: (0, c))
    const3 = _whole((RET_HEADS, CHUNK, LANES))
    y_frames, y_meta = pl.pallas_call(
        functools.partial(_retention_kernel, layer=layer),
        grid=(1 + FRAME_CHUNKS,),
        in_specs=[pl.BlockSpec(memory_space=pltpu.SMEM),
                  fblk(0), fblk(1), fblk(2), fblk(3), mblk(0), mblk(1), mblk(2), mblk(3),
                  const3, const3, const3, _whole((DEPTH, RET_WIDTH))],
        out_specs=[fblk(0), mblk(0)],
        out_shape=[jax.ShapeDtypeStruct((BATCH, FRAME_CHUNKS, CHUNK, RET_WIDTH), BF16),
                   jax.ShapeDtypeStruct((CHUNK, RET_WIDTH), BF16)],
        scratch_shapes=[pltpu.VMEM((BATCH, RET_HEADS, RET_DK, LANES), F32)],
        compiler_params=pltpu.CompilerParams(
            dimension_semantics=("arbitrary",), vmem_limit_bytes=VMEM_LIMIT),
        name="retention",
    )(cd, frames4, frames4, frames4, frames4, proj_meta, proj_meta, proj_meta, proj_meta,
      dintra, qdec, kdec, gn_w)
    return y_frames.reshape(FRAME_ROWS, RET_WIDTH), y_meta


_QUERY_BLOCKS = ((0, CHUNK),) + tuple((r, 2 * CHUNK) for r in range(CHUNK, T_PAD, 2 * CHUNK))


def _key_tiles(nk):
    return ((0, CHUNK),) + tuple((r, 2 * CHUNK) for r in range(CHUNK, nk, 2 * CHUNK))


def _diffattn_kernel(qf_ref, kf_ref, vf_ref, qm_ref, km_ref, vm_ref, w_ref, lq1_ref, lk1_ref,
                     lq2_ref, lk2_ref, of_ref, om_ref, s_ref, p_ref, *, layer, lambda_init):
    row = slice(layer, layer + 1)
    lam = (jnp.exp(jnp.sum(lq1_ref[row, :] * lk1_ref[row, :], axis=-1, keepdims=True))
           - jnp.exp(jnp.sum(lq2_ref[row, :] * lk2_ref[row, :], axis=-1, keepdims=True))
           + lambda_init)
    nt = (((1,), (1,)), ((), ()))
    tn = (((0,), (0,)), ((), ()))

    def rows(meta_ref, frames_ref, r0, n):
        return meta_ref[...] if r0 == 0 else frames_ref[0, r0 - CHUNK:r0 - CHUNK + n, :]

    def fold(state, key, value, op):
        state[key] = value if key not in state else op(state[key], value)

    def score_tiles(blk, state):
        r0, nq = _QUERY_BLOCKS[blk]
        nk, slot = r0 + nq, blk % 2
        q = rows(qm_ref, qf_ref, r0, nq)
        lane = lax.broadcasted_iota(jnp.int32, (nq, LANES), 1)
        zero = jnp.zeros_like(q)
        qq = jnp.concatenate([jnp.where(lane < DIFF_D, q, zero),
                              jnp.where(lane >= DIFF_D, q, zero)], axis=0)

        def tile(t0, tk):
            s = lax.dot_general(rows(km_ref, kf_ref, t0, tk), qq, nt, preferred_element_type=F32)
            krow = lax.broadcasted_iota(jnp.int32, (tk, 2 * nq), 0)
            if t0 + tk == nk:
                qcol = lax.broadcasted_iota(jnp.int32, (tk, 2 * nq), 1) % nq
                s = jnp.where(krow <= qcol, s, NEG)
            if t0 == 0:
                s = jnp.where(krow >= PAD, s, NEG)
            s_ref[slot, t0:t0 + tk, :2 * nq] = s
            fold(state, "max", jnp.max(s, axis=0, keepdims=True), jnp.maximum)

        return [functools.partial(tile, t0, tk) for t0, tk in _key_tiles(nk)]

    def prob_tiles(blk, smax, state):
        r0, nq = _QUERY_BLOCKS[blk]
        nk, slot = r0 + nq, blk % 2

        def tile(t0, tk):
            p = jnp.exp2(s_ref[slot, t0:t0 + tk, :2 * nq] - smax)
            fold(state, "pos", jnp.sum(p[:, :nq], axis=0, keepdims=True), jnp.add)
            fold(state, "neg", jnp.sum(p[:, nq:], axis=0, keepdims=True), jnp.add)
            p_ref[slot, t0:t0 + tk, :2 * nq] = p.astype(BF16)

        return [functools.partial(tile, t0, tk) for t0, tk in _key_tiles(nk)]

    def finish(blk, sums):
        r0, nq = _QUERY_BLOCKS[blk]
        nk, slot = r0 + nq, blk % 2
        ot = lax.dot_general(vm_ref[...], p_ref[slot, :CHUNK, :2 * nq], tn,
                             preferred_element_type=F32)
        if nk > CHUNK:
            ot = ot + lax.dot_general(vf_ref[0, :nk - CHUNK, :], p_ref[slot, CHUNK:nk, :2 * nq], tn,
                                      preferred_element_type=F32)
        o = (ot[:, :nq] * (1.0 / sums["pos"]) - ot[:, nq:] * (lam / sums["neg"])).T
        o = o * lax.rsqrt(jnp.mean(o * o, axis=-1, keepdims=True) + EPS) * w_ref[row, :]
        o = o * (1.0 - lambda_init)
        if r0 == 0:
            orow = lax.broadcasted_iota(jnp.int32, (nq, LANES), 0)
            om_ref[...] = jnp.where(orow >= PAD, o, 0.0).astype(BF16)
        else:
            of_ref[0, r0 - CHUNK:nk - CHUNK, :] = o.astype(BF16)

    score_state = {}
    for run in score_tiles(0, score_state):
        run()
    for blk in range(len(_QUERY_BLOCKS)):
        smax, sums, score_state = score_state["max"], {}, {}
        ahead = score_tiles(blk + 1, score_state) if blk + 1 < len(_QUERY_BLOCKS) else []
        current = prob_tiles(blk, smax, sums)
        for n in range(max(len(ahead), len(current))):
            for runs in (current, ahead):
                if n < len(runs):
                    runs[n]()
        finish(blk, sums)


def _diffattn(layer, proj, subln_w, lq1, lk1, lq2, lk2):
    proj_frames, proj_meta = proj
    frames3 = proj_frames.reshape(BATCH, SEQ, D_IN)
    lambda_init = 0.8 - 0.6 * math.exp(-0.3 * layer)
    first = 4 * RET_WIDTH // LANES
    fblk = lambda c: pl.BlockSpec((1, SEQ, LANES), lambda hd, b, c=c: (b, 0, first + c + hd))
    mblk = lambda c: pl.BlockSpec((CHUNK, LANES), lambda hd, b, c=c: (0, first + c + hd))
    small = lambda n: _whole((DEPTH, n))
    y_frames, y_meta = pl.pallas_call(
        functools.partial(_diffattn_kernel, layer=layer, lambda_init=lambda_init),
        grid=(DIFF_HEADS, BATCH),
        in_specs=[fblk(0), fblk(DIFF_HEADS), fblk(2 * DIFF_HEADS),
                  mblk(0), mblk(DIFF_HEADS), mblk(2 * DIFF_HEADS),
                  small(DIFF_DV), small(DIFF_D), small(DIFF_D), small(DIFF_D), small(DIFF_D)],
        out_specs=[pl.BlockSpec((1, SEQ, LANES), lambda hd, b: (b, 0, hd)),
                   pl.BlockSpec((CHUNK, LANES), lambda hd, b: (0, hd))],
        out_shape=[jax.ShapeDtypeStruct((BATCH, SEQ, DIFF_WIDTH), BF16),
                   jax.ShapeDtypeStruct((CHUNK, DIFF_WIDTH), BF16)],
        scratch_shapes=[pltpu.VMEM((2, T_PAD, 4 * CHUNK), F32),
                        pltpu.VMEM((2, T_PAD, 4 * CHUNK), BF16)],
        compiler_params=pltpu.CompilerParams(
            dimension_semantics=("arbitrary", "arbitrary"), vmem_limit_bytes=VMEM_LIMIT),
        name="diffattn",
    )(frames3, frames3, frames3, proj_meta, proj_meta, proj_meta, subln_w, lq1, lk1, lq2, lk2)
    return y_frames.reshape(FRAME_ROWS, DIFF_WIDTH), y_meta


def _rotary_tables():
    f32 = np.float32
    pos = np.arange(T_PAD, dtype=f32) - f32(PAD)
    angle = (f32(RET_THETA) ** (-np.linspace(0.0, 1.0, RET_DK // 2, dtype=f32))).astype(f32)
    fr = pos[:, None] * angle[None, :]
    c, s = np.cos(fr), np.sin(fr)
    zero = np.zeros_like(s)
    cos_i = np.repeat(c, 2, axis=-1)
    sin_even = np.stack([-s, zero], axis=-1).reshape(T_PAD, RET_DK)
    sin_odd = np.stack([zero, s], axis=-1).reshape(T_PAD, RET_DK)
    ks = f32(RET_DK ** -0.5)
    inv = (f32(ROPE_THETA) ** (-np.arange(0, ROPE_DIMS, 2, dtype=f32) / f32(ROPE_DIMS))).astype(f32)
    fq = pos[:, None] * inv[None, :]
    emb = np.concatenate([fq, fq], axis=-1)
    ce, se = np.cos(emb), np.sin(emb)
    half, rest = ROPE_DIMS // 2, DIFF_D - ROPE_DIMS
    zeros = lambda n: np.zeros((T_PAD, n), f32)
    two = lambda a: np.concatenate([a, a], axis=-1)
    c_d = two(np.concatenate([ce, np.ones((T_PAD, rest), f32)], axis=-1))
    s_lo = two(np.concatenate([zeros(half), se[:, half:], zeros(rest)], axis=-1))
    s_hi = two(np.concatenate([-se[:, :half], zeros(half + rest)], axis=-1))
    qs = f32((DIFF_D ** -0.5) * math.log2(math.e))
    tabs = [cos_i, sin_even, sin_odd, cos_i * ks, sin_even * ks, sin_odd * ks,
            c_d * qs, s_lo * qs, s_hi * qs, c_d, s_lo, s_hi]
    return np.concatenate(tabs, axis=-1).astype(f32)


def _retention_consts():
    f32 = np.float32
    log_gamma = np.log(f32(1.0) - f32(2.0) ** (f32(-5.0) - np.arange(RET_HEADS, dtype=f32))).astype(f32)
    idx = np.arange(CHUNK, dtype=f32)
    rel = idx[:, None] - idx[None, :]
    dintra = np.where(rel >= 0, np.exp(log_gamma[:, None, None] * np.maximum(rel, f32(0.0))), f32(0.0))
    k_decay = np.exp(log_gamma[:, None] * (CHUNK - 1 - idx)[None, :])
    q_decay = np.exp(log_gamma[:, None] * (idx + f32(1.0))[None, :])
    cd = np.exp(log_gamma * f32(CHUNK))
    bc = lambda a: np.ascontiguousarray(np.broadcast_to(a[:, :, None], (RET_HEADS, CHUNK, LANES)))
    return cd.astype(f32), dintra.astype(f32), bc(q_decay).astype(f32), bc(k_decay).astype(f32)


def kernel(x, meta_tokens, ffn1_norm, ffn1_w_gate, ffn1_w_up, ffn1_w_down, mix_norm, w_in, ret_gn_w, diff_subln_w, diff_lambda_q1, diff_lambda_k1, diff_lambda_q2, diff_lambda_k2, w_out, ffn2_norm, ffn2_w_gate, ffn2_w_up, ffn2_w_down, final_norm):
    meta_chunk = jnp.concatenate([jnp.zeros((PAD, D_MODEL), F32), meta_tokens.astype(F32)], axis=0)
    h = (x.astype(F32).reshape(FRAME_ROWS, D_MODEL), meta_chunk)

    tab = _rotary_tables()
    tab_meta, tab_frames = jnp.asarray(tab[:CHUNK]), jnp.asarray(tab[CHUNK:])
    cd, dintra, qdec, kdec = (jnp.asarray(a) for a in _retention_consts())

    for l in range(DEPTH):
        last = l == DEPTH - 1
        h = _ffn(l, h, ffn1_norm, ffn1_w_gate, ffn1_w_up, ffn1_w_down)
        proj = _inproj(l, h, mix_norm, w_in, tab_frames, tab_meta)
        y_ret = _retention(l, proj, cd, dintra, qdec, kdec, ret_gn_w)
        y_diff = _diffattn(l, proj, diff_subln_w, diff_lambda_q1, diff_lambda_k1, diff_lambda_q2,
                           diff_lambda_k2)
        h = _ffn(l, h, ffn2_norm, ffn2_w_gate, ffn2_w_up, ffn2_w_down,
                 mix=(y_ret, y_diff, w_out),
                 final_g=final_norm.reshape(1, D_MODEL) if last else None)

    return h[0].reshape(BATCH, SEQ, D_MODEL).astype(x.dtype)
```

```python
import functools
import math

import numpy as np
import jax
import jax.numpy as jnp
from jax import lax
from jax.experimental import pallas as pl
from jax.experimental.pallas import tpu as pltpu

D_MODEL = 1024
BATCH = 8
SEQ = 2048
DEPTH = 2
N_META = 16
CHUNK = 128
RET_HEADS = 4
RET_DK = 128
RET_WIDTH = 512
RET_THETA = 10000.0
DIFF_HEADS = 4
DIFF_D = 64
DIFF_DV = 128
DIFF_WIDTH = 512
ROPE_THETA = 500000.0
ROPE_DIMS = 16
D_FF = 2816
EPS = 1e-6
D_IN = 3584

PAD = CHUNK - N_META
FRAME_CHUNKS = SEQ // CHUNK
T_PAD = CHUNK + SEQ
FRAME_ROWS = BATCH * SEQ
ROW_TILE = 512
N_ROW_TILES = FRAME_ROWS // ROW_TILE
N_WCHUNK = 8
N_TAB = 12
LANES = 128
VMEM_LIMIT = 56 * 1024 * 1024
NEG = -1e30

F32 = jnp.float32
BF16 = jnp.bfloat16


def _rms(x, g):
    return x * lax.rsqrt(jnp.mean(x * x, axis=-1, keepdims=True) + EPS) * g


def _whole(shape):
    return pl.BlockSpec(shape, lambda *_: (0,) * len(shape))


def _tile_index(i):
    return jnp.clip(i - N_WCHUNK, 0, N_ROW_TILES - 1)


def _row_tile(width):
    return pl.BlockSpec((ROW_TILE, width), lambda i: (_tile_index(i), 0))


def _weight_chunk(layer, rows, cols):
    return pl.BlockSpec((None, rows // N_WCHUNK, cols),
                        lambda i: (layer, jnp.minimum(i, N_WCHUNK - 1), 0))


def _stage_weight(i, src_ref, dst_ref):
    rows = src_ref.shape[0]
    dst_ref[pl.ds(pl.multiple_of(i * rows, 16), rows), :] = src_ref[...].astype(BF16)


def _dense_steps(i, stage, frames, meta):
    pl.when(i < N_WCHUNK)(stage)
    pl.when((i >= N_WCHUNK) & (i < N_WCHUNK + N_ROW_TILES))(frames)
    if meta is not None:
        pl.when(i == N_WCHUNK + N_ROW_TILES)(meta)


def _ffn_kernel(*refs, layer, mix, final):
    refs = list(refs)
    take = lambda n: [refs.pop(0) for _ in range(n)]
    with_meta = not final
    h_refs = take(2 if with_meta else 1)
    if mix:
        yr_refs, yd_refs = take(len(h_refs)), take(len(h_refs))
        (wo_ref,) = take(1)
    g_ref, wg_ref, wu_ref, wd_ref = take(4)
    if final:
        (fn_ref,) = take(1)
    o_refs = take(len(h_refs))
    wg_s, wu_s, wd_s = take(3)
    if mix:
        (wo_s,) = take(1)
    i = pl.program_id(0)

    def stage():
        _stage_weight(i, wg_ref, wg_s)
        _stage_weight(i, wu_ref, wu_s)
        _stage_weight(i, wd_ref, wd_s)
        if mix:
            _stage_weight(i, wo_ref, wo_s)

    def rows(which):
        x = h_refs[which][...]
        if mix:
            x = (x + jnp.dot(yr_refs[which][...], wo_s[:RET_WIDTH, :], preferred_element_type=F32)
                 + jnp.dot(yd_refs[which][...], wo_s[RET_WIDTH:, :], preferred_element_type=F32))
        xn = _rms(x, g_ref[layer:layer + 1, :]).astype(BF16)
        gate = jnp.dot(xn, wg_s[...], preferred_element_type=F32)
        up = jnp.dot(xn, wu_s[...], preferred_element_type=F32)
        act = (jax.nn.silu(gate) * up).astype(BF16)
        y = x + 0.5 * jnp.dot(act, wd_s[...], preferred_element_type=F32)
        if final:
            y = _rms(y, fn_ref[...])
        o_refs[which][...] = y

    _dense_steps(i, stage, functools.partial(rows, 0),
                 functools.partial(rows, 1) if with_meta else None)


def _ffn(layer, h, norm_g, wg, wu, wd, mix=None, final_g=None):
    with_meta = final_g is None
    pair = lambda width: [_row_tile(width)] + ([_whole((CHUNK, width))] if with_meta else [])
    keep = lambda arrays: list(arrays) if with_meta else [arrays[0]]
    args, specs = keep(h), pair(D_MODEL)
    scratch = [pltpu.VMEM((D_MODEL, D_FF), BF16), pltpu.VMEM((D_MODEL, D_FF), BF16),
               pltpu.VMEM((D_FF, D_MODEL), BF16)]
    if mix is not None:
        y_ret, y_diff, w_out = mix
        args += keep(y_ret) + keep(y_diff) + [w_out]
        specs += pair(RET_WIDTH) + pair(DIFF_WIDTH) + [_weight_chunk(layer, D_MODEL, D_MODEL)]
        scratch.append(pltpu.VMEM((D_MODEL, D_MODEL), BF16))
    args += [norm_g, wg, wu, wd]
    specs += [_whole((DEPTH, D_MODEL)), _weight_chunk(layer, D_MODEL, D_FF),
              _weight_chunk(layer, D_MODEL, D_FF), _weight_chunk(layer, D_FF, D_MODEL)]
    if final_g is not None:
        args.append(final_g)
        specs.append(_whole((1, D_MODEL)))
    out_shape = [jax.ShapeDtypeStruct((FRAME_ROWS, D_MODEL), F32)]
    if with_meta:
        out_shape.append(jax.ShapeDtypeStruct((CHUNK, D_MODEL), F32))
    return pl.pallas_call(
        functools.partial(_ffn_kernel, layer=layer, mix=mix is not None,
                          final=final_g is not None),
        grid=(N_WCHUNK + N_ROW_TILES + int(with_meta),),
        in_specs=specs,
        out_specs=pair(D_MODEL),
        out_shape=out_shape,
        scratch_shapes=scratch,
        compiler_params=pltpu.CompilerParams(
            dimension_semantics=("arbitrary",), vmem_limit_bytes=VMEM_LIMIT),
        name="ffn",
    )(*args)


_ROTATIONS = (
    (0, 0, LANES - 1, 1),
    (RET_WIDTH, 3, LANES - 1, 1),
    (4 * RET_WIDTH, 6, 8, LANES - 8),
    (4 * RET_WIDTH + DIFF_WIDTH, 9, 8, LANES - 8),
)


def _inproj_kernel(hf_ref, hm_ref, g_ref, w_ref, tabf_ref, tabm_ref, of_ref, om_ref, w_s, *,
                   layer):
    i = pl.program_id(0)

    def stage():
        _stage_weight(i, w_ref, w_s)

    def rows(h_ref, tab_ref, o_ref):
        xn = _rms(h_ref[...], g_ref[layer:layer + 1, :]).astype(BF16)
        p = jnp.dot(xn, w_s[...], preferred_element_type=F32)

        def tab(t):
            return tab_ref[:, t * LANES:(t + 1) * LANES]

        for base, t0, roll_a, roll_b in _ROTATIONS:
            for hd in range(RET_HEADS):
                sl = slice(base + hd * LANES, base + (hd + 1) * LANES)
                xs = p[:, sl]
                o_ref[:, sl] = (xs * tab(t0) + pltpu.roll(xs, roll_a, 1) * tab(t0 + 1)
                                + pltpu.roll(xs, roll_b, 1) * tab(t0 + 2)).astype(BF16)
        for lo, hi in ((2 * RET_WIDTH, 4 * RET_WIDTH), (D_IN - DIFF_WIDTH, D_IN)):
            o_ref[:, lo:hi] = p[:, lo:hi].astype(BF16)

    _dense_steps(i, stage, functools.partial(rows, hf_ref, tabf_ref, of_ref),
                 functools.partial(rows, hm_ref, tabm_ref, om_ref))


def _inproj(layer, h, norm_g, w_in, tab_frames, tab_meta):
    tiles_per_batch = SEQ // ROW_TILE
    return pl.pallas_call(
        functools.partial(_inproj_kernel, layer=layer),
        grid=(N_WCHUNK + N_ROW_TILES + 1,),
        in_specs=[_row_tile(D_MODEL), _whole((CHUNK, D_MODEL)), _whole((DEPTH, D_MODEL)),
                  _weight_chunk(layer, D_MODEL, D_IN),
                  pl.BlockSpec((ROW_TILE, N_TAB * LANES),
                               lambda i: (_tile_index(i) % tiles_per_batch, 0)),
                  _whole((CHUNK, N_TAB * LANES))],
        out_specs=[_row_tile(D_IN), _whole((CHUNK, D_IN))],
        out_shape=[jax.ShapeDtypeStruct((FRAME_ROWS, D_IN), BF16),
                   jax.ShapeDtypeStruct((CHUNK, D_IN), BF16)],
        scratch_shapes=[pltpu.VMEM((D_MODEL, D_IN), BF16)],
        compiler_params=pltpu.CompilerParams(
            dimension_semantics=("arbitrary",), vmem_limit_bytes=VMEM_LIMIT),
        name="inproj",
    )(h[0], h[1], norm_g, w_in, tab_frames, tab_meta)


def _retention_kernel(cd_ref, qf_ref, kf_ref, vf_ref, gf_ref, qm_ref, km_ref, vm_ref, gm_ref,
                      dintra_ref, qdec_ref, kdec_ref, gnw_ref, of_ref, om_ref, state_ref, *,
                      layer):
    nt = (((1,), (1,)), ((), ()))
    tn = (((0,), (0,)), ((), ()))
    n = pl.program_id(0)

    def chunk(units, first):
        scores, incs, outs = [], [], []
        for hd, q, k, v, g, states, out in units:
            s = lax.dot_general(q(), k(), nt, preferred_element_type=F32) * dintra_ref[hd]
            scores.append(s.astype(BF16))
        for hd, q, k, v, g, states, out in units:
            kd = (k().astype(F32) * kdec_ref[hd]).astype(BF16)
            incs.append(lax.dot_general(kd, v(), tn, preferred_element_type=F32))
        for (hd, q, k, v, g, states, out), s in zip(units, scores):
            if first:
                outs.append(jnp.dot(s, v(), preferred_element_type=F32))
            else:
                qd = (q().astype(F32) * qdec_ref[hd]).astype(BF16)
                r = states[0][...].astype(BF16)
                outs.append(jnp.dot(jnp.concatenate([s, qd], axis=1),
                                    jnp.concatenate([v(), r], axis=0),
                                    preferred_element_type=F32))
        for (hd, q, k, v, g, states, out), inc in zip(units, incs):
            for st in states:
                st[...] = inc if first else st[...] * cd_ref[hd] + inc
        for (hd, q, k, v, g, states, out), o in zip(units, outs):
            mu = jnp.mean(o, axis=-1, keepdims=True)
            d = o - mu
            var = jnp.mean(d * d, axis=-1, keepdims=True)
            on = d * lax.rsqrt(var + EPS) * gnw_ref[layer:layer + 1, hd * LANES:(hd + 1) * LANES]
            out[...] = (jax.nn.silu(g().astype(F32)) * on).astype(BF16)

    def head_units(hd, b, refs, out_ref, states):
        sl = slice(hd * LANES, (hd + 1) * LANES)
        if b is None:
            load = lambda ref: (lambda: ref[:, sl])
            out = out_ref.at[:, sl]
        else:
            load = lambda ref: (lambda: ref[b, 0, :, sl])
            out = out_ref.at[b, 0, :, sl]
        q, k, v, g = (load(ref) for ref in refs)
        return (hd, q, k, v, g, states, out)

    @pl.when(n == 0)
    def _():
        chunk([head_units(hd, None, (qm_ref, km_ref, vm_ref, gm_ref), om_ref,
                          [state_ref.at[b, hd] for b in range(BATCH)])
               for hd in range(RET_HEADS)], first=True)

    @pl.when(n > 0)
    def _():
        chunk([head_units(hd, b, (qf_ref, kf_ref, vf_ref, gf_ref), of_ref, [state_ref.at[b, hd]])
               for b in range(BATCH) for hd in range(RET_HEADS)], first=False)


def _retention(layer, proj, cd, dintra, qdec, kdec, gn_w):
    proj_frames, proj_meta = proj
    frames4 = proj_frames.reshape(BATCH, FRAME_CHUNKS, CHUNK, D_IN)
    chunk = lambda n: jnp.maximum(n - 1, 0)
    fblk = lambda c: pl.BlockSpec((BATCH, 1, CHUNK, RET_WIDTH), lambda n, c=c: (0, chunk(n), 0, c))
    mblk = lambda c: pl.BlockSpec((CHUNK, RET_WIDTH), lambda n, c=c: (0, c))
    const3 = _whole((RET_HEADS, CHUNK, LANES))
    y_frames, y_meta = pl.pallas_call(
        functools.partial(_retention_kernel, layer=layer),
        grid=(1 + FRAME_CHUNKS,),
        in_specs=[pl.BlockSpec(memory_space=pltpu.SMEM),
                  fblk(0), fblk(1), fblk(2), fblk(3), mblk(0), mblk(1), mblk(2), mblk(3),
                  const3, const3, const3, _whole((DEPTH, RET_WIDTH))],
        out_specs=[fblk(0), mblk(0)],
        out_shape=[jax.ShapeDtypeStruct((BATCH, FRAME_CHUNKS, CHUNK, RET_WIDTH), BF16),
                   jax.ShapeDtypeStruct((CHUNK, RET_WIDTH), BF16)],
        scratch_shapes=[pltpu.VMEM((BATCH, RET_HEADS, RET_DK, LANES), F32)],
        compiler_params=pltpu.CompilerParams(
            dimension_semantics=("arbitrary",), vmem_limit_bytes=VMEM_LIMIT),
        name="retention",
    )(cd, frames4, frames4, frames4, frames4, proj_meta, proj_meta, proj_meta, proj_meta,
      dintra, qdec, kdec, gn_w)
    return y_frames.reshape(FRAME_ROWS, RET_WIDTH), y_meta


_QUERY_BLOCKS = ((0, CHUNK),) + tuple((r, 2 * CHUNK) for r in range(CHUNK, T_PAD, 2 * CHUNK))


def _key_tiles(nk):
    return ((0, CHUNK),) + tuple((r, 2 * CHUNK) for r in range(CHUNK, nk, 2 * CHUNK))


def _diffattn_kernel(qf_ref, kf_ref, vf_ref, qm_ref, km_ref, vm_ref, w_ref, lq1_ref, lk1_ref,
                     lq2_ref, lk2_ref, of_ref, om_ref, s_ref, p_ref, *, layer, lambda_init):
    row = slice(layer, layer + 1)
    lam = (jnp.exp(jnp.sum(lq1_ref[row, :] * lk1_ref[row, :], axis=-1, keepdims=True))
           - jnp.exp(jnp.sum(lq2_ref[row, :] * lk2_ref[row, :], axis=-1, keepdims=True))
           + lambda_init)
    nt = (((1,), (1,)), ((), ()))
    tn = (((0,), (0,)), ((), ()))

    def rows(meta_ref, frames_ref, r0, n):
        return meta_ref[...] if r0 == 0 else frames_ref[0, r0 - CHUNK:r0 - CHUNK + n, :]

    def fold(state, key, value, op):
        state[key] = value if key not in state else op(state[key], value)

    def score_tiles(blk, state):
        r0, nq = _QUERY_BLOCKS[blk]
        nk, slot = r0 + nq, blk % 2
        q = rows(qm_ref, qf_ref, r0, nq)
        lane = lax.broadcasted_iota(jnp.int32, (nq, LANES), 1)
        zero = jnp.zeros_like(q)
        qq = jnp.concatenate([jnp.where(lane < DIFF_D, q, zero),
                              jnp.where(lane >= DIFF_D, q, zero)], axis=0)

        def tile(t0, tk):
            s = lax.dot_general(rows(km_ref, kf_ref, t0, tk), qq, nt, preferred_element_type=F32)
            krow = lax.broadcasted_iota(jnp.int32, (tk, 2 * nq), 0)
            if t0 + tk == nk:
                qcol = lax.broadcasted_iota(jnp.int32, (tk, 2 * nq), 1) % nq
                s = jnp.where(krow <= qcol, s, NEG)
            if t0 == 0:
                s = jnp.where(krow >= PAD, s, NEG)
            s_ref[slot, t0:t0 + tk, :2 * nq] = s
            fold(state, "max", jnp.max(s, axis=0, keepdims=True), jnp.maximum)

        return [functools.partial(tile, t0, tk) for t0, tk in _key_tiles(nk)]

    def prob_tiles(blk, smax, state):
        r0, nq = _QUERY_BLOCKS[blk]
        nk, slot = r0 + nq, blk % 2

        def tile(t0, tk):
            p = jnp.exp2(s_ref[slot, t0:t0 + tk, :2 * nq] - smax)
            fold(state, "pos", jnp.sum(p[:, :nq], axis=0, keepdims=True), jnp.add)
            fold(state, "neg", jnp.sum(p[:, nq:], axis=0, keepdims=True), jnp.add)
            p_ref[slot, t0:t0 + tk, :2 * nq] = p.astype(BF16)

        return [functools.partial(tile, t0, tk) for t0, tk in _key_tiles(nk)]

    def finish(blk, sums):
        r0, nq = _QUERY_BLOCKS[blk]
        nk, slot = r0 + nq, blk % 2
        ot = lax.dot_general(vm_ref[...], p_ref[slot, :CHUNK, :2 * nq], tn,
                             preferred_element_type=F32)
        if nk > CHUNK:
            ot = ot + lax.dot_general(vf_ref[0, :nk - CHUNK, :], p_ref[slot, CHUNK:nk, :2 * nq], tn,
                                      preferred_element_type=F32)
        o = (ot[:, :nq] * (1.0 / sums["pos"]) - ot[:, nq:] * (lam / sums["neg"])).T
        o = o * lax.rsqrt(jnp.mean(o * o, axis=-1, keepdims=True) + EPS) * w_ref[row, :]
        o = o * (1.0 - lambda_init)
        if r0 == 0:
            orow = lax.broadcasted_iota(jnp.int32, (nq, LANES), 0)
            om_ref[...] = jnp.where(orow >= PAD, o, 0.0).astype(BF16)
        else:
            of_ref[0, r0 - CHUNK:nk - CHUNK, :] = o.astype(BF16)

    score_state = {}
    for run in score_tiles(0, score_state):
        run()
    for blk in range(len(_QUERY_BLOCKS)):
        smax, sums, score_state = score_state["max"], {}, {}
        ahead = score_tiles(blk + 1, score_state) if blk + 1 < len(_QUERY_BLOCKS) else []
        current = prob_tiles(blk, smax, sums)
        for n in range(max(len(ahead), len(current))):
            for runs in (current, ahead):
                if n < len(runs):
                    runs[n]()
        finish(blk, sums)


def _diffattn(layer, proj, subln_w, lq1, lk1, lq2, lk2):
    proj_frames, proj_meta = proj
    frames3 = proj_frames.reshape(BATCH, SEQ, D_IN)
    lambda_init = 0.8 - 0.6 * math.exp(-0.3 * layer)
    first = 4 * RET_WIDTH // LANES
    fblk = lambda c: pl.BlockSpec((1, SEQ, LANES), lambda hd, b, c=c: (b, 0, first + c + hd))
    mblk = lambda c: pl.BlockSpec((CHUNK, LANES), lambda hd, b, c=c: (0, first + c + hd))
    small = lambda n: _whole((DEPTH, n))
    y_frames, y_meta = pl.pallas_call(
        functools.partial(_diffattn_kernel, layer=layer, lambda_init=lambda_init),
        grid=(DIFF_HEADS, BATCH),
        in_specs=[fblk(0), fblk(DIFF_HEADS), fblk(2 * DIFF_HEADS),
                  mblk(0), mblk(DIFF_HEADS), mblk(2 * DIFF_HEADS),
                  small(DIFF_DV), small(DIFF_D), small(DIFF_D), small(DIFF_D), small(DIFF_D)],
        out_specs=[pl.BlockSpec((1, SEQ, LANES), lambda hd, b: (b, 0, hd)),
                   pl.BlockSpec((CHUNK, LANES), lambda hd, b: (0, hd))],
        out_shape=[jax.ShapeDtypeStruct((BATCH, SEQ, DIFF_WIDTH), BF16),
                   jax.ShapeDtypeStruct((CHUNK, DIFF_WIDTH), BF16)],
        scratch_shapes=[pltpu.VMEM((2, T_PAD, 4 * CHUNK), F32),
                        pltpu.VMEM((2, T_PAD, 4 * CHUNK), BF16)],
        compiler_params=pltpu.CompilerParams(
            dimension_semantics=("arbitrary", "arbitrary"), vmem_limit_bytes=VMEM_LIMIT),
        name="diffattn",
    )(frames3, frames3, frames3, proj_meta, proj_meta, proj_meta, subln_w, lq1, lk1, lq2, lk2)
    return y_frames.reshape(FRAME_ROWS, DIFF_WIDTH), y_meta


def _rotary_tables():
    f32 = np.float32
    pos = np.arange(T_PAD, dtype=f32) - f32(PAD)
    angle = (f32(RET_THETA) ** (-np.linspace(0.0, 1.0, RET_DK // 2, dtype=f32))).astype(f32)
    fr = pos[:, None] * angle[None, :]
    c, s = np.cos(fr), np.sin(fr)
    zero = np.zeros_like(s)
    cos_i = np.repeat(c, 2, axis=-1)
    sin_even = np.stack([-s, zero], axis=-1).reshape(T_PAD, RET_DK)
    sin_odd = np.stack([zero, s], axis=-1).reshape(T_PAD, RET_DK)
    ks = f32(RET_DK ** -0.5)
    inv = (f32(ROPE_THETA) ** (-np.arange(0, ROPE_DIMS, 2, dtype=f32) / f32(ROPE_DIMS))).astype(f32)
    fq = pos[:, None] * inv[None, :]
    emb = np.concatenate([fq, fq], axis=-1)
    ce, se = np.cos(emb), np.sin(emb)
    half, rest = ROPE_DIMS // 2, DIFF_D - ROPE_DIMS
    zeros = lambda n: np.zeros((T_PAD, n), f32)
    two = lambda a: np.concatenate([a, a], axis=-1)
    c_d = two(np.concatenate([ce, np.ones((T_PAD, rest), f32)], axis=-1))
    s_lo = two(np.concatenate([zeros(half), se[:, half:], zeros(rest)], axis=-1))
    s_hi = two(np.concatenate([-se[:, :half], zeros(half + rest)], axis=-1))
    qs = f32((DIFF_D ** -0.5) * math.log2(math.e))
    tabs = [cos_i, sin_even, sin_odd, cos_i * ks, sin_even * ks, sin_odd * ks,
            c_d * qs, s_lo * qs, s_hi * qs, c_d, s_lo, s_hi]
    return np.concatenate(tabs, axis=-1).astype(f32)


def _retention_consts():
    f32 = np.float32
    log_gamma = np.log(f32(1.0) - f32(2.0) ** (f32(-5.0) - np.arange(RET_HEADS, dtype=f32))).astype(f32)
    idx = np.arange(CHUNK, dtype=f32)
    rel = idx[:, None] - idx[None, :]
    dintra = np.where(rel >= 0, np.exp(log_gamma[:, None, None] * np.maximum(rel, f32(0.0))), f32(0.0))
    k_decay = np.exp(log_gamma[:, None] * (CHUNK - 1 - idx)[None, :])
    q_decay = np.exp(log_gamma[:, None] * (idx + f32(1.0))[None, :])
    cd = np.exp(log_gamma * f32(CHUNK))
    bc = lambda a: np.ascontiguousarray(np.broadcast_to(a[:, :, None], (RET_HEADS, CHUNK, LANES)))
    return cd.astype(f32), dintra.astype(f32), bc(q_decay).astype(f32), bc(k_decay).astype(f32)


def kernel(x, meta_tokens, ffn1_norm, ffn1_w_gate, ffn1_w_up, ffn1_w_down, mix_norm, w_in, ret_gn_w, diff_subln_w, diff_lambda_q1, diff_lambda_k1, diff_lambda_q2, diff_lambda_k2, w_out, ffn2_norm, ffn2_w_gate, ffn2_w_up, ffn2_w_down, final_norm):
    meta_chunk = jnp.concatenate([jnp.zeros((PAD, D_MODEL), F32), meta_tokens.astype(F32)], axis=0)
    h = (x.astype(F32).reshape(FRAME_ROWS, D_MODEL), meta_chunk)

    tab = _rotary_tables()
    tab_meta, tab_frames = jnp.asarray(tab[:CHUNK]), jnp.asarray(tab[CHUNK:])
    cd, dintra, qdec, kdec = (jnp.asarray(a) for a in _retention_consts())

    for l in range(DEPTH):
        last = l == DEPTH - 1
        h = _ffn(l, h, ffn1_norm, ffn1_w_gate, ffn1_w_up, ffn1_w_down)
        proj = _inproj(l, h, mix_norm, w_in, tab_frames, tab_meta)
        y_ret = _retention(l, proj, cd, dintra, qdec, kdec, ret_gn_w)
        y_diff = _diffattn(l, proj, diff_subln_w, diff_lambda_q1, diff_lambda_k1, diff_lambda_q2,
                           diff_lambda_k2)
        h = _ffn(l, h, ffn2_norm, ffn2_w_gate, ffn2_w_up, ffn2_w_down,
                 mix=(y_ret, y_diff, w_out),
                 final_g=final_norm.reshape(1, D_MODEL) if last else None)

    return h[0].reshape(BATCH, SEQ, D_MODEL).astype(x.dtype)
```

```python
import functools
import math

import numpy as np
import jax
import jax.numpy as jnp
from jax import lax
from jax.experimental import pallas as pl
from jax.experimental.pallas import tpu as pltpu

D_MODEL = 1024
BATCH = 8
SEQ = 2048
DEPTH = 2
N_META = 16
CHUNK = 128
RET_HEADS = 4
RET_DK = 128
RET_WIDTH = 512
RET_THETA = 10000.0
DIFF_HEADS = 4
DIFF_D = 64
DIFF_DV = 128
DIFF_WIDTH = 512
ROPE_THETA = 500000.0
ROPE_DIMS = 16
D_FF = 2816
EPS = 1e-6
D_IN = 3584

PAD = CHUNK - N_META
FRAME_CHUNKS = SEQ // CHUNK
T_PAD = CHUNK + SEQ
FRAME_ROWS = BATCH * SEQ
ROW_TILE = 512
N_ROW_TILES = FRAME_ROWS // ROW_TILE
N_WCHUNK = 8
N_TAB = 12
LANES = 128
VMEM_LIMIT = 56 * 1024 * 1024
NEG = -1e30

F32 = jnp.float32
BF16 = jnp.bfloat16


def _rms(x, g):
    return x * lax.rsqrt(jnp.mean(x * x, axis=-1, keepdims=True) + EPS) * g


def _whole(shape):
    return pl.BlockSpec(shape, lambda *_: (0,) * len(shape))


def _tile_index(i):
    return jnp.clip(i - N_WCHUNK, 0, N_ROW_TILES - 1)


def _row_tile(width):
    return pl.BlockSpec((ROW_TILE, width), lambda i: (_tile_index(i), 0))


def _weight_chunk(layer, rows, cols):
    return pl.BlockSpec((None, rows // N_WCHUNK, cols),
                        lambda i: (layer, jnp.minimum(i, N_WCHUNK - 1), 0))


def _stage_weight(i, src_ref, dst_ref):
    rows = src_ref.shape[0]
    dst_ref[pl.ds(pl.multiple_of(i * rows, 16), rows), :] = src_ref[...].astype(BF16)


def _dense_steps(i, stage, frames, meta):
    pl.when(i < N_WCHUNK)(stage)
    pl.when((i >= N_WCHUNK) & (i < N_WCHUNK + N_ROW_TILES))(frames)
    if meta is not None:
        pl.when(i == N_WCHUNK + N_ROW_TILES)(meta)


def _ffn_kernel(*refs, layer, mix, final):
    refs = list(refs)
    take = lambda n: [refs.pop(0) for _ in range(n)]
    with_meta = not final
    h_refs = take(2 if with_meta else 1)
    if mix:
        yr_refs, yd_refs = take(len(h_refs)), take(len(h_refs))
        (wo_ref,) = take(1)
    g_ref, wg_ref, wu_ref, wd_ref = take(4)
    if final:
        (fn_ref,) = take(1)
    o_refs = take(len(h_refs))
    wg_s, wu_s, wd_s = take(3)
    if mix:
        (wo_s,) = take(1)
    i = pl.program_id(0)

    def stage():
        _stage_weight(i, wg_ref, wg_s)
        _stage_weight(i, wu_ref, wu_s)
        _stage_weight(i, wd_ref, wd_s)
        if mix:
            _stage_weight(i, wo_ref, wo_s)

    def rows(which):
        n = h_refs[which].shape[0]
        parts = [slice(0, n // 2), slice(n // 2, n)] if n > CHUNK else [slice(0, n)]
        xs, xns, acts = [], [], []
        for sl in parts:
            x = h_refs[which][sl, :]
            if mix:
                x = (x + jnp.dot(yr_refs[which][sl, :], wo_s[:RET_WIDTH, :],
                                 preferred_element_type=F32)
                     + jnp.dot(yd_refs[which][sl, :], wo_s[RET_WIDTH:, :],
                               preferred_element_type=F32))
            xs.append(x)
            xns.append(_rms(x, g_ref[layer:layer + 1, :]).astype(BF16))
        for xn in xns:
            gate = jnp.dot(xn, wg_s[...], preferred_element_type=F32)
            up = jnp.dot(xn, wu_s[...], preferred_element_type=F32)
            acts.append((jax.nn.silu(gate) * up).astype(BF16))
        for sl, x, act in zip(parts, xs, acts):
            y = x + 0.5 * jnp.dot(act, wd_s[...], preferred_element_type=F32)
            if final:
                y = _rms(y, fn_ref[...])
            o_refs[which][sl, :] = y

    _dense_steps(i, stage, functools.partial(rows, 0),
                 functools.partial(rows, 1) if with_meta else None)


def _ffn(layer, h, norm_g, wg, wu, wd, mix=None, final_g=None):
    with_meta = final_g is None
    pair = lambda width: [_row_tile(width)] + ([_whole((CHUNK, width))] if with_meta else [])
    keep = lambda arrays: list(arrays) if with_meta else [arrays[0]]
    args, specs = keep(h), pair(D_MODEL)
    scratch = [pltpu.VMEM((D_MODEL, D_FF), BF16), pltpu.VMEM((D_MODEL, D_FF), BF16),
               pltpu.VMEM((D_FF, D_MODEL), BF16)]
    if mix is not None:
        y_ret, y_diff, w_out = mix
        args += keep(y_ret) + keep(y_diff) + [w_out]
        specs += pair(RET_WIDTH) + pair(DIFF_WIDTH) + [_weight_chunk(layer, D_MODEL, D_MODEL)]
        scratch.append(pltpu.VMEM((D_MODEL, D_MODEL), BF16))
    args += [norm_g, wg, wu, wd]
    specs += [_whole((DEPTH, D_MODEL)), _weight_chunk(layer, D_MODEL, D_FF),
              _weight_chunk(layer, D_MODEL, D_FF), _weight_chunk(layer, D_FF, D_MODEL)]
    if final_g is not None:
        args.append(final_g)
        specs.append(_whole((1, D_MODEL)))
    out_shape = [jax.ShapeDtypeStruct((FRAME_ROWS, D_MODEL), F32)]
    if with_meta:
        out_shape.append(jax.ShapeDtypeStruct((CHUNK, D_MODEL), F32))
    return pl.pallas_call(
        functools.partial(_ffn_kernel, layer=layer, mix=mix is not None,
                          final=final_g is not None),
        grid=(N_WCHUNK + N_ROW_TILES + int(with_meta),),
        in_specs=specs,
        out_specs=pair(D_MODEL),
        out_shape=out_shape,
        scratch_shapes=scratch,
        compiler_params=pltpu.CompilerParams(
            dimension_semantics=("arbitrary",), vmem_limit_bytes=VMEM_LIMIT),
        name="ffn",
    )(*args)


_ROTATIONS = (
    (0, 0, LANES - 1, 1),
    (RET_WIDTH, 3, LANES - 1, 1),
    (4 * RET_WIDTH, 6, 8, LANES - 8),
    (4 * RET_WIDTH + DIFF_WIDTH, 9, 8, LANES - 8),
)


def _inproj_kernel(hf_ref, hm_ref, g_ref, w_ref, tabf_ref, tabm_ref, of_ref, om_ref, w_s, *,
                   layer):
    i = pl.program_id(0)

    def stage():
        _stage_weight(i, w_ref, w_s)

    def rows(h_ref, tab_ref, o_ref):
        xn = _rms(h_ref[...], g_ref[layer:layer + 1, :]).astype(BF16)
        p = jnp.dot(xn, w_s[...], preferred_element_type=F32)

        def tab(t):
            return tab_ref[:, t * LANES:(t + 1) * LANES]

        for base, t0, roll_a, roll_b in _ROTATIONS:
            for hd in range(RET_HEADS):
                sl = slice(base + hd * LANES, base + (hd + 1) * LANES)
                xs = p[:, sl]
                o_ref[:, sl] = (xs * tab(t0) + pltpu.roll(xs, roll_a, 1) * tab(t0 + 1)
                                + pltpu.roll(xs, roll_b, 1) * tab(t0 + 2)).astype(BF16)
        for lo, hi in ((2 * RET_WIDTH, 4 * RET_WIDTH), (D_IN - DIFF_WIDTH, D_IN)):
            o_ref[:, lo:hi] = p[:, lo:hi].astype(BF16)

    _dense_steps(i, stage, functools.partial(rows, hf_ref, tabf_ref, of_ref),
                 functools.partial(rows, hm_ref, tabm_ref, om_ref))


def _inproj(layer, h, norm_g, w_in, tab_frames, tab_meta):
    tiles_per_batch = SEQ // ROW_TILE
    return pl.pallas_call(
        functools.partial(_inproj_kernel, layer=layer),
        grid=(N_WCHUNK + N_ROW_TILES + 1,),
        in_specs=[_row_tile(D_MODEL), _whole((CHUNK, D_MODEL)), _whole((DEPTH, D_MODEL)),
                  _weight_chunk(layer, D_MODEL, D_IN),
                  pl.BlockSpec((ROW_TILE, N_TAB * LANES),
                               lambda i: (_tile_index(i) % tiles_per_batch, 0)),
                  _whole((CHUNK, N_TAB * LANES))],
        out_specs=[_row_tile(D_IN), _whole((CHUNK, D_IN))],
        out_shape=[jax.ShapeDtypeStruct((FRAME_ROWS, D_IN), BF16),
                   jax.ShapeDtypeStruct((CHUNK, D_IN), BF16)],
        scratch_shapes=[pltpu.VMEM((D_MODEL, D_IN), BF16)],
        compiler_params=pltpu.CompilerParams(
            dimension_semantics=("arbitrary",), vmem_limit_bytes=VMEM_LIMIT),
        name="inproj",
    )(h[0], h[1], norm_g, w_in, tab_frames, tab_meta)


def _retention_kernel(cd_ref, qf_ref, kf_ref, vf_ref, gf_ref, qm_ref, km_ref, vm_ref, gm_ref,
                      dintra_ref, qdec_ref, kdec_ref, gnw_ref, of_ref, om_ref, state_ref, *,
                      layer):
    nt = (((1,), (1,)), ((), ()))
    tn = (((0,), (0,)), ((), ()))
    n = pl.program_id(0)

    def chunk(units, first):
        scores, incs, outs = [], [], []
        for hd, q, k, v, g, states, out in units:
            s = lax.dot_general(q(), k(), nt, preferred_element_type=F32) * dintra_ref[hd]
            scores.append(s.astype(BF16))
        for hd, q, k, v, g, states, out in units:
            kd = (k().astype(F32) * kdec_ref[hd]).astype(BF16)
            incs.append(lax.dot_general(kd, v(), tn, preferred_element_type=F32))
        for (hd, q, k, v, g, states, out), s in zip(units, scores):
            if first:
                outs.append(jnp.dot(s, v(), preferred_element_type=F32))
            else:
                qd = (q().astype(F32) * qdec_ref[hd]).astype(BF16)
                r = states[0][...].astype(BF16)
                outs.append(jnp.dot(jnp.concatenate([s, qd], axis=1),
                                    jnp.concatenate([v(), r], axis=0),
                                    preferred_element_type=F32))
        for (hd, q, k, v, g, states, out), inc in zip(units, incs):
            for st in states:
                st[...] = inc if first else st[...] * cd_ref[hd] + inc
        for (hd, q, k, v, g, states, out), o in zip(units, outs):
            mu = jnp.mean(o, axis=-1, keepdims=True)
            d = o - mu
            var = jnp.mean(d * d, axis=-1, keepdims=True)
            on = d * lax.rsqrt(var + EPS) * gnw_ref[layer:layer + 1, hd * LANES:(hd + 1) * LANES]
            out[...] = (jax.nn.silu(g().astype(F32)) * on).astype(BF16)

    def head_units(hd, b, refs, out_ref, states):
        sl = slice(hd * LANES, (hd + 1) * LANES)
        if b is None:
            load = lambda ref: (lambda: ref[:, sl])
            out = out_ref.at[:, sl]
        else:
            load = lambda ref: (lambda: ref[b, 0, :, sl])
            out = out_ref.at[b, 0, :, sl]
        q, k, v, g = (load(ref) for ref in refs)
        return (hd, q, k, v, g, states, out)

    @pl.when(n == 0)
    def _():
        chunk([head_units(hd, None, (qm_ref, km_ref, vm_ref, gm_ref), om_ref,
                          [state_ref.at[b, hd] for b in range(BATCH)])
               for hd in range(RET_HEADS)], first=True)

    @pl.when(n > 0)
    def _():
        chunk([head_units(hd, b, (qf_ref, kf_ref, vf_ref, gf_ref), of_ref, [state_ref.at[b, hd]])
               for b in range(BATCH) for hd in range(RET_HEADS)], first=False)


def _retention(layer, proj, cd, dintra, qdec, kdec, gn_w):
    proj_frames, proj_meta = proj
    frames4 = proj_frames.reshape(BATCH, FRAME_CHUNKS, CHUNK, D_IN)
    chunk = lambda n: jnp.maximum(n - 1, 0)
    fblk = lambda c: pl.BlockSpec((BATCH, 1, CHUNK, RET_WIDTH), lambda n, c=c: (0, chunk(n), 0, c))
    mblk = lambda c: pl.BlockSpec((CHUNK, RET_WIDTH), lambda n, c=c: (0, c))
    const3 = _whole((RET_HEADS, CHUNK, LANES))
    y_frames, y_meta = pl.pallas_call(
        functools.partial(_retention_kernel, layer=layer),
        grid=(1 + FRAME_CHUNKS,),
        in_specs=[pl.BlockSpec(memory_space=pltpu.SMEM),
                  fblk(0), fblk(1), fblk(2), fblk(3), mblk(0), mblk(1), mblk(2), mblk(3),
                  const3, const3, const3, _whole((DEPTH, RET_WIDTH))],
        out_specs=[fblk(0), mblk(0)],
        out_shape=[jax.ShapeDtypeStruct((BATCH, FRAME_CHUNKS, CHUNK, RET_WIDTH), BF16),
                   jax.ShapeDtypeStruct((CHUNK, RET_WIDTH), BF16)],
        scratch_shapes=[pltpu.VMEM((BATCH, RET_HEADS, RET_DK, LANES), F32)],
        compiler_params=pltpu.CompilerParams(
            dimension_semantics=("arbitrary",), vmem_limit_bytes=VMEM_LIMIT),
        name="retention",
    )(cd, frames4, frames4, frames4, frames4, proj_meta, proj_meta, proj_meta, proj_meta,
      dintra, qdec, kdec, gn_w)
    return y_frames.reshape(FRAME_ROWS, RET_WIDTH), y_meta


_QUERY_BLOCKS = ((0, CHUNK),) + tuple((r, 2 * CHUNK) for r in range(CHUNK, T_PAD, 2 * CHUNK))


def _key_tiles(nk):
    return ((0, CHUNK),) + tuple((r, 2 * CHUNK) for r in range(CHUNK, nk, 2 * CHUNK))


def _diffattn_kernel(qf_ref, kf_ref, vf_ref, qm_ref, km_ref, vm_ref, w_ref, lq1_ref, lk1_ref,
                     lq2_ref, lk2_ref, of_ref, om_ref, s_ref, p_ref, *, layer, lambda_init):
    row = slice(layer, layer + 1)
    lam = (jnp.exp(jnp.sum(lq1_ref[row, :] * lk1_ref[row, :], axis=-1, keepdims=True))
           - jnp.exp(jnp.sum(lq2_ref[row, :] * lk2_ref[row, :], axis=-1, keepdims=True))
           + lambda_init)
    nt = (((1,), (1,)), ((), ()))
    tn = (((0,), (0,)), ((), ()))

    def rows(meta_ref, frames_ref, r0, n):
        return meta_ref[...] if r0 == 0 else frames_ref[0, r0 - CHUNK:r0 - CHUNK + n, :]

    def fold(state, key, value, op):
        state[key] = value if key not in state else op(state[key], value)

    def score_tiles(blk, slot, state):
        r0, nq = _QUERY_BLOCKS[blk]
        nk = r0 + nq
        q = rows(qm_ref, qf_ref, r0, nq)
        lane = lax.broadcasted_iota(jnp.int32, (nq, LANES), 1)
        zero = jnp.zeros_like(q)
        qq = jnp.concatenate([jnp.where(lane < DIFF_D, q, zero),
                              jnp.where(lane >= DIFF_D, q, zero)], axis=0)

        def tile(t0, tk):
            s = lax.dot_general(rows(km_ref, kf_ref, t0, tk), qq, nt, preferred_element_type=F32)
            krow = lax.broadcasted_iota(jnp.int32, (tk, 2 * nq), 0)
            if t0 + tk == nk:
                qcol = lax.broadcasted_iota(jnp.int32, (tk, 2 * nq), 1) % nq
                s = jnp.where(krow <= qcol, s, NEG)
            if t0 == 0:
                s = jnp.where(krow >= PAD, s, NEG)
            s_ref[slot, t0:t0 + tk, :2 * nq] = s
            fold(state, "max", jnp.max(s, axis=0, keepdims=True), jnp.maximum)

        return [functools.partial(tile, t0, tk) for t0, tk in _key_tiles(nk)]

    def prob_tiles(blk, slot, smax, state):
        r0, nq = _QUERY_BLOCKS[blk]
        nk = r0 + nq

        def tile(t0, tk):
            p = jnp.exp2(s_ref[slot, t0:t0 + tk, :2 * nq] - smax)
            fold(state, "pos", jnp.sum(p[:, :nq], axis=0, keepdims=True), jnp.add)
            fold(state, "neg", jnp.sum(p[:, nq:], axis=0, keepdims=True), jnp.add)
            p_ref[slot, t0:t0 + tk, :2 * nq] = p.astype(BF16)

        return [functools.partial(tile, t0, tk) for t0, tk in _key_tiles(nk)]

    def finish(blk, slot, sums):
        r0, nq = _QUERY_BLOCKS[blk]
        nk = r0 + nq
        ot = lax.dot_general(vm_ref[...], p_ref[slot, :CHUNK, :2 * nq], tn,
                             preferred_element_type=F32)
        if nk > CHUNK:
            ot = ot + lax.dot_general(vf_ref[0, :nk - CHUNK, :], p_ref[slot, CHUNK:nk, :2 * nq], tn,
                                      preferred_element_type=F32)
        o = (ot[:, :nq] * (1.0 / sums["pos"]) - ot[:, nq:] * (lam / sums["neg"])).T
        o = o * lax.rsqrt(jnp.mean(o * o, axis=-1, keepdims=True) + EPS) * w_ref[row, :]
        o = o * (1.0 - lambda_init)
        if r0 == 0:
            orow = lax.broadcasted_iota(jnp.int32, (nq, LANES), 0)
            om_ref[...] = jnp.where(orow >= PAD, o, 0.0).astype(BF16)
        else:
            of_ref[0, r0 - CHUNK:nk - CHUNK, :] = o.astype(BF16)

    n_blocks = len(_QUERY_BLOCKS)
    order = list(range(0, n_blocks, 2)) + list(range(n_blocks - 1 - n_blocks % 2, 0, -2))
    score_state, pending = {}, None
    for run in score_tiles(order[0], 0, score_state):
        run()
    for r, blk in enumerate(order):
        smax, sums, score_state = score_state["max"], {}, {}
        ahead = score_tiles(order[r + 1], (r + 1) % 2, score_state) if r + 1 < n_blocks else []
        current = prob_tiles(blk, r % 2, smax, sums)
        for n in range(max(len(ahead), len(current))):
            for runs in (current, ahead):
                if n < len(runs):
                    runs[n]()
            if n == 0 and pending is not None:
                finish(*pending)
        pending = (blk, r % 2, sums)
    finish(*pending)


def _diffattn(layer, proj, subln_w, lq1, lk1, lq2, lk2):
    proj_frames, proj_meta = proj
    frames3 = proj_frames.reshape(BATCH, SEQ, D_IN)
    lambda_init = 0.8 - 0.6 * math.exp(-0.3 * layer)
    first = 4 * RET_WIDTH // LANES
    fblk = lambda c: pl.BlockSpec((1, SEQ, LANES), lambda hd, b, c=c: (b, 0, first + c + hd))
    mblk = lambda c: pl.BlockSpec((CHUNK, LANES), lambda hd, b, c=c: (0, first + c + hd))
    small = lambda n: _whole((DEPTH, n))
    y_frames, y_meta = pl.pallas_call(
        functools.partial(_diffattn_kernel, layer=layer, lambda_init=lambda_init),
        grid=(DIFF_HEADS, BATCH),
        in_specs=[fblk(0), fblk(DIFF_HEADS), fblk(2 * DIFF_HEADS),
                  mblk(0), mblk(DIFF_HEADS), mblk(2 * DIFF_HEADS),
                  small(DIFF_DV), small(DIFF_D), small(DIFF_D), small(DIFF_D), small(DIFF_D)],
        out_specs=[pl.BlockSpec((1, SEQ, LANES), lambda hd, b: (b, 0, hd)),
                   pl.BlockSpec((CHUNK, LANES), lambda hd, b: (0, hd))],
        out_shape=[jax.ShapeDtypeStruct((BATCH, SEQ, DIFF_WIDTH), BF16),
                   jax.ShapeDtypeStruct((CHUNK, DIFF_WIDTH), BF16)],
        scratch_shapes=[pltpu.VMEM((2, T_PAD, 4 * CHUNK), F32),
                        pltpu.VMEM((2, T_PAD, 4 * CHUNK), BF16)],
        compiler_params=pltpu.CompilerParams(
            dimension_semantics=("arbitrary", "arbitrary"), vmem_limit_bytes=VMEM_LIMIT),
        name="diffattn",
    )(frames3, frames3, frames3, proj_meta, proj_meta, proj_meta, subln_w, lq1, lk1, lq2, lk2)
    return y_frames.reshape(FRAME_ROWS, DIFF_WIDTH), y_meta


def _rotary_tables():
    f32 = np.float32
    pos = np.arange(T_PAD, dtype=f32) - f32(PAD)
    angle = (f32(RET_THETA) ** (-np.linspace(0.0, 1.0, RET_DK // 2, dtype=f32))).astype(f32)
    fr = pos[:, None] * angle[None, :]
    c, s = np.cos(fr), np.sin(fr)
    zero = np.zeros_like(s)
    cos_i = np.repeat(c, 2, axis=-1)
    sin_even = np.stack([-s, zero], axis=-1).reshape(T_PAD, RET_DK)
    sin_odd = np.stack([zero, s], axis=-1).reshape(T_PAD, RET_DK)
    ks = f32(RET_DK ** -0.5)
    inv = (f32(ROPE_THETA) ** (-np.arange(0, ROPE_DIMS, 2, dtype=f32) / f32(ROPE_DIMS))).astype(f32)
    fq = pos[:, None] * inv[None, :]
    emb = np.concatenate([fq, fq], axis=-1)
    ce, se = np.cos(emb), np.sin(emb)
    half, rest = ROPE_DIMS // 2, DIFF_D - ROPE_DIMS
    zeros = lambda n: np.zeros((T_PAD, n), f32)
    two = lambda a: np.concatenate([a, a], axis=-1)
    c_d = two(np.concatenate([ce, np.ones((T_PAD, rest), f32)], axis=-1))
    s_lo = two(np.concatenate([zeros(half), se[:, half:], zeros(rest)], axis=-1))
    s_hi = two(np.concatenate([-se[:, :half], zeros(half + rest)], axis=-1))
    qs = f32((DIFF_D ** -0.5) * math.log2(math.e))
    tabs = [cos_i, sin_even, sin_odd, cos_i * ks, sin_even * ks, sin_odd * ks,
            c_d * qs, s_lo * qs, s_hi * qs, c_d, s_lo, s_hi]
    return np.concatenate(tabs, axis=-1).astype(f32)


def _retention_consts():
    f32 = np.float32
    log_gamma = np.log(f32(1.0) - f32(2.0) ** (f32(-5.0) - np.arange(RET_HEADS, dtype=f32))).astype(f32)
    idx = np.arange(CHUNK, dtype=f32)
    rel = idx[:, None] - idx[None, :]
    dintra = np.where(rel >= 0, np.exp(log_gamma[:, None, None] * np.maximum(rel, f32(0.0))), f32(0.0))
    k_decay = np.exp(log_gamma[:, None] * (CHUNK - 1 - idx)[None, :])
    q_decay = np.exp(log_gamma[:, None] * (idx + f32(1.0))[None, :])
    cd = np.exp(log_gamma * f32(CHUNK))
    bc = lambda a: np.ascontiguousarray(np.broadcast_to(a[:, :, None], (RET_HEADS, CHUNK, LANES)))
    return cd.astype(f32), dintra.astype(f32), bc(q_decay).astype(f32), bc(k_decay).astype(f32)


def kernel(x, meta_tokens, ffn1_norm, ffn1_w_gate, ffn1_w_up, ffn1_w_down, mix_norm, w_in, ret_gn_w, diff_subln_w, diff_lambda_q1, diff_lambda_k1, diff_lambda_q2, diff_lambda_k2, w_out, ffn2_norm, ffn2_w_gate, ffn2_w_up, ffn2_w_down, final_norm):
    meta_chunk = jnp.concatenate([jnp.zeros((PAD, D_MODEL), F32), meta_tokens.astype(F32)], axis=0)
    h = (x.astype(F32).reshape(FRAME_ROWS, D_MODEL), meta_chunk)

    tab = _rotary_tables()
    tab_meta, tab_frames = jnp.asarray(tab[:CHUNK]), jnp.asarray(tab[CHUNK:])
    cd, dintra, qdec, kdec = (jnp.asarray(a) for a in _retention_consts())

    for l in range(DEPTH):
        last = l == DEPTH - 1
        h = _ffn(l, h, ffn1_norm, ffn1_w_gate, ffn1_w_up, ffn1_w_down)
        proj = _inproj(l, h, mix_norm, w_in, tab_frames, tab_meta)
        y_ret = _retention(l, proj, cd, dintra, qdec, kdec, ret_gn_w)
        y_diff = _diffattn(l, proj, diff_subln_w, diff_lambda_q1, diff_lambda_k1, diff_lambda_q2,
                           diff_lambda_k2)
        h = _ffn(l, h, ffn2_norm, ffn2_w_gate, ffn2_w_up, ffn2_w_down,
                 mix=(y_ret, y_diff, w_out),
                 final_g=final_norm.reshape(1, D_MODEL) if last else None)

    return h[0].reshape(BATCH, SEQ, D_MODEL).astype(x.dtype)
```

```python
import functools
import math

import numpy as np
import jax
import jax.numpy as jnp
from jax import lax
from jax.experimental import pallas as pl
from jax.experimental.pallas import tpu as pltpu

D_MODEL = 1024
BATCH = 8
SEQ = 2048
DEPTH = 2
N_META = 16
CHUNK = 128
RET_HEADS = 4
RET_DK = 128
RET_WIDTH = 512
RET_THETA = 10000.0
DIFF_HEADS = 4
DIFF_D = 64
DIFF_DV = 128
DIFF_WIDTH = 512
ROPE_THETA = 500000.0
ROPE_DIMS = 16
D_FF = 2816
EPS = 1e-6
D_IN = 3584

PAD = CHUNK - N_META
FRAME_CHUNKS = SEQ // CHUNK
T_PAD = CHUNK + SEQ
FRAME_ROWS = BATCH * SEQ
N_TAB = 12
LANES = 128
VMEM_LIMIT = 56 * 1024 * 1024
NEG = -1e30

F32 = jnp.float32
BF16 = jnp.bfloat16


def _rms(x, g):
    return x * lax.rsqrt(jnp.mean(x * x, axis=-1, keepdims=True) + EPS) * g


def _whole(shape):
    return pl.BlockSpec(shape, lambda *_: (0,) * len(shape))


class _DenseGrid:
    def __init__(self, tile, stage, meta):
        self.tile, self.stage, self.meta = tile, stage, meta
        self.n_tiles = FRAME_ROWS // tile
        self.steps = stage + self.n_tiles + int(meta)

    def tile_index(self, i):
        return jnp.clip(i - self.stage, 0, self.n_tiles - 1)

    def row_tile(self, width):
        return pl.BlockSpec((self.tile, width), lambda i: (self.tile_index(i), 0))

    def weight_chunk(self, layer, rows, cols):
        return pl.BlockSpec((None, rows // self.stage, cols),
                            lambda i: (layer, jnp.minimum(i, self.stage - 1), 0))

    def run(self, i, stage, frames, meta):
        pl.when(i < self.stage)(stage)
        pl.when((i >= self.stage) & (i < self.stage + self.n_tiles))(frames)
        if self.meta:
            pl.when(i == self.stage + self.n_tiles)(meta)


def _stage_weight(i, src_ref, dst_ref):
    rows = src_ref.shape[0]
    dst_ref[pl.ds(pl.multiple_of(i * rows, 16), rows), :] = src_ref[...].astype(BF16)


FFN_TILE = 1024
FFN_SUB_TILE = 256
FFN_STAGE_STEPS = 16
PROJ_GRID = _DenseGrid(tile=512, stage=8, meta=True)


def _ffn_kernel(*refs, grid, layer, mix, final):
    refs = list(refs)
    take = lambda n: [refs.pop(0) for _ in range(n)]
    with_meta = not final
    h_refs = take(2 if with_meta else 1)
    if mix:
        yr_refs, yd_refs = take(len(h_refs)), take(len(h_refs))
        (wo_ref,) = take(1)
    g_ref, wg_ref, wu_ref, wd_ref = take(4)
    if final:
        (fn_ref,) = take(1)
    o_refs = take(len(h_refs))
    wg_s, wu_s, wd_s = take(3)
    if mix:
        (wo_s,) = take(1)
    i = pl.program_id(0)

    def stage():
        _stage_weight(i, wg_ref, wg_s)
        _stage_weight(i, wu_ref, wu_s)
        _stage_weight(i, wd_ref, wd_s)
        if mix:
            _stage_weight(i, wo_ref, wo_s)

    def rows(which):
        n = h_refs[which].shape[0]
        parts = [slice(r, min(r + FFN_SUB_TILE, n)) for r in range(0, n, FFN_SUB_TILE)]
        xs, xns, acts = [], [], []
        for sl in parts:
            x = h_refs[which][sl, :]
            if mix:
                x = (x + jnp.dot(yr_refs[which][sl, :], wo_s[:RET_WIDTH, :],
                                 preferred_element_type=F32)
                     + jnp.dot(yd_refs[which][sl, :], wo_s[RET_WIDTH:, :],
                               preferred_element_type=F32))
            xs.append(x)
            xns.append(_rms(x, g_ref[layer:layer + 1, :]).astype(BF16))
        for xn in xns:
            gate = jnp.dot(xn, wg_s[...], preferred_element_type=F32)
            up = jnp.dot(xn, wu_s[...], preferred_element_type=F32)
            acts.append((jax.nn.silu(gate) * up).astype(BF16))
        for sl, x, act in zip(parts, xs, acts):
            y = x + 0.5 * jnp.dot(act, wd_s[...], preferred_element_type=F32)
            if final:
                y = _rms(y, fn_ref[...])
            o_refs[which][sl, :] = y

    grid.run(i, stage, functools.partial(rows, 0), functools.partial(rows, 1))


def _ffn(layer, h, norm_g, wg, wu, wd, mix=None, final_g=None):
    with_meta = final_g is None
    grid = _DenseGrid(tile=FFN_TILE, stage=FFN_STAGE_STEPS, meta=with_meta)
    _row_tile, _weight_chunk = grid.row_tile, grid.weight_chunk
    pair = lambda width: [_row_tile(width)] + ([_whole((CHUNK, width))] if with_meta else [])
    keep = lambda arrays: list(arrays) if with_meta else [arrays[0]]
    args, specs = keep(h), pair(D_MODEL)
    scratch = [pltpu.VMEM((D_MODEL, D_FF), BF16), pltpu.VMEM((D_MODEL, D_FF), BF16),
               pltpu.VMEM((D_FF, D_MODEL), BF16)]
    if mix is not None:
        y_ret, y_diff, w_out = mix
        args += keep(y_ret) + keep(y_diff) + [w_out]
        specs += pair(RET_WIDTH) + pair(DIFF_WIDTH) + [_weight_chunk(layer, D_MODEL, D_MODEL)]
        scratch.append(pltpu.VMEM((D_MODEL, D_MODEL), BF16))
    args += [norm_g, wg, wu, wd]
    specs += [_whole((DEPTH, D_MODEL)), _weight_chunk(layer, D_MODEL, D_FF),
              _weight_chunk(layer, D_MODEL, D_FF), _weight_chunk(layer, D_FF, D_MODEL)]
    if final_g is not None:
        args.append(final_g)
        specs.append(_whole((1, D_MODEL)))
    out_shape = [jax.ShapeDtypeStruct((FRAME_ROWS, D_MODEL), F32)]
    if with_meta:
        out_shape.append(jax.ShapeDtypeStruct((CHUNK, D_MODEL), F32))
    return pl.pallas_call(
        functools.partial(_ffn_kernel, grid=grid, layer=layer, mix=mix is not None,
                          final=final_g is not None),
        grid=(grid.steps,),
        in_specs=specs,
        out_specs=pair(D_MODEL),
        out_shape=out_shape,
        scratch_shapes=scratch,
        compiler_params=pltpu.CompilerParams(
            dimension_semantics=("arbitrary",), vmem_limit_bytes=VMEM_LIMIT),
        name="ffn",
    )(*args)


_ROTATIONS = (
    (0, 0, LANES - 1, 1),
    (RET_WIDTH, 3, LANES - 1, 1),
    (4 * RET_WIDTH, 6, 8, LANES - 8),
    (4 * RET_WIDTH + DIFF_WIDTH, 9, 8, LANES - 8),
)


def _inproj_kernel(hf_ref, hm_ref, g_ref, w_ref, tabf_ref, tabm_ref, of_ref, om_ref, w_s, *,
                   layer):
    i = pl.program_id(0)

    def stage():
        _stage_weight(i, w_ref, w_s)

    def rows(h_ref, tab_ref, o_ref):
        xn = _rms(h_ref[...], g_ref[layer:layer + 1, :]).astype(BF16)
        p = jnp.dot(xn, w_s[...], preferred_element_type=F32)

        def tab(t):
            return tab_ref[:, t * LANES:(t + 1) * LANES]

        for base, t0, roll_a, roll_b in _ROTATIONS:
            for hd in range(RET_HEADS):
                sl = slice(base + hd * LANES, base + (hd + 1) * LANES)
                xs = p[:, sl]
                o_ref[:, sl] = (xs * tab(t0) + pltpu.roll(xs, roll_a, 1) * tab(t0 + 1)
                                + pltpu.roll(xs, roll_b, 1) * tab(t0 + 2)).astype(BF16)
        for lo, hi in ((2 * RET_WIDTH, 4 * RET_WIDTH), (D_IN - DIFF_WIDTH, D_IN)):
            o_ref[:, lo:hi] = p[:, lo:hi].astype(BF16)

    PROJ_GRID.run(i, stage, functools.partial(rows, hf_ref, tabf_ref, of_ref),
                  functools.partial(rows, hm_ref, tabm_ref, om_ref))


def _inproj(layer, h, norm_g, w_in, tab_frames, tab_meta):
    grid = PROJ_GRID
    tiles_per_batch = SEQ // grid.tile
    return pl.pallas_call(
        functools.partial(_inproj_kernel, layer=layer),
        grid=(grid.steps,),
        in_specs=[grid.row_tile(D_MODEL), _whole((CHUNK, D_MODEL)), _whole((DEPTH, D_MODEL)),
                  grid.weight_chunk(layer, D_MODEL, D_IN),
                  pl.BlockSpec((grid.tile, N_TAB * LANES),
                               lambda i: (grid.tile_index(i) % tiles_per_batch, 0)),
                  _whole((CHUNK, N_TAB * LANES))],
        out_specs=[grid.row_tile(D_IN), _whole((CHUNK, D_IN))],
        out_shape=[jax.ShapeDtypeStruct((FRAME_ROWS, D_IN), BF16),
                   jax.ShapeDtypeStruct((CHUNK, D_IN), BF16)],
        scratch_shapes=[pltpu.VMEM((D_MODEL, D_IN), BF16)],
        compiler_params=pltpu.CompilerParams(
            dimension_semantics=("arbitrary",), vmem_limit_bytes=VMEM_LIMIT),
        name="inproj",
    )(h[0], h[1], norm_g, w_in, tab_frames, tab_meta)


def _retention_kernel(cd_ref, qf_ref, kf_ref, vf_ref, gf_ref, qm_ref, km_ref, vm_ref, gm_ref,
                      dintra_ref, qdec_ref, kdec_ref, gnw_ref, of_ref, om_ref, state_ref, *,
                      layer):
    nt = (((1,), (1,)), ((), ()))
    tn = (((0,), (0,)), ((), ()))
    n = pl.program_id(0)

    def chunk(units, first):
        scores, incs, outs = [], [], []
        for hd, q, k, v, g, states, out in units:
            s = lax.dot_general(q(), k(), nt, preferred_element_type=F32) * dintra_ref[hd]
            scores.append(s.astype(BF16))
        for hd, q, k, v, g, states, out in units:
            kd = (k().astype(F32) * kdec_ref[hd]).astype(BF16)
            incs.append(lax.dot_general(kd, v(), tn, preferred_element_type=F32))
        for (hd, q, k, v, g, states, out), s in zip(units, scores):
            if first:
                outs.append(jnp.dot(s, v(), preferred_element_type=F32))
            else:
                qd = (q().astype(F32) * qdec_ref[hd]).astype(BF16)
                r = states[0][...].astype(BF16)
                outs.append(jnp.dot(jnp.concatenate([s, qd], axis=1),
                                    jnp.concatenate([v(), r], axis=0),
                                    preferred_element_type=F32))
        for (hd, q, k, v, g, states, out), inc in zip(units, incs):
            for st in states:
                st[...] = inc if first else st[...] * cd_ref[hd] + inc
        for (hd, q, k, v, g, states, out), o in zip(units, outs):
            mu = jnp.mean(o, axis=-1, keepdims=True)
            d = o - mu
            var = jnp.mean(d * d, axis=-1, keepdims=True)
            on = d * lax.rsqrt(var + EPS) * gnw_ref[layer:layer + 1, hd * LANES:(hd + 1) * LANES]
            out[...] = (jax.nn.silu(g().astype(F32)) * on).astype(BF16)

    def head_units(hd, b, refs, out_ref, states):
        sl = slice(hd * LANES, (hd + 1) * LANES)
        if b is None:
            load = lambda ref: (lambda: ref[:, sl])
            out = out_ref.at[:, sl]
        else:
            load = lambda ref: (lambda: ref[b, 0, :, sl])
            out = out_ref.at[b, 0, :, sl]
        q, k, v, g = (load(ref) for ref in refs)
        return (hd, q, k, v, g, states, out)

    @pl.when(n == 0)
    def _():
        chunk([head_units(hd, None, (qm_ref, km_ref, vm_ref, gm_ref), om_ref,
                          [state_ref.at[b, hd] for b in range(BATCH)])
               for hd in range(RET_HEADS)], first=True)

    @pl.when(n > 0)
    def _():
        chunk([head_units(hd, b, (qf_ref, kf_ref, vf_ref, gf_ref), of_ref, [state_ref.at[b, hd]])
               for b in range(BATCH) for hd in range(RET_HEADS)], first=False)


def _retention(layer, proj, cd, dintra, qdec, kdec, gn_w):
    proj_frames, proj_meta = proj
    frames4 = proj_frames.reshape(BATCH, FRAME_CHUNKS, CHUNK, D_IN)
    chunk = lambda n: jnp.maximum(n - 1, 0)
    fblk = lambda c: pl.BlockSpec((BATCH, 1, CHUNK, RET_WIDTH), lambda n, c=c: (0, chunk(n), 0, c))
    mblk = lambda c: pl.BlockSpec((CHUNK, RET_WIDTH), lambda n, c=c: (0, c))
    const3 = _whole((RET_HEADS, CHUNK, LANES))
    y_frames, y_meta = pl.pallas_call(
        functools.partial(_retention_kernel, layer=layer),
        grid=(1 + FRAME_CHUNKS,),
        in_specs=[pl.BlockSpec(memory_space=pltpu.SMEM),
                  fblk(0), fblk(1), fblk(2), fblk(3), mblk(0), mblk(1), mblk(2), mblk(3),
                  const3, const3, const3, _whole((DEPTH, RET_WIDTH))],
        out_specs=[fblk(0), mblk(0)],
        out_shape=[jax.ShapeDtypeStruct((BATCH, FRAME_CHUNKS, CHUNK, RET_WIDTH), BF16),
                   jax.ShapeDtypeStruct((CHUNK, RET_WIDTH), BF16)],
        scratch_shapes=[pltpu.VMEM((BATCH, RET_HEADS, RET_DK, LANES), F32)],
        compiler_params=pltpu.CompilerParams(
            dimension_semantics=("arbitrary",), vmem_limit_bytes=VMEM_LIMIT),
        name="retention",
    )(cd, frames4, frames4, frames4, frames4, proj_meta, proj_meta, proj_meta, proj_meta,
      dintra, qdec, kdec, gn_w)
    return y_frames.reshape(FRAME_ROWS, RET_WIDTH), y_meta


_QUERY_BLOCKS = ((0, CHUNK),) + tuple((r, 2 * CHUNK) for r in range(CHUNK, T_PAD, 2 * CHUNK))


def _key_tiles(nk):
    return ((PAD, N_META),) + tuple((r, 2 * CHUNK) for r in range(CHUNK, nk, 2 * CHUNK))


def _diffattn_kernel(qf_ref, kf_ref, vf_ref, qm_ref, km_ref, vm_ref, w_ref, lq1_ref, lk1_ref,
                     lq2_ref, lk2_ref, of_ref, om_ref, s_ref, p_ref, *, layer, lambda_init):
    row = slice(layer, layer + 1)
    lam = (jnp.exp(jnp.sum(lq1_ref[row, :] * lk1_ref[row, :], axis=-1, keepdims=True))
           - jnp.exp(jnp.sum(lq2_ref[row, :] * lk2_ref[row, :], axis=-1, keepdims=True))
           + lambda_init)
    nt = (((1,), (1,)), ((), ()))
    tn = (((0,), (0,)), ((), ()))

    def rows(meta_ref, frames_ref, r0, n):
        if r0 < CHUNK:
            return meta_ref[r0:r0 + n, :]
        return frames_ref[0, r0 - CHUNK:r0 - CHUNK + n, :]

    def fold(state, key, value, op):
        state[key] = value if key not in state else op(state[key], value)

    def score_tiles(blk, slot, state):
        r0, nq = _QUERY_BLOCKS[blk]
        nk = r0 + nq
        q = rows(qm_ref, qf_ref, r0, nq)
        lane = lax.broadcasted_iota(jnp.int32, (nq, LANES), 1)
        zero = jnp.zeros_like(q)
        qq = jnp.concatenate([jnp.where(lane < DIFF_D, q, zero),
                              jnp.where(lane >= DIFF_D, q, zero)], axis=0)

        def tile(t0, tk):
            s = lax.dot_general(rows(km_ref, kf_ref, t0, tk), qq, nt, preferred_element_type=F32)
            krow = lax.broadcasted_iota(jnp.int32, (tk, 2 * nq), 0)
            if t0 + tk > r0:
                qcol = lax.broadcasted_iota(jnp.int32, (tk, 2 * nq), 1) % nq
                s = jnp.where(t0 + krow <= r0 + qcol, s, NEG)
            s_ref[slot, t0:t0 + tk, :2 * nq] = s
            fold(state, "max", jnp.max(s, axis=0, keepdims=True), jnp.maximum)

        return [functools.partial(tile, t0, tk) for t0, tk in _key_tiles(nk)]

    def prob_tiles(blk, slot, smax, state):
        r0, nq = _QUERY_BLOCKS[blk]
        nk = r0 + nq

        def tile(t0, tk):
            p = jnp.exp2(s_ref[slot, t0:t0 + tk, :2 * nq] - smax)
            fold(state, "pos", jnp.sum(p[:, :nq], axis=0, keepdims=True), jnp.add)
            fold(state, "neg", jnp.sum(p[:, nq:], axis=0, keepdims=True), jnp.add)
            p_ref[slot, t0:t0 + tk, :2 * nq] = p.astype(BF16)

        return [functools.partial(tile, t0, tk) for t0, tk in _key_tiles(nk)]

    def finish(blk, slot, sums):
        r0, nq = _QUERY_BLOCKS[blk]
        nk = r0 + nq
        ot = lax.dot_general(vm_ref[PAD:, :], p_ref[slot, PAD:CHUNK, :2 * nq], tn,
                             preferred_element_type=F32)
        if nk > CHUNK:
            ot = ot + lax.dot_general(vf_ref[0, :nk - CHUNK, :], p_ref[slot, CHUNK:nk, :2 * nq], tn,
                                      preferred_element_type=F32)
        o = (ot[:, :nq] * (1.0 / sums["pos"]) - ot[:, nq:] * (lam / sums["neg"])).T
        o = o * lax.rsqrt(jnp.mean(o * o, axis=-1, keepdims=True) + EPS) * w_ref[row, :]
        o = o * (1.0 - lambda_init)
        if r0 == 0:
            orow = lax.broadcasted_iota(jnp.int32, (nq, LANES), 0)
            om_ref[...] = jnp.where(orow >= PAD, o, 0.0).astype(BF16)
        else:
            of_ref[0, r0 - CHUNK:nk - CHUNK, :] = o.astype(BF16)

    n_blocks = len(_QUERY_BLOCKS)
    order = list(range(0, n_blocks, 2)) + list(range(n_blocks - 1 - n_blocks % 2, 0, -2))
    score_state, pending = {}, None
    for run in score_tiles(order[0], 0, score_state):
        run()
    for r, blk in enumerate(order):
        smax, sums, score_state = score_state["max"], {}, {}
        ahead = score_tiles(order[r + 1], (r + 1) % 2, score_state) if r + 1 < n_blocks else []
        current = prob_tiles(blk, r % 2, smax, sums)
        for n in range(max(len(ahead), len(current))):
            for runs in (current, ahead):
                if n < len(runs):
                    runs[n]()
            if n == 0 and pending is not None:
                finish(*pending)
        pending = (blk, r % 2, sums)
    finish(*pending)


def _diffattn(layer, proj, subln_w, lq1, lk1, lq2, lk2):
    proj_frames, proj_meta = proj
    frames3 = proj_frames.reshape(BATCH, SEQ, D_IN)
    lambda_init = 0.8 - 0.6 * math.exp(-0.3 * layer)
    first = 4 * RET_WIDTH // LANES
    fblk = lambda c: pl.BlockSpec((1, SEQ, LANES), lambda hd, b, c=c: (b, 0, first + c + hd))
    mblk = lambda c: pl.BlockSpec((CHUNK, LANES), lambda hd, b, c=c: (0, first + c + hd))
    small = lambda n: _whole((DEPTH, n))
    y_frames, y_meta = pl.pallas_call(
        functools.partial(_diffattn_kernel, layer=layer, lambda_init=lambda_init),
        grid=(DIFF_HEADS, BATCH),
        in_specs=[fblk(0), fblk(DIFF_HEADS), fblk(2 * DIFF_HEADS),
                  mblk(0), mblk(DIFF_HEADS), mblk(2 * DIFF_HEADS),
                  small(DIFF_DV), small(DIFF_D), small(DIFF_D), small(DIFF_D), small(DIFF_D)],
        out_specs=[pl.BlockSpec((1, SEQ, LANES), lambda hd, b: (b, 0, hd)),
                   pl.BlockSpec((CHUNK, LANES), lambda hd, b: (0, hd))],
        out_shape=[jax.ShapeDtypeStruct((BATCH, SEQ, DIFF_WIDTH), BF16),
                   jax.ShapeDtypeStruct((CHUNK, DIFF_WIDTH), BF16)],
        scratch_shapes=[pltpu.VMEM((2, T_PAD, 4 * CHUNK), F32),
                        pltpu.VMEM((2, T_PAD, 4 * CHUNK), BF16)],
        compiler_params=pltpu.CompilerParams(
            dimension_semantics=("arbitrary", "arbitrary"), vmem_limit_bytes=VMEM_LIMIT),
        name="diffattn",
    )(frames3, frames3, frames3, proj_meta, proj_meta, proj_meta, subln_w, lq1, lk1, lq2, lk2)
    return y_frames.reshape(FRAME_ROWS, DIFF_WIDTH), y_meta


def _rotary_tables():
    f32 = np.float32
    pos = np.arange(T_PAD, dtype=f32) - f32(PAD)
    angle = (f32(RET_THETA) ** (-np.linspace(0.0, 1.0, RET_DK // 2, dtype=f32))).astype(f32)
    fr = pos[:, None] * angle[None, :]
    c, s = np.cos(fr), np.sin(fr)
    zero = np.zeros_like(s)
    cos_i = np.repeat(c, 2, axis=-1)
    sin_even = np.stack([-s, zero], axis=-1).reshape(T_PAD, RET_DK)
    sin_odd = np.stack([zero, s], axis=-1).reshape(T_PAD, RET_DK)
    ks = f32(RET_DK ** -0.5)
    inv = (f32(ROPE_THETA) ** (-np.arange(0, ROPE_DIMS, 2, dtype=f32) / f32(ROPE_DIMS))).astype(f32)
    fq = pos[:, None] * inv[None, :]
    emb = np.concatenate([fq, fq], axis=-1)
    ce, se = np.cos(emb), np.sin(emb)
    half, rest = ROPE_DIMS // 2, DIFF_D - ROPE_DIMS
    zeros = lambda n: np.zeros((T_PAD, n), f32)
    two = lambda a: np.concatenate([a, a], axis=-1)
    c_d = two(np.concatenate([ce, np.ones((T_PAD, rest), f32)], axis=-1))
    s_lo = two(np.concatenate([zeros(half), se[:, half:], zeros(rest)], axis=-1))
    s_hi = two(np.concatenate([-se[:, :half], zeros(half + rest)], axis=-1))
    qs = f32((DIFF_D ** -0.5) * math.log2(math.e))
    tabs = [cos_i, sin_even, sin_odd, cos_i * ks, sin_even * ks, sin_odd * ks,
            c_d * qs, s_lo * qs, s_hi * qs, c_d, s_lo, s_hi]
    return np.concatenate(tabs, axis=-1).astype(f32)


def _retention_consts():
    f32 = np.float32
    log_gamma = np.log(f32(1.0) - f32(2.0) ** (f32(-5.0) - np.arange(RET_HEADS, dtype=f32))).astype(f32)
    idx = np.arange(CHUNK, dtype=f32)
    rel = idx[:, None] - idx[None, :]
    dintra = np.where(rel >= 0, np.exp(log_gamma[:, None, None] * np.maximum(rel, f32(0.0))), f32(0.0))
    k_decay = np.exp(log_gamma[:, None] * (CHUNK - 1 - idx)[None, :])
    q_decay = np.exp(log_gamma[:, None] * (idx + f32(1.0))[None, :])
    cd = np.exp(log_gamma * f32(CHUNK))
    bc = lambda a: np.ascontiguousarray(np.broadcast_to(a[:, :, None], (RET_HEADS, CHUNK, LANES)))
    return cd.astype(f32), dintra.astype(f32), bc(q_decay).astype(f32), bc(k_decay).astype(f32)


def kernel(x, meta_tokens, ffn1_norm, ffn1_w_gate, ffn1_w_up, ffn1_w_down, mix_norm, w_in, ret_gn_w, diff_subln_w, diff_lambda_q1, diff_lambda_k1, diff_lambda_q2, diff_lambda_k2, w_out, ffn2_norm, ffn2_w_gate, ffn2_w_up, ffn2_w_down, final_norm):
    meta_chunk = jnp.concatenate([jnp.zeros((PAD, D_MODEL), F32), meta_tokens.astype(F32)], axis=0)
    h = (x.astype(F32).reshape(FRAME_ROWS, D_MODEL), meta_chunk)

    tab = _rotary_tables()
    tab_meta, tab_frames = jnp.asarray(tab[:CHUNK]), jnp.asarray(tab[CHUNK:])
    cd, dintra, qdec, kdec = (jnp.asarray(a) for a in _retention_consts())

    for l in range(DEPTH):
        last = l == DEPTH - 1
        h = _ffn(l, h, ffn1_norm, ffn1_w_gate, ffn1_w_up, ffn1_w_down)
        proj = _inproj(l, h, mix_norm, w_in, tab_frames, tab_meta)
        y_ret = _retention(l, proj, cd, dintra, qdec, kdec, ret_gn_w)
        y_diff = _diffattn(l, proj, diff_subln_w, diff_lambda_q1, diff_lambda_k1, diff_lambda_q2,
                           diff_lambda_k2)
        h = _ffn(l, h, ffn2_norm, ffn2_w_gate, ffn2_w_up, ffn2_w_down,
                 mix=(y_ret, y_diff, w_out),
                 final_g=final_norm.reshape(1, D_MODEL) if last else None)

    return h[0].reshape(BATCH, SEQ, D_MODEL).astype(x.dtype)
```

```python
import functools
import math

import numpy as np
import jax
import jax.numpy as jnp
from jax import lax
from jax.experimental import pallas as pl
from jax.experimental.pallas import tpu as pltpu

D_MODEL = 1024
BATCH = 8
SEQ = 2048
DEPTH = 2
N_META = 16
CHUNK = 128
RET_HEADS = 4
RET_DK = 128
RET_WIDTH = 512
RET_THETA = 10000.0
DIFF_HEADS = 4
DIFF_D = 64
DIFF_DV = 128
DIFF_WIDTH = 512
ROPE_THETA = 500000.0
ROPE_DIMS = 16
D_FF = 2816
EPS = 1e-6
D_IN = 3584

PAD = CHUNK - N_META
FRAME_CHUNKS = SEQ // CHUNK
T_PAD = CHUNK + SEQ
FRAME_ROWS = BATCH * SEQ
N_TAB = 12
LANES = 128
VMEM_LIMIT = 56 * 1024 * 1024
NEG = -1e30

F32 = jnp.float32
BF16 = jnp.bfloat16


def _rms(x, g):
    return x * lax.rsqrt(jnp.mean(x * x, axis=-1, keepdims=True) + EPS) * g


def _whole(shape):
    return pl.BlockSpec(shape, lambda *_: (0,) * len(shape))


class _DenseGrid:
    def __init__(self, tile, stage, meta):
        self.tile, self.stage, self.meta = tile, stage, meta
        self.n_tiles = FRAME_ROWS // tile
        self.steps = stage + self.n_tiles + int(meta)

    def tile_index(self, i):
        return jnp.clip(i - self.stage, 0, self.n_tiles - 1)

    def row_tile(self, width):
        return pl.BlockSpec((self.tile, width), lambda i: (self.tile_index(i), 0))

    def weight_chunk(self, layer, rows, cols):
        return pl.BlockSpec((None, rows // self.stage, cols),
                            lambda i: (layer, jnp.minimum(i, self.stage - 1), 0))

    def run(self, i, stage, frames, meta):
        pl.when(i < self.stage)(stage)
        pl.when((i >= self.stage) & (i < self.stage + self.n_tiles))(frames)
        if self.meta:
            pl.when(i == self.stage + self.n_tiles)(meta)


def _stage_weight(i, src_ref, dst_ref):
    rows = src_ref.shape[0]
    dst_ref[pl.ds(pl.multiple_of(i * rows, 16), rows), :] = src_ref[...].astype(BF16)


FFN_TILE = 512
FFN_SUB_TILE = 256
STAGE_STEPS = 8
PROJ_TILE = 512


def _ffn_kernel(*refs, grid, layer, mix, final):
    refs = list(refs)
    take = lambda n: [refs.pop(0) for _ in range(n)]
    with_meta = not final
    h_refs = take(2 if with_meta else 1)
    if mix:
        yr_refs, yd_refs = take(len(h_refs)), take(len(h_refs))
        (wo_ref,) = take(1)
    g_ref, wg_ref, wu_ref, wd_ref = take(4)
    if final:
        (fn_ref,) = take(1)
    o_refs = take(len(h_refs))
    wg_s, wu_s, wd_s = take(3)
    if mix:
        (wo_s,) = take(1)
    i = pl.program_id(0)

    def stage():
        _stage_weight(i, wg_ref, wg_s)
        _stage_weight(i, wu_ref, wu_s)
        _stage_weight(i, wd_ref, wd_s)
        if mix:
            _stage_weight(i, wo_ref, wo_s)

    def rows(which):
        n = h_refs[which].shape[0]
        parts = [slice(r, min(r + FFN_SUB_TILE, n)) for r in range(0, n, FFN_SUB_TILE)]
        xs, xns, acts = [], [], []
        for sl in parts:
            x = h_refs[which][sl, :]
            if mix:
                x = (x + jnp.dot(yr_refs[which][sl, :], wo_s[:RET_WIDTH, :],
                                 preferred_element_type=F32)
                     + jnp.dot(yd_refs[which][sl, :], wo_s[RET_WIDTH:, :],
                               preferred_element_type=F32))
            xs.append(x)
            xns.append(_rms(x, g_ref[layer:layer + 1, :]).astype(BF16))
        for xn in xns:
            gate = jnp.dot(xn, wg_s[...], preferred_element_type=F32)
            up = jnp.dot(xn, wu_s[...], preferred_element_type=F32)
            acts.append((jax.nn.silu(gate) * up).astype(BF16))
        for sl, x, act in zip(parts, xs, acts):
            y = x + 0.5 * jnp.dot(act, wd_s[...], preferred_element_type=F32)
            if final:
                y = _rms(y, fn_ref[...])
            o_refs[which][sl, :] = y

    grid.run(i, stage, functools.partial(rows, 0), functools.partial(rows, 1))


def _ffn(layer, h, norm_g, wg, wu, wd, mix=None, final_g=None):
    with_meta = final_g is None
    grid = _DenseGrid(tile=FFN_TILE, stage=STAGE_STEPS, meta=with_meta)
    _row_tile, _weight_chunk = grid.row_tile, grid.weight_chunk
    pair = lambda width: [_row_tile(width)] + ([_whole((CHUNK, width))] if with_meta else [])
    keep = lambda arrays: list(arrays) if with_meta else [arrays[0]]
    args, specs = keep(h), pair(D_MODEL)
    scratch = [pltpu.VMEM((D_MODEL, D_FF), BF16), pltpu.VMEM((D_MODEL, D_FF), BF16),
               pltpu.VMEM((D_FF, D_MODEL), BF16)]
    if mix is not None:
        y_ret, y_diff, w_out = mix
        args += keep(y_ret) + keep(y_diff) + [w_out]
        specs += pair(RET_WIDTH) + pair(DIFF_WIDTH) + [_weight_chunk(layer, D_MODEL, D_MODEL)]
        scratch.append(pltpu.VMEM((D_MODEL, D_MODEL), BF16))
    args += [norm_g, wg, wu, wd]
    specs += [_whole((DEPTH, D_MODEL)), _weight_chunk(layer, D_MODEL, D_FF),
              _weight_chunk(layer, D_MODEL, D_FF), _weight_chunk(layer, D_FF, D_MODEL)]
    if final_g is not None:
        args.append(final_g)
        specs.append(_whole((1, D_MODEL)))
    out_shape = [jax.ShapeDtypeStruct((FRAME_ROWS, D_MODEL), F32)]
    if with_meta:
        out_shape.append(jax.ShapeDtypeStruct((CHUNK, D_MODEL), F32))
    return pl.pallas_call(
        functools.partial(_ffn_kernel, grid=grid, layer=layer, mix=mix is not None,
                          final=final_g is not None),
        grid=(grid.steps,),
        in_specs=specs,
        out_specs=pair(D_MODEL),
        out_shape=out_shape,
        scratch_shapes=scratch,
        compiler_params=pltpu.CompilerParams(
            dimension_semantics=("arbitrary",), vmem_limit_bytes=VMEM_LIMIT),
        name="ffn",
    )(*args)


_ROTATIONS = (
    (0, 0, LANES - 1, 1),
    (RET_WIDTH, 3, LANES - 1, 1),
    (4 * RET_WIDTH, 6, 8, LANES - 8),
    (4 * RET_WIDTH + DIFF_WIDTH, 9, 8, LANES - 8),
)


RET_COLS = 4 * RET_WIDTH
DIFF_COLS = D_IN - RET_COLS
N_PROJ_TILES = FRAME_ROWS // PROJ_TILE
TILES_PER_BATCH = SEQ // PROJ_TILE


def _inproj_retention_kernel(cd_ref, hf_ref, hm_ref, g_ref, w_ref, tabf_ref, tabm_ref,
                             dintra_ref, qdec_ref, kdec_ref, gnw_ref,
                             pf_ref, pm_ref, yf_ref, ym_ref,
                             w_s, ring_ref, state_ref, meta_state_ref, *, layer):
    nt = (((1,), (1,)), ((), ()))
    tn = (((0,), (0,)), ((), ()))
    i = pl.program_id(0)
    t = i - (STAGE_STEPS + 1)

    def project_steps(h_ref, tab_ref, ret_ref, diff_ref):
        rotation = {base: rest for base, *rest in _ROTATIONS}
        cell = {}

        def tab(k):
            return tab_ref[:, k * LANES:(k + 1) * LANES]

        def group(base):
            if not cell:
                cell["xn"] = _rms(h_ref[...], g_ref[layer:layer + 1, :]).astype(BF16)
            p = jnp.dot(cell["xn"], w_s[:, base:base + RET_WIDTH], preferred_element_type=F32)
            dst, lo = (ret_ref, base) if base < RET_COLS else (diff_ref, base - RET_COLS)
            if base in rotation:
                t0, roll_a, roll_b = rotation[base]
                for hd in range(RET_HEADS):
                    xs = p[:, hd * LANES:(hd + 1) * LANES]
                    rot = (xs * tab(t0) + pltpu.roll(xs, roll_a, 1) * tab(t0 + 1)
                           + pltpu.roll(xs, roll_b, 1) * tab(t0 + 2))
                    dst[:, lo + hd * LANES:lo + (hd + 1) * LANES] = rot.astype(BF16)
            else:
                dst[:, lo:lo + RET_WIDTH] = p.astype(BF16)

        return [functools.partial(group, base) for base in range(0, D_IN, RET_WIDTH)]

    def retention_steps(src_ref, y_ref, state_in, state_out_refs):
        n_chunks = src_ref.shape[0] // CHUNK
        units = [(c, hd) for c in range(n_chunks) for hd in range(RET_HEADS)]
        scores, incs, outs, cell = {}, {}, {}, {}

        def col(c, hd, which):
            return src_ref[c * CHUNK:(c + 1) * CHUNK,
                           which * RET_WIDTH + hd * LANES:which * RET_WIDTH + (hd + 1) * LANES]

        def score(c, hd):
            s = lax.dot_general(col(c, hd, 0), col(c, hd, 1), nt, preferred_element_type=F32)
            scores[c, hd] = (s * dintra_ref[hd]).astype(BF16)

        def increment(c, hd):
            kd = (col(c, hd, 1).astype(F32) * kdec_ref[hd]).astype(BF16)
            incs[c, hd] = lax.dot_general(kd, col(c, hd, 2), tn, preferred_element_type=F32)

        def output(c, hd):
            if not cell:
                cell["state"] = state_in()
            state = cell["state"]
            if state is None:
                outs[c, hd] = jnp.dot(scores[c, hd], col(c, hd, 2), preferred_element_type=F32)
            else:
                qd = (col(c, hd, 0).astype(F32) * qdec_ref[hd]).astype(BF16)
                outs[c, hd] = jnp.dot(jnp.concatenate([scores[c, hd], qd], axis=1),
                                      jnp.concatenate([col(c, hd, 2), state[hd].astype(BF16)],
                                                      axis=0),
                                      preferred_element_type=F32)
            if hd == RET_HEADS - 1:
                if state is None:
                    cell["state"] = [incs[c, h] for h in range(RET_HEADS)]
                else:
                    cell["state"] = [state[h] * cd_ref[h] + incs[c, h] for h in range(RET_HEADS)]

        def store_state():
            for hd in range(RET_HEADS):
                for ref in state_out_refs:
                    ref[hd] = cell["state"][hd]

        def norm_gate(c, hd):
            o = outs[c, hd]
            mu = jnp.mean(o, axis=-1, keepdims=True)
            d = o - mu
            var = jnp.mean(d * d, axis=-1, keepdims=True)
            on = d * lax.rsqrt(var + EPS) * gnw_ref[layer:layer + 1, hd * LANES:(hd + 1) * LANES]
            y_ref[c * CHUNK:(c + 1) * CHUNK, hd * LANES:(hd + 1) * LANES] = (
                jax.nn.silu(col(c, hd, 3).astype(F32)) * on).astype(BF16)

        return ([functools.partial(stage, c, hd) for stage in (score, increment, output)
                 for c, hd in units] + [store_state]
                + [functools.partial(norm_gate, c, hd) for c, hd in units])

    def previous_tile_retention_steps():
        def state_in():
            first = (t + TILES_PER_BATCH - 1) % TILES_PER_BATCH == 0
            return [jnp.where(first, meta_state_ref[hd], state_ref[hd])
                    for hd in range(RET_HEADS)]

        return retention_steps(ring_ref.at[(t + 1) % 2], yf_ref, state_in, [state_ref])

    def emit(major, minor=()):
        per = -(-len(minor) // len(major))
        for k, step in enumerate(major):
            step()
            for extra in minor[k * per:(k + 1) * per]:
                extra()

    @pl.when(i < STAGE_STEPS)
    def _():
        _stage_weight(i, w_ref, w_s)

    @pl.when(i == STAGE_STEPS)
    def _():
        ring_ref[...] = jnp.zeros_like(ring_ref)
        state_ref[...] = jnp.zeros_like(state_ref)
        meta_ring = ring_ref.at[0, :CHUNK]
        emit(project_steps(hm_ref, tabm_ref, meta_ring, pm_ref))
        emit(retention_steps(meta_ring, ym_ref, lambda: None, [meta_state_ref]))

    @pl.when((t >= 0) & (t < N_PROJ_TILES))
    def _():
        steps = previous_tile_retention_steps()
        n_norm = (PROJ_TILE // CHUNK) * RET_HEADS
        emit(steps[:-n_norm])
        emit(project_steps(hf_ref, tabf_ref, ring_ref.at[t % 2], pf_ref), steps[-n_norm:])

    @pl.when(t == N_PROJ_TILES)
    def _():
        emit(previous_tile_retention_steps())


def _inproj_retention(layer, h, norm_g, w_in, tab_frames, tab_meta, cd, dintra, qdec, kdec, gn_w):
    steps = STAGE_STEPS + 1 + N_PROJ_TILES + 1
    tile = lambda i: jnp.clip(i - (STAGE_STEPS + 1), 0, N_PROJ_TILES - 1)
    lag = lambda i: jnp.clip(i - (STAGE_STEPS + 2), 0, N_PROJ_TILES - 1)
    const3 = _whole((RET_HEADS, CHUNK, LANES))
    outs = pl.pallas_call(
        functools.partial(_inproj_retention_kernel, layer=layer),
        grid=(steps,),
        in_specs=[pl.BlockSpec(memory_space=pltpu.SMEM),
                  pl.BlockSpec((PROJ_TILE, D_MODEL), lambda i: (tile(i), 0)),
                  _whole((CHUNK, D_MODEL)), _whole((DEPTH, D_MODEL)),
                  pl.BlockSpec((None, D_MODEL // STAGE_STEPS, D_IN),
                               lambda i: (layer, jnp.minimum(i, STAGE_STEPS - 1), 0)),
                  pl.BlockSpec((PROJ_TILE, N_TAB * LANES),
                               lambda i: (tile(i) % TILES_PER_BATCH, 0)),
                  _whole((CHUNK, N_TAB * LANES)),
                  const3, const3, const3, _whole((DEPTH, RET_WIDTH))],
        out_specs=[pl.BlockSpec((PROJ_TILE, DIFF_COLS), lambda i: (tile(i), 0)),
                   _whole((CHUNK, DIFF_COLS)),
                   pl.BlockSpec((PROJ_TILE, RET_WIDTH), lambda i: (lag(i), 0)),
                   _whole((CHUNK, RET_WIDTH))],
        out_shape=[jax.ShapeDtypeStruct((FRAME_ROWS, DIFF_COLS), BF16),
                   jax.ShapeDtypeStruct((CHUNK, DIFF_COLS), BF16),
                   jax.ShapeDtypeStruct((FRAME_ROWS, RET_WIDTH), BF16),
                   jax.ShapeDtypeStruct((CHUNK, RET_WIDTH), BF16)],
        scratch_shapes=[pltpu.VMEM((D_MODEL, D_IN), BF16),
                        pltpu.VMEM((2, PROJ_TILE, RET_COLS), BF16),
                        pltpu.VMEM((RET_HEADS, RET_DK, LANES), F32),
                        pltpu.VMEM((RET_HEADS, RET_DK, LANES), F32)],
        compiler_params=pltpu.CompilerParams(
            dimension_semantics=("arbitrary",), vmem_limit_bytes=VMEM_LIMIT),
        name="inproj_retention",
    )(cd, h[0], h[1], norm_g, w_in, tab_frames, tab_meta, dintra, qdec, kdec, gn_w)
    return (outs[0], outs[1]), (outs[2], outs[3])


_QUERY_BLOCKS = ((0, CHUNK),) + tuple((r, 2 * CHUNK) for r in range(CHUNK, T_PAD, 2 * CHUNK))


def _key_tiles(nk):
    return ((PAD, N_META),) + tuple((r, 2 * CHUNK) for r in range(CHUNK, nk, 2 * CHUNK))


def _diffattn_kernel(qf_ref, kf_ref, vf_ref, qm_ref, km_ref, vm_ref, w_ref, lq1_ref, lk1_ref,
                     lq2_ref, lk2_ref, of_ref, om_ref, s_ref, p_ref, *, layer, lambda_init):
    row = slice(layer, layer + 1)
    lam = (jnp.exp(jnp.sum(lq1_ref[row, :] * lk1_ref[row, :], axis=-1, keepdims=True))
           - jnp.exp(jnp.sum(lq2_ref[row, :] * lk2_ref[row, :], axis=-1, keepdims=True))
           + lambda_init)
    nt = (((1,), (1,)), ((), ()))
    tn = (((0,), (0,)), ((), ()))

    def rows(meta_ref, frames_ref, r0, n):
        if r0 < CHUNK:
            return meta_ref[r0:r0 + n, :]
        return frames_ref[0, r0 - CHUNK:r0 - CHUNK + n, :]

    def fold(state, key, value, op):
        state[key] = value if key not in state else op(state[key], value)

    def score_tiles(blk, slot, state):
        r0, nq = _QUERY_BLOCKS[blk]
        nk = r0 + nq
        q = rows(qm_ref, qf_ref, r0, nq)
        lane = lax.broadcasted_iota(jnp.int32, (nq, LANES), 1)
        zero = jnp.zeros_like(q)
        qq = jnp.concatenate([jnp.where(lane < DIFF_D, q, zero),
                              jnp.where(lane >= DIFF_D, q, zero)], axis=0)

        def tile(t0, tk):
            s = lax.dot_general(rows(km_ref, kf_ref, t0, tk), qq, nt, preferred_element_type=F32)
            krow = lax.broadcasted_iota(jnp.int32, (tk, 2 * nq), 0)
            if t0 + tk > r0:
                qcol = lax.broadcasted_iota(jnp.int32, (tk, 2 * nq), 1) % nq
                s = jnp.where(t0 + krow <= r0 + qcol, s, NEG)
            s_ref[slot, t0:t0 + tk, :2 * nq] = s
            fold(state, "max", jnp.max(s, axis=0, keepdims=True), jnp.maximum)

        return [functools.partial(tile, t0, tk) for t0, tk in _key_tiles(nk)]

    def prob_tiles(blk, slot, smax, state):
        r0, nq = _QUERY_BLOCKS[blk]
        nk = r0 + nq

        def tile(t0, tk):
            p = jnp.exp2(s_ref[slot, t0:t0 + tk, :2 * nq] - smax)
            fold(state, "pos", jnp.sum(p[:, :nq], axis=0, keepdims=True), jnp.add)
            fold(state, "neg", jnp.sum(p[:, nq:], axis=0, keepdims=True), jnp.add)
            p_ref[slot, t0:t0 + tk, :2 * nq] = p.astype(BF16)

        return [functools.partial(tile, t0, tk) for t0, tk in _key_tiles(nk)]

    def finish(blk, slot, sums):
        r0, nq = _QUERY_BLOCKS[blk]
        nk = r0 + nq
        ot = lax.dot_general(vm_ref[PAD:, :], p_ref[slot, PAD:CHUNK, :2 * nq], tn,
                             preferred_element_type=F32)
        if nk > CHUNK:
            ot = ot + lax.dot_general(vf_ref[0, :nk - CHUNK, :], p_ref[slot, CHUNK:nk, :2 * nq], tn,
                                      preferred_element_type=F32)
        o = (ot[:, :nq] * (1.0 / sums["pos"]) - ot[:, nq:] * (lam / sums["neg"])).T
        o = o * lax.rsqrt(jnp.mean(o * o, axis=-1, keepdims=True) + EPS) * w_ref[row, :]
        o = o * (1.0 - lambda_init)
        if r0 == 0:
            orow = lax.broadcasted_iota(jnp.int32, (nq, LANES), 0)
            om_ref[...] = jnp.where(orow >= PAD, o, 0.0).astype(BF16)
        else:
            of_ref[0, r0 - CHUNK:nk - CHUNK, :] = o.astype(BF16)

    n_blocks = len(_QUERY_BLOCKS)
    order = list(range(0, n_blocks, 2)) + list(range(n_blocks - 1 - n_blocks % 2, 0, -2))
    score_state, pending = {}, None
    for run in score_tiles(order[0], 0, score_state):
        run()
    for r, blk in enumerate(order):
        smax, sums, score_state = score_state["max"], {}, {}
        ahead = score_tiles(order[r + 1], (r + 1) % 2, score_state) if r + 1 < n_blocks else []
        current = prob_tiles(blk, r % 2, smax, sums)
        for n in range(max(len(ahead), len(current))):
            for runs in (current, ahead):
                if n < len(runs):
                    runs[n]()
            if n == 0 and pending is not None:
                finish(*pending)
        pending = (blk, r % 2, sums)
    finish(*pending)


def _diffattn(layer, proj, subln_w, lq1, lk1, lq2, lk2):
    proj_frames, proj_meta = proj
    frames3 = proj_frames.reshape(BATCH, SEQ, DIFF_COLS)
    lambda_init = 0.8 - 0.6 * math.exp(-0.3 * layer)
    fblk = lambda c: pl.BlockSpec((1, SEQ, LANES), lambda hd, b, c=c: (b, 0, c + hd))
    mblk = lambda c: pl.BlockSpec((CHUNK, LANES), lambda hd, b, c=c: (0, c + hd))
    small = lambda n: _whole((DEPTH, n))
    y_frames, y_meta = pl.pallas_call(
        functools.partial(_diffattn_kernel, layer=layer, lambda_init=lambda_init),
        grid=(DIFF_HEADS, BATCH),
        in_specs=[fblk(0), fblk(DIFF_HEADS), fblk(2 * DIFF_HEADS),
                  mblk(0), mblk(DIFF_HEADS), mblk(2 * DIFF_HEADS),
                  small(DIFF_DV), small(DIFF_D), small(DIFF_D), small(DIFF_D), small(DIFF_D)],
        out_specs=[pl.BlockSpec((1, SEQ, LANES), lambda hd, b: (b, 0, hd)),
                   pl.BlockSpec((CHUNK, LANES), lambda hd, b: (0, hd))],
        out_shape=[jax.ShapeDtypeStruct((BATCH, SEQ, DIFF_WIDTH), BF16),
                   jax.ShapeDtypeStruct((CHUNK, DIFF_WIDTH), BF16)],
        scratch_shapes=[pltpu.VMEM((2, T_PAD, 4 * CHUNK), F32),
                        pltpu.VMEM((2, T_PAD, 4 * CHUNK), BF16)],
        compiler_params=pltpu.CompilerParams(
            dimension_semantics=("arbitrary", "arbitrary"), vmem_limit_bytes=VMEM_LIMIT),
        name="diffattn",
    )(frames3, frames3, frames3, proj_meta, proj_meta, proj_meta, subln_w, lq1, lk1, lq2, lk2)
    return y_frames.reshape(FRAME_ROWS, DIFF_WIDTH), y_meta


def _rotary_tables():
    f32 = np.float32
    pos = np.arange(T_PAD, dtype=f32) - f32(PAD)
    angle = (f32(RET_THETA) ** (-np.linspace(0.0, 1.0, RET_DK // 2, dtype=f32))).astype(f32)
    fr = pos[:, None] * angle[None, :]
    c, s = np.cos(fr), np.sin(fr)
    zero = np.zeros_like(s)
    cos_i = np.repeat(c, 2, axis=-1)
    sin_even = np.stack([-s, zero], axis=-1).reshape(T_PAD, RET_DK)
    sin_odd = np.stack([zero, s], axis=-1).reshape(T_PAD, RET_DK)
    ks = f32(RET_DK ** -0.5)
    inv = (f32(ROPE_THETA) ** (-np.arange(0, ROPE_DIMS, 2, dtype=f32) / f32(ROPE_DIMS))).astype(f32)
    fq = pos[:, None] * inv[None, :]
    emb = np.concatenate([fq, fq], axis=-1)
    ce, se = np.cos(emb), np.sin(emb)
    half, rest = ROPE_DIMS // 2, DIFF_D - ROPE_DIMS
    zeros = lambda n: np.zeros((T_PAD, n), f32)
    two = lambda a: np.concatenate([a, a], axis=-1)
    c_d = two(np.concatenate([ce, np.ones((T_PAD, rest), f32)], axis=-1))
    s_lo = two(np.concatenate([zeros(half), se[:, half:], zeros(rest)], axis=-1))
    s_hi = two(np.concatenate([-se[:, :half], zeros(half + rest)], axis=-1))
    qs = f32((DIFF_D ** -0.5) * math.log2(math.e))
    tabs = [cos_i, sin_even, sin_odd, cos_i * ks, sin_even * ks, sin_odd * ks,
            c_d * qs, s_lo * qs, s_hi * qs, c_d, s_lo, s_hi]
    return np.concatenate(tabs, axis=-1).astype(f32)


def _retention_consts():
    f32 = np.float32
    log_gamma = np.log(f32(1.0) - f32(2.0) ** (f32(-5.0) - np.arange(RET_HEADS, dtype=f32))).astype(f32)
    idx = np.arange(CHUNK, dtype=f32)
    rel = idx[:, None] - idx[None, :]
    dintra = np.where(rel >= 0, np.exp(log_gamma[:, None, None] * np.maximum(rel, f32(0.0))), f32(0.0))
    k_decay = np.exp(log_gamma[:, None] * (CHUNK - 1 - idx)[None, :])
    q_decay = np.exp(log_gamma[:, None] * (idx + f32(1.0))[None, :])
    cd = np.exp(log_gamma * f32(CHUNK))
    bc = lambda a: np.ascontiguousarray(np.broadcast_to(a[:, :, None], (RET_HEADS, CHUNK, LANES)))
    return cd.astype(f32), dintra.astype(f32), bc(q_decay).astype(f32), bc(k_decay).astype(f32)


def kernel(x, meta_tokens, ffn1_norm, ffn1_w_gate, ffn1_w_up, ffn1_w_down, mix_norm, w_in, ret_gn_w, diff_subln_w, diff_lambda_q1, diff_lambda_k1, diff_lambda_q2, diff_lambda_k2, w_out, ffn2_norm, ffn2_w_gate, ffn2_w_up, ffn2_w_down, final_norm):
    meta_chunk = jnp.concatenate([jnp.zeros((PAD, D_MODEL), F32), meta_tokens.astype(F32)], axis=0)
    h = (x.astype(F32).reshape(FRAME_ROWS, D_MODEL), meta_chunk)

    tab = _rotary_tables()
    tab_meta, tab_frames = jnp.asarray(tab[:CHUNK]), jnp.asarray(tab[CHUNK:])
    cd, dintra, qdec, kdec = (jnp.asarray(a) for a in _retention_consts())

    for l in range(DEPTH):
        last = l == DEPTH - 1
        h = _ffn(l, h, ffn1_norm, ffn1_w_gate, ffn1_w_up, ffn1_w_down)
        proj_diff, y_ret = _inproj_retention(l, h, mix_norm, w_in, tab_frames, tab_meta,
                                             cd, dintra, qdec, kdec, ret_gn_w)
        y_diff = _diffattn(l, proj_diff, diff_subln_w, diff_lambda_q1, diff_lambda_k1,
                           diff_lambda_q2, diff_lambda_k2)
        h = _ffn(l, h, ffn2_norm, ffn2_w_gate, ffn2_w_up, ffn2_w_down,
                 mix=(y_ret, y_diff, w_out),
                 final_g=final_norm.reshape(1, D_MODEL) if last else None)

    return h[0].reshape(BATCH, SEQ, D_MODEL).astype(x.dtype)
```

```python
import functools
import math

import numpy as np
import jax
import jax.numpy as jnp
from jax import lax
from jax.experimental import pallas as pl
from jax.experimental.pallas import tpu as pltpu

D_MODEL = 1024
BATCH = 8
SEQ = 2048
DEPTH = 2
N_META = 16
CHUNK = 128
RET_HEADS = 4
RET_DK = 128
RET_WIDTH = 512
RET_THETA = 10000.0
DIFF_HEADS = 4
DIFF_D = 64
DIFF_DV = 128
DIFF_WIDTH = 512
ROPE_THETA = 500000.0
ROPE_DIMS = 16
D_FF = 2816
EPS = 1e-6
D_IN = 3584

PAD = CHUNK - N_META
T_PAD = CHUNK + SEQ
FRAME_ROWS = BATCH * SEQ
N_TAB = 12
LANES = 128
VMEM_LIMIT = 56 * 1024 * 1024
NEG = -1e30

F32 = jnp.float32
BF16 = jnp.bfloat16


def _rms(x, g):
    return x * lax.rsqrt(jnp.mean(x * x, axis=-1, keepdims=True) + EPS) * g


def _whole(shape):
    return pl.BlockSpec(shape, lambda *_: (0,) * len(shape))


class _DenseGrid:
    def __init__(self, tile, stage, meta):
        self.tile, self.stage, self.meta = tile, stage, meta
        self.n_tiles = FRAME_ROWS // tile
        self.steps = stage + self.n_tiles + int(meta)

    def tile_index(self, i):
        return jnp.clip(i - self.stage, 0, self.n_tiles - 1)

    def row_tile(self, width):
        return pl.BlockSpec((self.tile, width), lambda i: (self.tile_index(i), 0))

    def weight_chunk(self, layer, rows, cols):
        return pl.BlockSpec((None, rows // self.stage, cols),
                            lambda i: (layer, jnp.minimum(i, self.stage - 1), 0))

    def run(self, i, stage, frames, meta):
        pl.when(i < self.stage)(stage)
        pl.when((i >= self.stage) & (i < self.stage + self.n_tiles))(frames)
        if self.meta:
            pl.when(i == self.stage + self.n_tiles)(meta)


def _stage_weight(i, src_ref, dst_ref):
    rows = src_ref.shape[0]
    dst_ref[pl.ds(pl.multiple_of(i * rows, 16), rows), :] = src_ref[...].astype(BF16)


FFN_TILE = 512
FFN_SUB_TILE = 256
STAGE_STEPS = 8
PROJ_TILE = 512


def _ffn_kernel(*refs, grid, layer, mix, final):
    refs = list(refs)
    take = lambda n: [refs.pop(0) for _ in range(n)]
    with_meta = not final
    h_refs = take(2 if with_meta else 1)
    if mix:
        yr_refs, yd_refs = take(len(h_refs)), take(len(h_refs))
        (wo_ref,) = take(1)
    g_ref, wg_ref, wu_ref, wd_ref = take(4)
    if final:
        (fn_ref,) = take(1)
    o_refs = take(len(h_refs))
    wg_s, wu_s, wd_s = take(3)
    if mix:
        (wo_s,) = take(1)
    i = pl.program_id(0)

    def stage():
        _stage_weight(i, wg_ref, wg_s)
        _stage_weight(i, wu_ref, wu_s)
        _stage_weight(i, wd_ref, wd_s)
        if mix:
            _stage_weight(i, wo_ref, wo_s)

    def rows(which):
        n = h_refs[which].shape[0]
        parts = [slice(r, min(r + FFN_SUB_TILE, n)) for r in range(0, n, FFN_SUB_TILE)]
        xs, xns, acts = [], [], []
        for sl in parts:
            x = h_refs[which][sl, :]
            if mix:
                x = (x + jnp.dot(yr_refs[which][sl, :], wo_s[:RET_WIDTH, :],
                                 preferred_element_type=F32)
                     + jnp.dot(yd_refs[which][sl, :], wo_s[RET_WIDTH:, :],
                               preferred_element_type=F32))
            xs.append(x)
            xns.append(_rms(x, g_ref[layer:layer + 1, :]).astype(BF16))
        for xn in xns:
            gate = jnp.dot(xn, wg_s[...], preferred_element_type=F32)
            up = jnp.dot(xn, wu_s[...], preferred_element_type=F32)
            acts.append((jax.nn.silu(gate) * up).astype(BF16))
        for sl, x, act in zip(parts, xs, acts):
            y = x + 0.5 * jnp.dot(act, wd_s[...], preferred_element_type=F32)
            if final:
                y = _rms(y, fn_ref[...])
            o_refs[which][sl, :] = y

    grid.run(i, stage, functools.partial(rows, 0), functools.partial(rows, 1))


def _ffn(layer, h, norm_g, wg, wu, wd, mix=None, final_g=None):
    with_meta = final_g is None
    grid = _DenseGrid(tile=FFN_TILE, stage=STAGE_STEPS, meta=with_meta)
    _row_tile, _weight_chunk = grid.row_tile, grid.weight_chunk
    pair = lambda width: [_row_tile(width)] + ([_whole((CHUNK, width))] if with_meta else [])
    keep = lambda arrays: list(arrays) if with_meta else [arrays[0]]
    args, specs = keep(h), pair(D_MODEL)
    scratch = [pltpu.VMEM((D_MODEL, D_FF), BF16), pltpu.VMEM((D_MODEL, D_FF), BF16),
               pltpu.VMEM((D_FF, D_MODEL), BF16)]
    if mix is not None:
        y_ret, y_diff, w_out = mix
        args += keep(y_ret) + keep(y_diff) + [w_out]
        specs += pair(RET_WIDTH) + pair(DIFF_WIDTH) + [_weight_chunk(layer, D_MODEL, D_MODEL)]
        scratch.append(pltpu.VMEM((D_MODEL, D_MODEL), BF16))
    args += [norm_g, wg, wu, wd]
    specs += [_whole((DEPTH, D_MODEL)), _weight_chunk(layer, D_MODEL, D_FF),
              _weight_chunk(layer, D_MODEL, D_FF), _weight_chunk(layer, D_FF, D_MODEL)]
    if final_g is not None:
        args.append(final_g)
        specs.append(_whole((1, D_MODEL)))
    out_shape = [jax.ShapeDtypeStruct((FRAME_ROWS, D_MODEL), F32)]
    if with_meta:
        out_shape.append(jax.ShapeDtypeStruct((CHUNK, D_MODEL), F32))
    return pl.pallas_call(
        functools.partial(_ffn_kernel, grid=grid, layer=layer, mix=mix is not None,
                          final=final_g is not None),
        grid=(grid.steps,),
        in_specs=specs,
        out_specs=pair(D_MODEL),
        out_shape=out_shape,
        scratch_shapes=scratch,
        compiler_params=pltpu.CompilerParams(
            dimension_semantics=("arbitrary",), vmem_limit_bytes=VMEM_LIMIT),
        name="ffn",
    )(*args)


_ROTATIONS = (
    (0, 0, LANES - 1, 1),
    (RET_WIDTH, 3, LANES - 1, 1),
    (4 * RET_WIDTH, 6, 8, LANES - 8),
    (4 * RET_WIDTH + DIFF_WIDTH, 9, 8, LANES - 8),
)


RET_COLS = 4 * RET_WIDTH
DIFF_COLS = D_IN - RET_COLS
N_PROJ_TILES = FRAME_ROWS // PROJ_TILE
TILES_PER_BATCH = SEQ // PROJ_TILE


def _inproj_retention_kernel(cd_ref, hf_ref, hm_ref, g_ref, w_ref, tabf_ref, tabm_ref,
                             dintra_ref, qdec_ref, kdec_ref, gnw_ref,
                             pf_ref, pm_ref, yf_ref, ym_ref,
                             w_s, ring_ref, state_ref, meta_state_ref, *, layer):
    nt = (((1,), (1,)), ((), ()))
    tn = (((0,), (0,)), ((), ()))
    i = pl.program_id(0)
    t = i - (STAGE_STEPS + 1)

    def project_steps(h_ref, tab_ref, ret_ref, diff_ref):
        rotation = {base: rest for base, *rest in _ROTATIONS}
        cell = {}

        def tab(k):
            return tab_ref[:, k * LANES:(k + 1) * LANES]

        def group(base):
            if not cell:
                cell["xn"] = _rms(h_ref[...], g_ref[layer:layer + 1, :]).astype(BF16)
            p = jnp.dot(cell["xn"], w_s[:, base:base + RET_WIDTH], preferred_element_type=F32)
            dst, lo = (ret_ref, base) if base < RET_COLS else (diff_ref, base - RET_COLS)
            if base in rotation:
                t0, roll_a, roll_b = rotation[base]
                for hd in range(RET_HEADS):
                    xs = p[:, hd * LANES:(hd + 1) * LANES]
                    rot = (xs * tab(t0) + pltpu.roll(xs, roll_a, 1) * tab(t0 + 1)
                           + pltpu.roll(xs, roll_b, 1) * tab(t0 + 2))
                    dst[:, lo + hd * LANES:lo + (hd + 1) * LANES] = rot.astype(BF16)
            else:
                dst[:, lo:lo + RET_WIDTH] = p.astype(BF16)

        return [functools.partial(group, base) for base in range(0, D_IN, RET_WIDTH)]

    def retention_steps(src_ref, y_ref, state_in, state_out_refs):
        n_chunks = src_ref.shape[0] // CHUNK
        units = [(c, hd) for c in range(n_chunks) for hd in range(RET_HEADS)]
        scores, incs, outs, cell = {}, {}, {}, {}

        def col(c, hd, which):
            return src_ref[c * CHUNK:(c + 1) * CHUNK,
                           which * RET_WIDTH + hd * LANES:which * RET_WIDTH + (hd + 1) * LANES]

        def score(c, hd):
            s = lax.dot_general(col(c, hd, 0), col(c, hd, 1), nt, preferred_element_type=F32)
            scores[c, hd] = (s * dintra_ref[hd]).astype(BF16)

        def increment(c, hd):
            kd = (col(c, hd, 1).astype(F32) * kdec_ref[hd]).astype(BF16)
            incs[c, hd] = lax.dot_general(kd, col(c, hd, 2), tn, preferred_element_type=F32)

        def output(c, hd):
            if not cell:
                cell["state"] = state_in()
            state = cell["state"]
            if state is None:
                outs[c, hd] = jnp.dot(scores[c, hd], col(c, hd, 2), preferred_element_type=F32)
            else:
                qd = (col(c, hd, 0).astype(F32) * qdec_ref[hd]).astype(BF16)
                outs[c, hd] = jnp.dot(jnp.concatenate([scores[c, hd], qd], axis=1),
                                      jnp.concatenate([col(c, hd, 2), state[hd].astype(BF16)],
                                                      axis=0),
                                      preferred_element_type=F32)
            if hd == RET_HEADS - 1:
                if state is None:
                    cell["state"] = [incs[c, h] for h in range(RET_HEADS)]
                else:
                    cell["state"] = [state[h] * cd_ref[h] + incs[c, h] for h in range(RET_HEADS)]

        def store_state():
            for hd in range(RET_HEADS):
                for ref in state_out_refs:
                    ref[hd] = cell["state"][hd]

        def norm_gate(c, hd):
            o = outs[c, hd]
            mu = jnp.mean(o, axis=-1, keepdims=True)
            d = o - mu
            var = jnp.mean(d * d, axis=-1, keepdims=True)
            on = d * lax.rsqrt(var + EPS) * gnw_ref[layer:layer + 1, hd * LANES:(hd + 1) * LANES]
            y_ref[c * CHUNK:(c + 1) * CHUNK, hd * LANES:(hd + 1) * LANES] = (
                jax.nn.silu(col(c, hd, 3).astype(F32)) * on).astype(BF16)

        return ([functools.partial(stage, c, hd) for stage in (score, increment, output)
                 for c, hd in units] + [store_state]
                + [functools.partial(norm_gate, c, hd) for c, hd in units])

    def previous_tile_retention_steps():
        def state_in():
            first = (t + TILES_PER_BATCH - 1) % TILES_PER_BATCH == 0
            return [jnp.where(first, meta_state_ref[hd], state_ref[hd])
                    for hd in range(RET_HEADS)]

        return retention_steps(ring_ref.at[(t + 1) % 2], yf_ref, state_in, [state_ref])

    def emit(major, minor=()):
        per = -(-len(minor) // len(major))
        for k, step in enumerate(major):
            step()
            for extra in minor[k * per:(k + 1) * per]:
                extra()

    @pl.when(i < STAGE_STEPS)
    def _():
        _stage_weight(i, w_ref, w_s)

    @pl.when(i == STAGE_STEPS)
    def _():
        ring_ref[...] = jnp.zeros_like(ring_ref)
        state_ref[...] = jnp.zeros_like(state_ref)
        meta_ring = ring_ref.at[0, :CHUNK]
        emit(project_steps(hm_ref, tabm_ref, meta_ring, pm_ref))
        emit(retention_steps(meta_ring, ym_ref, lambda: None, [meta_state_ref]))

    @pl.when((t >= 0) & (t < N_PROJ_TILES))
    def _():
        steps = previous_tile_retention_steps()
        n_norm = (PROJ_TILE // CHUNK) * RET_HEADS
        emit(steps[:-n_norm])
        emit(project_steps(hf_ref, tabf_ref, ring_ref.at[t % 2], pf_ref), steps[-n_norm:])

    @pl.when(t == N_PROJ_TILES)
    def _():
        emit(previous_tile_retention_steps())


def _inproj_retention(layer, h, norm_g, w_in, tab_frames, tab_meta, cd, dintra, qdec, kdec, gn_w):
    steps = STAGE_STEPS + 1 + N_PROJ_TILES + 1
    tile = lambda i: jnp.clip(i - (STAGE_STEPS + 1), 0, N_PROJ_TILES - 1)
    lag = lambda i: jnp.clip(i - (STAGE_STEPS + 2), 0, N_PROJ_TILES - 1)
    const3 = _whole((RET_HEADS, CHUNK, LANES))
    outs = pl.pallas_call(
        functools.partial(_inproj_retention_kernel, layer=layer),
        grid=(steps,),
        in_specs=[pl.BlockSpec(memory_space=pltpu.SMEM),
                  pl.BlockSpec((PROJ_TILE, D_MODEL), lambda i: (tile(i), 0)),
                  _whole((CHUNK, D_MODEL)), _whole((DEPTH, D_MODEL)),
                  pl.BlockSpec((None, D_MODEL // STAGE_STEPS, D_IN),
                               lambda i: (layer, jnp.minimum(i, STAGE_STEPS - 1), 0)),
                  pl.BlockSpec((PROJ_TILE, N_TAB * LANES),
                               lambda i: (tile(i) % TILES_PER_BATCH, 0)),
                  _whole((CHUNK, N_TAB * LANES)),
                  const3, const3, const3, _whole((DEPTH, RET_WIDTH))],
        out_specs=[pl.BlockSpec((PROJ_TILE, DIFF_COLS), lambda i: (tile(i), 0)),
                   _whole((CHUNK, DIFF_COLS)),
                   pl.BlockSpec((PROJ_TILE, RET_WIDTH), lambda i: (lag(i), 0)),
                   _whole((CHUNK, RET_WIDTH))],
        out_shape=[jax.ShapeDtypeStruct((FRAME_ROWS, DIFF_COLS), BF16),
                   jax.ShapeDtypeStruct((CHUNK, DIFF_COLS), BF16),
                   jax.ShapeDtypeStruct((FRAME_ROWS, RET_WIDTH), BF16),
                   jax.ShapeDtypeStruct((CHUNK, RET_WIDTH), BF16)],
        scratch_shapes=[pltpu.VMEM((D_MODEL, D_IN), BF16),
                        pltpu.VMEM((2, PROJ_TILE, RET_COLS), BF16),
                        pltpu.VMEM((RET_HEADS, RET_DK, LANES), F32),
                        pltpu.VMEM((RET_HEADS, RET_DK, LANES), F32)],
        compiler_params=pltpu.CompilerParams(
            dimension_semantics=("arbitrary",), vmem_limit_bytes=VMEM_LIMIT),
        name="inproj_retention",
    )(cd, h[0], h[1], norm_g, w_in, tab_frames, tab_meta, dintra, qdec, kdec, gn_w)
    return (outs[0], outs[1]), (outs[2], outs[3])


_QUERY_BLOCKS = ((0, CHUNK),) + tuple((r, 2 * CHUNK) for r in range(CHUNK, T_PAD, 2 * CHUNK))


def _key_tiles(nk):
    return ((PAD, N_META),) + tuple((r, 2 * CHUNK) for r in range(CHUNK, nk, 2 * CHUNK))


def _diffattn_kernel(qf_ref, kf_ref, vf_ref, qm_ref, km_ref, vm_ref, w_ref, lq1_ref, lk1_ref,
                     lq2_ref, lk2_ref, of_ref, om_ref, s_ref, p_ref, *, layer, lambda_init):
    row = slice(layer, layer + 1)
    lam = (jnp.exp(jnp.sum(lq1_ref[row, :] * lk1_ref[row, :], axis=-1, keepdims=True))
           - jnp.exp(jnp.sum(lq2_ref[row, :] * lk2_ref[row, :], axis=-1, keepdims=True))
           + lambda_init)
    w_out = w_ref[row, :] * (1.0 - lambda_init)
    nt = (((1,), (1,)), ((), ()))
    tn = (((0,), (0,)), ((), ()))

    def rows(meta_ref, frames_ref, r0, n):
        if r0 < CHUNK:
            return meta_ref[r0:r0 + n, :]
        return frames_ref[0, r0 - CHUNK:r0 - CHUNK + n, :]

    def fold(state, key, value, op):
        state[key] = value if key not in state else op(state[key], value)

    def score_tiles(blk, slot, state):
        r0, nq = _QUERY_BLOCKS[blk]
        nk = r0 + nq
        q = rows(qm_ref, qf_ref, r0, nq)
        lane = lax.broadcasted_iota(jnp.int32, (nq, LANES), 1)
        zero = jnp.zeros_like(q)
        qq = jnp.concatenate([jnp.where(lane < DIFF_D, q, zero),
                              jnp.where(lane >= DIFF_D, q, zero)], axis=0)

        def tile(t0, tk):
            s = lax.dot_general(rows(km_ref, kf_ref, t0, tk), qq, nt, preferred_element_type=F32)
            krow = lax.broadcasted_iota(jnp.int32, (tk, 2 * nq), 0)
            if t0 + tk > r0:
                qcol = lax.broadcasted_iota(jnp.int32, (tk, 2 * nq), 1) % nq
                s = jnp.where(krow + (t0 - r0) <= qcol, s, NEG)
            s_ref[slot, t0:t0 + tk, :2 * nq] = s
            fold(state, "max", jnp.max(s, axis=0, keepdims=True), jnp.maximum)

        return [functools.partial(tile, t0, tk) for t0, tk in _key_tiles(nk)]

    def prob_tiles(blk, slot, smax, state):
        r0, nq = _QUERY_BLOCKS[blk]
        nk = r0 + nq

        def tile(t0, tk):
            p = jnp.exp2(s_ref[slot, t0:t0 + tk, :2 * nq] - smax)
            fold(state, "pos", jnp.sum(p[:, :nq], axis=0, keepdims=True), jnp.add)
            fold(state, "neg", jnp.sum(p[:, nq:], axis=0, keepdims=True), jnp.add)
            p_ref[slot, t0:t0 + tk, :2 * nq] = p.astype(BF16)

        return [functools.partial(tile, t0, tk) for t0, tk in _key_tiles(nk)]

    def finish(blk, slot, sums):
        r0, nq = _QUERY_BLOCKS[blk]
        nk = r0 + nq
        ot = lax.dot_general(vm_ref[PAD:, :], p_ref[slot, PAD:CHUNK, :2 * nq], tn,
                             preferred_element_type=F32)
        if nk > CHUNK:
            ot = ot + lax.dot_general(vf_ref[0, :nk - CHUNK, :], p_ref[slot, CHUNK:nk, :2 * nq], tn,
                                      preferred_element_type=F32)
        o = (ot[:, :nq] * (1.0 / sums["pos"]) - ot[:, nq:] * (lam / sums["neg"])).T
        o = o * lax.rsqrt(jnp.mean(o * o, axis=-1, keepdims=True) + EPS) * w_out
        if r0 == 0:
            orow = lax.broadcasted_iota(jnp.int32, (nq, LANES), 0)
            om_ref[...] = jnp.where(orow >= PAD, o, 0.0).astype(BF16)
        else:
            of_ref[0, r0 - CHUNK:nk - CHUNK, :] = o.astype(BF16)

    n_blocks = len(_QUERY_BLOCKS)
    order = list(range(0, n_blocks, 2)) + list(range(n_blocks - 1 - n_blocks % 2, 0, -2))
    score_state, pending = {}, None
    for run in score_tiles(order[0], 0, score_state):
        run()
    for r, blk in enumerate(order):
        smax, sums, score_state = score_state["max"], {}, {}
        ahead = score_tiles(order[r + 1], (r + 1) % 2, score_state) if r + 1 < n_blocks else []
        current = prob_tiles(blk, r % 2, smax, sums)
        for n in range(max(len(ahead), len(current))):
            for runs in (current, ahead):
                if n < len(runs):
                    runs[n]()
            if n == 0 and pending is not None:
                finish(*pending)
        pending = (blk, r % 2, sums)
    finish(*pending)


def _diffattn(layer, proj, subln_w, lq1, lk1, lq2, lk2):
    proj_frames, proj_meta = proj
    frames3 = proj_frames.reshape(BATCH, SEQ, DIFF_COLS)
    lambda_init = 0.8 - 0.6 * math.exp(-0.3 * layer)
    fblk = lambda c: pl.BlockSpec((1, SEQ, LANES), lambda hd, b, c=c: (b, 0, c + hd))
    mblk = lambda c: pl.BlockSpec((CHUNK, LANES), lambda hd, b, c=c: (0, c + hd))
    small = lambda n: _whole((DEPTH, n))
    y_frames, y_meta = pl.pallas_call(
        functools.partial(_diffattn_kernel, layer=layer, lambda_init=lambda_init),
        grid=(DIFF_HEADS, BATCH),
        in_specs=[fblk(0), fblk(DIFF_HEADS), fblk(2 * DIFF_HEADS),
                  mblk(0), mblk(DIFF_HEADS), mblk(2 * DIFF_HEADS),
                  small(DIFF_DV), small(DIFF_D), small(DIFF_D), small(DIFF_D), small(DIFF_D)],
        out_specs=[pl.BlockSpec((1, SEQ, LANES), lambda hd, b: (b, 0, hd)),
                   pl.BlockSpec((CHUNK, LANES), lambda hd, b: (0, hd))],
        out_shape=[jax.ShapeDtypeStruct((BATCH, SEQ, DIFF_WIDTH), BF16),
                   jax.ShapeDtypeStruct((CHUNK, DIFF_WIDTH), BF16)],
        scratch_shapes=[pltpu.VMEM((2, T_PAD, 4 * CHUNK), F32),
                        pltpu.VMEM((2, T_PAD, 4 * CHUNK), BF16)],
        compiler_params=pltpu.CompilerParams(
            dimension_semantics=("arbitrary", "arbitrary"), vmem_limit_bytes=VMEM_LIMIT),
        name="diffattn",
    )(frames3, frames3, frames3, proj_meta, proj_meta, proj_meta, subln_w, lq1, lk1, lq2, lk2)
    return y_frames.reshape(FRAME_ROWS, DIFF_WIDTH), y_meta


def _rotary_tables():
    f32 = np.float32
    pos = np.arange(T_PAD, dtype=f32) - f32(PAD)
    angle = (f32(RET_THETA) ** (-np.linspace(0.0, 1.0, RET_DK // 2, dtype=f32))).astype(f32)
    fr = pos[:, None] * angle[None, :]
    c, s = np.cos(fr), np.sin(fr)
    zero = np.zeros_like(s)
    cos_i = np.repeat(c, 2, axis=-1)
    sin_even = np.stack([-s, zero], axis=-1).reshape(T_PAD, RET_DK)
    sin_odd = np.stack([zero, s], axis=-1).reshape(T_PAD, RET_DK)
    ks = f32(RET_DK ** -0.5)
    inv = (f32(ROPE_THETA) ** (-np.arange(0, ROPE_DIMS, 2, dtype=f32) / f32(ROPE_DIMS))).astype(f32)
    fq = pos[:, None] * inv[None, :]
    emb = np.concatenate([fq, fq], axis=-1)
    ce, se = np.cos(emb), np.sin(emb)
    half, rest = ROPE_DIMS // 2, DIFF_D - ROPE_DIMS
    zeros = lambda n: np.zeros((T_PAD, n), f32)
    two = lambda a: np.concatenate([a, a], axis=-1)
    c_d = two(np.concatenate([ce, np.ones((T_PAD, rest), f32)], axis=-1))
    s_lo = two(np.concatenate([zeros(half), se[:, half:], zeros(rest)], axis=-1))
    s_hi = two(np.concatenate([-se[:, :half], zeros(half + rest)], axis=-1))
    qs = f32((DIFF_D ** -0.5) * math.log2(math.e))
    tabs = [cos_i, sin_even, sin_odd, cos_i * ks, sin_even * ks, sin_odd * ks,
            c_d * qs, s_lo * qs, s_hi * qs, c_d, s_lo, s_hi]
    return np.concatenate(tabs, axis=-1).astype(f32)


def _retention_consts():
    f32 = np.float32
    log_gamma = np.log(f32(1.0) - f32(2.0) ** (f32(-5.0) - np.arange(RET_HEADS, dtype=f32))).astype(f32)
    idx = np.arange(CHUNK, dtype=f32)
    rel = idx[:, None] - idx[None, :]
    dintra = np.where(rel >= 0, np.exp(log_gamma[:, None, None] * np.maximum(rel, f32(0.0))), f32(0.0))
    k_decay = np.exp(log_gamma[:, None] * (CHUNK - 1 - idx)[None, :])
    q_decay = np.exp(log_gamma[:, None] * (idx + f32(1.0))[None, :])
    cd = np.exp(log_gamma * f32(CHUNK))
    bc = lambda a: np.ascontiguousarray(np.broadcast_to(a[:, :, None], (RET_HEADS, CHUNK, LANES)))
    return cd.astype(f32), dintra.astype(f32), bc(q_decay).astype(f32), bc(k_decay).astype(f32)


def kernel(x, meta_tokens, ffn1_norm, ffn1_w_gate, ffn1_w_up, ffn1_w_down, mix_norm, w_in, ret_gn_w, diff_subln_w, diff_lambda_q1, diff_lambda_k1, diff_lambda_q2, diff_lambda_k2, w_out, ffn2_norm, ffn2_w_gate, ffn2_w_up, ffn2_w_down, final_norm):
    meta_chunk = jnp.concatenate([jnp.zeros((PAD, D_MODEL), F32), meta_tokens.astype(F32)], axis=0)
    h = (x.astype(F32).reshape(FRAME_ROWS, D_MODEL), meta_chunk)

    tab = _rotary_tables()
    tab_meta, tab_frames = jnp.asarray(tab[:CHUNK]), jnp.asarray(tab[CHUNK:])
    cd, dintra, qdec, kdec = (jnp.asarray(a) for a in _retention_consts())

    for l in range(DEPTH):
        last = l == DEPTH - 1
        h = _ffn(l, h, ffn1_norm, ffn1_w_gate, ffn1_w_up, ffn1_w_down)
        proj_diff, y_ret = _inproj_retention(l, h, mix_norm, w_in, tab_frames, tab_meta,
                                             cd, dintra, qdec, kdec, ret_gn_w)
        y_diff = _diffattn(l, proj_diff, diff_subln_w, diff_lambda_q1, diff_lambda_k1,
                           diff_lambda_q2, diff_lambda_k2)
        h = _ffn(l, h, ffn2_norm, ffn2_w_gate, ffn2_w_up, ffn2_w_down,
                 mix=(y_ret, y_diff, w_out),
                 final_g=final_norm.reshape(1, D_MODEL) if last else None)

    return h[0].reshape(BATCH, SEQ, D_MODEL).astype(x.dtype)
```

```python
import functools
import math

import numpy as np
import jax
import jax.numpy as jnp
from jax import lax
from jax.experimental import pallas as pl
from jax.experimental.pallas import tpu as pltpu

D_MODEL = 1024
BATCH = 8
SEQ = 2048
DEPTH = 2
N_META = 16
CHUNK = 128
RET_HEADS = 4
RET_DK = 128
RET_WIDTH = 512
RET_THETA = 10000.0
DIFF_HEADS = 4
DIFF_D = 64
DIFF_DV = 128
DIFF_WIDTH = 512
ROPE_THETA = 500000.0
ROPE_DIMS = 16
D_FF = 2816
EPS = 1e-6
D_IN = 3584

PAD = CHUNK - N_META
T_PAD = CHUNK + SEQ
FRAME_ROWS = BATCH * SEQ
N_TAB = 12
LANES = 128
VMEM_LIMIT = 56 * 1024 * 1024
NEG = -1e30

F32 = jnp.float32
BF16 = jnp.bfloat16


def _rms(x, g):
    return x * lax.rsqrt(jnp.mean(x * x, axis=-1, keepdims=True) + EPS) * g


def _whole(shape):
    return pl.BlockSpec(shape, lambda *_: (0,) * len(shape))


class _DenseGrid:
    def __init__(self, tile, stage, meta):
        self.tile, self.stage, self.meta = tile, stage, meta
        self.n_tiles = FRAME_ROWS // tile
        self.steps = stage + self.n_tiles + int(meta)

    def tile_index(self, i):
        return jnp.clip(i - self.stage, 0, self.n_tiles - 1)

    def row_tile(self, width):
        return pl.BlockSpec((self.tile, width), lambda i: (self.tile_index(i), 0))

    def weight_chunk(self, layer, rows, cols):
        return pl.BlockSpec((None, rows // self.stage, cols),
                            lambda i: (layer, jnp.minimum(i, self.stage - 1), 0))

    def run(self, i, stage, frames, meta):
        pl.when(i < self.stage)(stage)
        pl.when((i >= self.stage) & (i < self.stage + self.n_tiles))(frames)
        if self.meta:
            pl.when(i == self.stage + self.n_tiles)(meta)


def _stage_weight(i, src_ref, dst_ref):
    rows = src_ref.shape[0]
    dst_ref[pl.ds(pl.multiple_of(i * rows, 16), rows), :] = src_ref[...].astype(BF16)


FFN_TILE = 512
FFN_SUB_TILE = 256
STAGE_STEPS = 8
PROJ_TILE = 512


def _ffn_kernel(*refs, grid, layer, mix, final):
    refs = list(refs)
    take = lambda n: [refs.pop(0) for _ in range(n)]
    with_meta = not final
    h_refs = take(2 if with_meta else 1)
    if mix:
        yr_refs, yd_refs = take(len(h_refs)), take(len(h_refs))
        (wo_ref,) = take(1)
    g_ref, wg_ref, wu_ref, wd_ref = take(4)
    if final:
        (fn_ref,) = take(1)
    o_refs = take(len(h_refs))
    wg_s, wu_s, wd_s = take(3)
    if mix:
        (wo_s,) = take(1)
    i = pl.program_id(0)

    def stage():
        _stage_weight(i, wg_ref, wg_s)
        _stage_weight(i, wu_ref, wu_s)
        _stage_weight(i, wd_ref, wd_s)
        if mix:
            _stage_weight(i, wo_ref, wo_s)

    def rows(which):
        n = h_refs[which].shape[0]
        parts = [slice(r, min(r + FFN_SUB_TILE, n)) for r in range(0, n, FFN_SUB_TILE)]
        xs, xns, acts = [], [], []
        for sl in parts:
            x = h_refs[which][sl, :]
            if mix:
                x = (x + jnp.dot(yr_refs[which][sl, :], wo_s[:RET_WIDTH, :],
                                 preferred_element_type=F32)
                     + jnp.dot(yd_refs[which][sl, :], wo_s[RET_WIDTH:, :],
                               preferred_element_type=F32))
            xs.append(x)
            xns.append(_rms(x, g_ref[layer:layer + 1, :]).astype(BF16))
        for xn in xns:
            gate = jnp.dot(xn, wg_s[...], preferred_element_type=F32)
            up = jnp.dot(xn, wu_s[...], preferred_element_type=F32)
            acts.append((jax.nn.silu(gate) * up).astype(BF16))
        for sl, x, act in zip(parts, xs, acts):
            y = x + 0.5 * jnp.dot(act, wd_s[...], preferred_element_type=F32)
            if final:
                y = _rms(y, fn_ref[...])
            o_refs[which][sl, :] = y

    grid.run(i, stage, functools.partial(rows, 0), functools.partial(rows, 1))


def _ffn(layer, h, norm_g, wg, wu, wd, mix=None, final_g=None):
    with_meta = final_g is None
    grid = _DenseGrid(tile=FFN_TILE, stage=STAGE_STEPS, meta=with_meta)
    _row_tile, _weight_chunk = grid.row_tile, grid.weight_chunk
    pair = lambda width: [_row_tile(width)] + ([_whole((CHUNK, width))] if with_meta else [])
    keep = lambda arrays: list(arrays) if with_meta else [arrays[0]]
    args, specs = keep(h), pair(D_MODEL)
    scratch = [pltpu.VMEM((D_MODEL, D_FF), BF16), pltpu.VMEM((D_MODEL, D_FF), BF16),
               pltpu.VMEM((D_FF, D_MODEL), BF16)]
    if mix is not None:
        y_ret, y_diff, w_out = mix
        args += keep(y_ret) + keep(y_diff) + [w_out]
        specs += pair(RET_WIDTH) + pair(DIFF_WIDTH) + [_weight_chunk(layer, D_MODEL, D_MODEL)]
        scratch.append(pltpu.VMEM((D_MODEL, D_MODEL), BF16))
    args += [norm_g, wg, wu, wd]
    specs += [_whole((DEPTH, D_MODEL)), _weight_chunk(layer, D_MODEL, D_FF),
              _weight_chunk(layer, D_MODEL, D_FF), _weight_chunk(layer, D_FF, D_MODEL)]
    if final_g is not None:
        args.append(final_g)
        specs.append(_whole((1, D_MODEL)))
    out_shape = [jax.ShapeDtypeStruct((FRAME_ROWS, D_MODEL), F32)]
    if with_meta:
        out_shape.append(jax.ShapeDtypeStruct((CHUNK, D_MODEL), F32))
    return pl.pallas_call(
        functools.partial(_ffn_kernel, grid=grid, layer=layer, mix=mix is not None,
                          final=final_g is not None),
        grid=(grid.steps,),
        in_specs=specs,
        out_specs=pair(D_MODEL),
        out_shape=out_shape,
        scratch_shapes=scratch,
        compiler_params=pltpu.CompilerParams(
            dimension_semantics=("arbitrary",), vmem_limit_bytes=VMEM_LIMIT),
        name="ffn",
    )(*args)


_ROTATIONS = (
    (0, 0, LANES - 1, 1),
    (RET_WIDTH, 3, LANES - 1, 1),
    (4 * RET_WIDTH, 6, 8, LANES - 8),
    (4 * RET_WIDTH + DIFF_WIDTH, 9, 8, LANES - 8),
)


RET_COLS = 4 * RET_WIDTH
DIFF_COLS = D_IN - RET_COLS
N_PROJ_TILES = FRAME_ROWS // PROJ_TILE
TILES_PER_BATCH = SEQ // PROJ_TILE


def _inproj_retention_kernel(cd_ref, hf_ref, hm_ref, g_ref, w_ref, tabf_ref, tabm_ref,
                             dintra_ref, qdec_ref, kdec_ref, gnw_ref,
                             pf_ref, pm_ref, yf_ref, ym_ref,
                             w_s, ring_ref, state_ref, meta_state_ref, *, layer):
    nt = (((1,), (1,)), ((), ()))
    tn = (((0,), (0,)), ((), ()))
    i = pl.program_id(0)
    t = i - (STAGE_STEPS + 1)

    def project_steps(h_ref, tab_ref, ret_ref, diff_ref):
        rotation = {base: rest for base, *rest in _ROTATIONS}
        cell = {}

        def tab(k):
            return tab_ref[:, k * LANES:(k + 1) * LANES]

        def group(base):
            if not cell:
                cell["xn"] = _rms(h_ref[...], g_ref[layer:layer + 1, :]).astype(BF16)
            p = jnp.dot(cell["xn"], w_s[:, base:base + RET_WIDTH], preferred_element_type=F32)
            dst, lo = (ret_ref, base) if base < RET_COLS else (diff_ref, base - RET_COLS)
            if base in rotation:
                t0, roll_a, roll_b = rotation[base]
                for hd in range(RET_HEADS):
                    xs = p[:, hd * LANES:(hd + 1) * LANES]
                    rot = (xs * tab(t0) + pltpu.roll(xs, roll_a, 1) * tab(t0 + 1)
                           + pltpu.roll(xs, roll_b, 1) * tab(t0 + 2))
                    dst[:, lo + hd * LANES:lo + (hd + 1) * LANES] = rot.astype(BF16)
            else:
                dst[:, lo:lo + RET_WIDTH] = p.astype(BF16)

        return [functools.partial(group, base) for base in range(0, D_IN, RET_WIDTH)]

    def retention_steps(src_ref, y_ref, state_in, state_out_refs):
        n_chunks = src_ref.shape[0] // CHUNK
        units = [(c, hd) for c in range(n_chunks) for hd in range(RET_HEADS)]
        scores, incs, outs, cell = {}, {}, {}, {}

        def col(c, hd, which):
            return src_ref[c * CHUNK:(c + 1) * CHUNK,
                           which * RET_WIDTH + hd * LANES:which * RET_WIDTH + (hd + 1) * LANES]

        def score(c, hd):
            s = lax.dot_general(col(c, hd, 0), col(c, hd, 1), nt, preferred_element_type=F32)
            scores[c, hd] = (s * dintra_ref[hd]).astype(BF16)

        def increment(c, hd):
            kd = (col(c, hd, 1).astype(F32) * kdec_ref[hd]).astype(BF16)
            incs[c, hd] = lax.dot_general(kd, col(c, hd, 2), tn, preferred_element_type=F32)

        def output(c, hd):
            if not cell:
                cell["state"] = state_in()
            state = cell["state"]
            if state is None:
                outs[c, hd] = jnp.dot(scores[c, hd], col(c, hd, 2), preferred_element_type=F32)
            else:
                qd = (col(c, hd, 0).astype(F32) * qdec_ref[hd]).astype(BF16)
                outs[c, hd] = jnp.dot(jnp.concatenate([scores[c, hd], qd], axis=1),
                                      jnp.concatenate([col(c, hd, 2), state[hd].astype(BF16)],
                                                      axis=0),
                                      preferred_element_type=F32)
            if hd == RET_HEADS - 1:
                if state is None:
                    cell["state"] = [incs[c, h] for h in range(RET_HEADS)]
                else:
                    cell["state"] = [state[h] * cd_ref[h] + incs[c, h] for h in range(RET_HEADS)]

        def store_state():
            for hd in range(RET_HEADS):
                for ref in state_out_refs:
                    ref[hd] = cell["state"][hd]

        def norm_gate(c, hd):
            o = outs[c, hd]
            mu = jnp.mean(o, axis=-1, keepdims=True)
            d = o - mu
            var = jnp.mean(d * d, axis=-1, keepdims=True)
            on = d * lax.rsqrt(var + EPS) * gnw_ref[layer:layer + 1, hd * LANES:(hd + 1) * LANES]
            y_ref[c * CHUNK:(c + 1) * CHUNK, hd * LANES:(hd + 1) * LANES] = (
                jax.nn.silu(col(c, hd, 3).astype(F32)) * on).astype(BF16)

        return ([functools.partial(stage, c, hd) for stage in (score, increment, output)
                 for c, hd in units] + [store_state]
                + [functools.partial(norm_gate, c, hd) for c, hd in units])

    def previous_tile_retention_steps():
        def state_in():
            first = (t + TILES_PER_BATCH - 1) % TILES_PER_BATCH == 0
            return [jnp.where(first, meta_state_ref[hd], state_ref[hd])
                    for hd in range(RET_HEADS)]

        return retention_steps(ring_ref.at[(t + 1) % 2], yf_ref, state_in, [state_ref])

    def emit(major, minor=()):
        per = -(-len(minor) // len(major))
        for k, step in enumerate(major):
            step()
            for extra in minor[k * per:(k + 1) * per]:
                extra()

    @pl.when(i < STAGE_STEPS)
    def _():
        _stage_weight(i, w_ref, w_s)

    @pl.when(i == STAGE_STEPS)
    def _():
        ring_ref[...] = jnp.zeros_like(ring_ref)
        state_ref[...] = jnp.zeros_like(state_ref)
        meta_ring = ring_ref.at[0, :CHUNK]
        emit(project_steps(hm_ref, tabm_ref, meta_ring, pm_ref))
        emit(retention_steps(meta_ring, ym_ref, lambda: None, [meta_state_ref]))

    @pl.when((t >= 0) & (t < N_PROJ_TILES))
    def _():
        steps = previous_tile_retention_steps()
        n_norm = (PROJ_TILE // CHUNK) * RET_HEADS
        emit(steps[:-n_norm])
        emit(project_steps(hf_ref, tabf_ref, ring_ref.at[t % 2], pf_ref), steps[-n_norm:])

    @pl.when(t == N_PROJ_TILES)
    def _():
        emit(previous_tile_retention_steps())


def _inproj_retention(layer, h, norm_g, w_in, tab_frames, tab_meta, cd, dintra, qdec, kdec, gn_w):
    steps = STAGE_STEPS + 1 + N_PROJ_TILES + 1
    tile = lambda i: jnp.clip(i - (STAGE_STEPS + 1), 0, N_PROJ_TILES - 1)
    lag = lambda i: jnp.clip(i - (STAGE_STEPS + 2), 0, N_PROJ_TILES - 1)
    const3 = _whole((RET_HEADS, CHUNK, LANES))
    outs = pl.pallas_call(
        functools.partial(_inproj_retention_kernel, layer=layer),
        grid=(steps,),
        in_specs=[pl.BlockSpec(memory_space=pltpu.SMEM),
                  pl.BlockSpec((PROJ_TILE, D_MODEL), lambda i: (tile(i), 0)),
                  _whole((CHUNK, D_MODEL)), _whole((DEPTH, D_MODEL)),
                  pl.BlockSpec((None, D_MODEL // STAGE_STEPS, D_IN),
                               lambda i: (layer, jnp.minimum(i, STAGE_STEPS - 1), 0)),
                  pl.BlockSpec((PROJ_TILE, N_TAB * LANES),
                               lambda i: (tile(i) % TILES_PER_BATCH, 0)),
                  _whole((CHUNK, N_TAB * LANES)),
                  const3, const3, const3, _whole((DEPTH, RET_WIDTH))],
        out_specs=[pl.BlockSpec((PROJ_TILE, DIFF_COLS), lambda i: (tile(i), 0)),
                   _whole((CHUNK, DIFF_COLS)),
                   pl.BlockSpec((PROJ_TILE, RET_WIDTH), lambda i: (lag(i), 0)),
                   _whole((CHUNK, RET_WIDTH))],
        out_shape=[jax.ShapeDtypeStruct((FRAME_ROWS, DIFF_COLS), BF16),
                   jax.ShapeDtypeStruct((CHUNK, DIFF_COLS), BF16),
                   jax.ShapeDtypeStruct((FRAME_ROWS, RET_WIDTH), BF16),
                   jax.ShapeDtypeStruct((CHUNK, RET_WIDTH), BF16)],
        scratch_shapes=[pltpu.VMEM((D_MODEL, D_IN), BF16),
                        pltpu.VMEM((2, PROJ_TILE, RET_COLS), BF16),
                        pltpu.VMEM((RET_HEADS, RET_DK, LANES), F32),
                        pltpu.VMEM((RET_HEADS, RET_DK, LANES), F32)],
        compiler_params=pltpu.CompilerParams(
            dimension_semantics=("arbitrary",), vmem_limit_bytes=VMEM_LIMIT),
        name="inproj_retention",
    )(cd, h[0], h[1], norm_g, w_in, tab_frames, tab_meta, dintra, qdec, kdec, gn_w)
    return (outs[0], outs[1]), (outs[2], outs[3])


_QUERY_BLOCKS = ((0, CHUNK),) + tuple((r, 2 * CHUNK) for r in range(CHUNK, T_PAD, 2 * CHUNK))


def _key_tiles(nk):
    return ((PAD, N_META),) + tuple((r, 2 * CHUNK) for r in range(CHUNK, nk, 2 * CHUNK))


def _diffattn_kernel(qf_ref, kf_ref, vf_ref, qm_ref, km_ref, vm_ref, w_ref, lq1_ref, lk1_ref,
                     lq2_ref, lk2_ref, of_ref, om_ref, s_ref, p_ref, *, layer, lambda_init):
    row = slice(layer, layer + 1)
    lam = (jnp.exp(jnp.sum(lq1_ref[row, :] * lk1_ref[row, :], axis=-1, keepdims=True))
           - jnp.exp(jnp.sum(lq2_ref[row, :] * lk2_ref[row, :], axis=-1, keepdims=True))
           + lambda_init)
    w_out = w_ref[row, :] * (1.0 - lambda_init)
    nt = (((1,), (1,)), ((), ()))
    tn = (((0,), (0,)), ((), ()))

    def rows(meta_ref, frames_ref, r0, n):
        if r0 < CHUNK:
            return meta_ref[r0:r0 + n, :]
        return frames_ref[0, r0 - CHUNK:r0 - CHUNK + n, :]

    def fold(state, key, value, op):
        state[key] = value if key not in state else op(state[key], value)

    def score_tiles(blk, slot, state):
        r0, nq = _QUERY_BLOCKS[blk]
        nk = r0 + nq
        qt = rows(qm_ref, qf_ref, r0, nq).astype(F32).T
        feat = lax.broadcasted_iota(jnp.int32, (LANES, nq), 0)
        qq = jnp.concatenate([jnp.where(feat < DIFF_D, qt, 0.0),
                              jnp.where(feat >= DIFF_D, qt, 0.0)], axis=1).astype(BF16)

        def tile(t0, tk):
            s = jnp.dot(rows(km_ref, kf_ref, t0, tk), qq, preferred_element_type=F32)
            krow = lax.broadcasted_iota(jnp.int32, (tk, 2 * nq), 0)
            if t0 + tk > r0:
                qcol = lax.broadcasted_iota(jnp.int32, (tk, 2 * nq), 1) % nq
                s = jnp.where(krow + (t0 - r0) <= qcol, s, NEG)
            s_ref[slot, t0:t0 + tk, :2 * nq] = s
            fold(state, "max", jnp.max(s, axis=0, keepdims=True), jnp.maximum)

        return [functools.partial(tile, t0, tk) for t0, tk in _key_tiles(nk)]

    def prob_tiles(blk, slot, smax, state):
        r0, nq = _QUERY_BLOCKS[blk]
        nk = r0 + nq

        def tile(t0, tk):
            p = jnp.exp2(s_ref[slot, t0:t0 + tk, :2 * nq] - smax)
            fold(state, "pos", jnp.sum(p[:, :nq], axis=0, keepdims=True), jnp.add)
            fold(state, "neg", jnp.sum(p[:, nq:], axis=0, keepdims=True), jnp.add)
            p_ref[slot, t0:t0 + tk, :2 * nq] = p.astype(BF16)

        return [functools.partial(tile, t0, tk) for t0, tk in _key_tiles(nk)]

    def finish(blk, slot, sums):
        r0, nq = _QUERY_BLOCKS[blk]
        nk = r0 + nq
        ot = lax.dot_general(vm_ref[PAD:, :], p_ref[slot, PAD:CHUNK, :2 * nq], tn,
                             preferred_element_type=F32)
        if nk > CHUNK:
            ot = ot + lax.dot_general(vf_ref[0, :nk - CHUNK, :], p_ref[slot, CHUNK:nk, :2 * nq], tn,
                                      preferred_element_type=F32)
        o = (ot[:, :nq] * (1.0 / sums["pos"]) - ot[:, nq:] * (lam / sums["neg"])).T
        o = o * lax.rsqrt(jnp.mean(o * o, axis=-1, keepdims=True) + EPS) * w_out
        if r0 == 0:
            orow = lax.broadcasted_iota(jnp.int32, (nq, LANES), 0)
            om_ref[...] = jnp.where(orow >= PAD, o, 0.0).astype(BF16)
        else:
            of_ref[0, r0 - CHUNK:nk - CHUNK, :] = o.astype(BF16)

    n_blocks = len(_QUERY_BLOCKS)
    order = list(range(0, n_blocks, 2)) + list(range(n_blocks - 1 - n_blocks % 2, 0, -2))
    score_state, pending = {}, None
    for run in score_tiles(order[0], 0, score_state):
        run()
    for r, blk in enumerate(order):
        smax, sums, score_state = score_state["max"], {}, {}
        ahead = score_tiles(order[r + 1], (r + 1) % 2, score_state) if r + 1 < n_blocks else []
        current = prob_tiles(blk, r % 2, smax, sums)
        for n in range(max(len(ahead), len(current))):
            for runs in (current, ahead):
                if n < len(runs):
                    runs[n]()
            if n == 0 and pending is not None:
                finish(*pending)
        pending = (blk, r % 2, sums)
    finish(*pending)


def _diffattn(layer, proj, subln_w, lq1, lk1, lq2, lk2):
    proj_frames, proj_meta = proj
    frames3 = proj_frames.reshape(BATCH, SEQ, DIFF_COLS)
    lambda_init = 0.8 - 0.6 * math.exp(-0.3 * layer)
    fblk = lambda c: pl.BlockSpec((1, SEQ, LANES), lambda hd, b, c=c: (b, 0, c + hd))
    mblk = lambda c: pl.BlockSpec((CHUNK, LANES), lambda hd, b, c=c: (0, c + hd))
    small = lambda n: _whole((DEPTH, n))
    y_frames, y_meta = pl.pallas_call(
        functools.partial(_diffattn_kernel, layer=layer, lambda_init=lambda_init),
        grid=(DIFF_HEADS, BATCH),
        in_specs=[fblk(0), fblk(DIFF_HEADS), fblk(2 * DIFF_HEADS),
                  mblk(0), mblk(DIFF_HEADS), mblk(2 * DIFF_HEADS),
                  small(DIFF_DV), small(DIFF_D), small(DIFF_D), small(DIFF_D), small(DIFF_D)],
        out_specs=[pl.BlockSpec((1, SEQ, LANES), lambda hd, b: (b, 0, hd)),
                   pl.BlockSpec((CHUNK, LANES), lambda hd, b: (0, hd))],
        out_shape=[jax.ShapeDtypeStruct((BATCH, SEQ, DIFF_WIDTH), BF16),
                   jax.ShapeDtypeStruct((CHUNK, DIFF_WIDTH), BF16)],
        scratch_shapes=[pltpu.VMEM((2, T_PAD, 4 * CHUNK), F32),
                        pltpu.VMEM((2, T_PAD, 4 * CHUNK), BF16)],
        compiler_params=pltpu.CompilerParams(
            dimension_semantics=("arbitrary", "arbitrary"), vmem_limit_bytes=VMEM_LIMIT),
        name="diffattn",
    )(frames3, frames3, frames3, proj_meta, proj_meta, proj_meta, subln_w, lq1, lk1, lq2, lk2)
    return y_frames.reshape(FRAME_ROWS, DIFF_WIDTH), y_meta


def _rotary_tables():
    f32 = np.float32
    pos = np.arange(T_PAD, dtype=f32) - f32(PAD)
    angle = (f32(RET_THETA) ** (-np.linspace(0.0, 1.0, RET_DK // 2, dtype=f32))).astype(f32)
    fr = pos[:, None] * angle[None, :]
    c, s = np.cos(fr), np.sin(fr)
    zero = np.zeros_like(s)
    cos_i = np.repeat(c, 2, axis=-1)
    sin_even = np.stack([-s, zero], axis=-1).reshape(T_PAD, RET_DK)
    sin_odd = np.stack([zero, s], axis=-1).reshape(T_PAD, RET_DK)
    ks = f32(RET_DK ** -0.5)
    inv = (f32(ROPE_THETA) ** (-np.arange(0, ROPE_DIMS, 2, dtype=f32) / f32(ROPE_DIMS))).astype(f32)
    fq = pos[:, None] * inv[None, :]
    emb = np.concatenate([fq, fq], axis=-1)
    ce, se = np.cos(emb), np.sin(emb)
    half, rest = ROPE_DIMS // 2, DIFF_D - ROPE_DIMS
    zeros = lambda n: np.zeros((T_PAD, n), f32)
    two = lambda a: np.concatenate([a, a], axis=-1)
    c_d = two(np.concatenate([ce, np.ones((T_PAD, rest), f32)], axis=-1))
    s_lo = two(np.concatenate([zeros(half), se[:, half:], zeros(rest)], axis=-1))
    s_hi = two(np.concatenate([-se[:, :half], zeros(half + rest)], axis=-1))
    qs = f32((DIFF_D ** -0.5) * math.log2(math.e))
    tabs = [cos_i, sin_even, sin_odd, cos_i * ks, sin_even * ks, sin_odd * ks,
            c_d * qs, s_lo * qs, s_hi * qs, c_d, s_lo, s_hi]
    return np.concatenate(tabs, axis=-1).astype(f32)


def _retention_consts():
    f32 = np.float32
    log_gamma = np.log(f32(1.0) - f32(2.0) ** (f32(-5.0) - np.arange(RET_HEADS, dtype=f32))).astype(f32)
    idx = np.arange(CHUNK, dtype=f32)
    rel = idx[:, None] - idx[None, :]
    dintra = np.where(rel >= 0, np.exp(log_gamma[:, None, None] * np.maximum(rel, f32(0.0))), f32(0.0))
    k_decay = np.exp(log_gamma[:, None] * (CHUNK - 1 - idx)[None, :])
    q_decay = np.exp(log_gamma[:, None] * (idx + f32(1.0))[None, :])
    cd = np.exp(log_gamma * f32(CHUNK))
    bc = lambda a: np.ascontiguousarray(np.broadcast_to(a[:, :, None], (RET_HEADS, CHUNK, LANES)))
    return cd.astype(f32), dintra.astype(f32), bc(q_decay).astype(f32), bc(k_decay).astype(f32)


def kernel(x, meta_tokens, ffn1_norm, ffn1_w_gate, ffn1_w_up, ffn1_w_down, mix_norm, w_in, ret_gn_w, diff_subln_w, diff_lambda_q1, diff_lambda_k1, diff_lambda_q2, diff_lambda_k2, w_out, ffn2_norm, ffn2_w_gate, ffn2_w_up, ffn2_w_down, final_norm):
    meta_chunk = jnp.concatenate([jnp.zeros((PAD, D_MODEL), F32), meta_tokens.astype(F32)], axis=0)
    h = (x.astype(F32).reshape(FRAME_ROWS, D_MODEL), meta_chunk)

    tab = _rotary_tables()
    tab_meta, tab_frames = jnp.asarray(tab[:CHUNK]), jnp.asarray(tab[CHUNK:])
    cd, dintra, qdec, kdec = (jnp.asarray(a) for a in _retention_consts())

    for l in range(DEPTH):
        last = l == DEPTH - 1
        h = _ffn(l, h, ffn1_norm, ffn1_w_gate, ffn1_w_up, ffn1_w_down)
        proj_diff, y_ret = _inproj_retention(l, h, mix_norm, w_in, tab_frames, tab_meta,
                                             cd, dintra, qdec, kdec, ret_gn_w)
        y_diff = _diffattn(l, proj_diff, diff_subln_w, diff_lambda_q1, diff_lambda_k1,
                           diff_lambda_q2, diff_lambda_k2)
        h = _ffn(l, h, ffn2_norm, ffn2_w_gate, ffn2_w_up, ffn2_w_down,
                 mix=(y_ret, y_diff, w_out),
                 final_g=final_norm.reshape(1, D_MODEL) if last else None)

    return h[0].reshape(BATCH, SEQ, D_MODEL).astype(x.dtype)
```

```python
import functools
import math

import numpy as np
import jax
import jax.numpy as jnp
from jax import lax
from jax.experimental import pallas as pl
from jax.experimental.pallas import tpu as pltpu

D_MODEL = 1024
BATCH = 8
SEQ = 2048
DEPTH = 2
N_META = 16
CHUNK = 128
RET_HEADS = 4
RET_DK = 128
RET_WIDTH = 512
RET_THETA = 10000.0
DIFF_HEADS = 4
DIFF_D = 64
DIFF_DV = 128
DIFF_WIDTH = 512
ROPE_THETA = 500000.0
ROPE_DIMS = 16
D_FF = 2816
EPS = 1e-6
D_IN = 3584

PAD = CHUNK - N_META
T_PAD = CHUNK + SEQ
FRAME_ROWS = BATCH * SEQ
N_TAB = 8
LANES = 128
VMEM_LIMIT = 56 * 1024 * 1024
NEG = -1e30

F32 = jnp.float32
BF16 = jnp.bfloat16


def _rms(x, g):
    return x * lax.rsqrt(jnp.mean(x * x, axis=-1, keepdims=True) + EPS) * g


def _whole(shape):
    return pl.BlockSpec(shape, lambda *_: (0,) * len(shape))


class _DenseGrid:
    def __init__(self, tile, stage, meta):
        self.tile, self.stage, self.meta = tile, stage, meta
        self.n_tiles = FRAME_ROWS // tile
        self.steps = stage + self.n_tiles + int(meta)

    def tile_index(self, i):
        return jnp.clip(i - self.stage, 0, self.n_tiles - 1)

    def row_tile(self, width):
        return pl.BlockSpec((self.tile, width), lambda i: (self.tile_index(i), 0))

    def weight_chunk(self, layer, rows, cols):
        return pl.BlockSpec((None, rows // self.stage, cols),
                            lambda i: (layer, jnp.minimum(i, self.stage - 1), 0))

    def run(self, i, stage, frames, meta):
        pl.when(i < self.stage)(stage)
        pl.when((i >= self.stage) & (i < self.stage + self.n_tiles))(frames)
        if self.meta:
            pl.when(i == self.stage + self.n_tiles)(meta)


def _stage_weight(i, src_ref, dst_ref):
    rows = src_ref.shape[0]
    dst_ref[pl.ds(pl.multiple_of(i * rows, 16), rows), :] = src_ref[...].astype(BF16)


FFN_TILE = 512
FFN_SUB_TILE = 256
STAGE_STEPS = 8
PROJ_TILE = 1024


def _ffn_kernel(*refs, grid, layer, mix, final):
    refs = list(refs)
    take = lambda n: [refs.pop(0) for _ in range(n)]
    with_meta = not final
    h_refs = take(2 if with_meta else 1)
    if mix:
        yr_refs, yd_refs = take(len(h_refs)), take(len(h_refs))
        (wo_ref,) = take(1)
    g_ref, wg_ref, wu_ref, wd_ref = take(4)
    if final:
        (fn_ref,) = take(1)
    o_refs = take(len(h_refs))
    wg_s, wu_s, wd_s = take(3)
    if mix:
        (wo_s,) = take(1)
    i = pl.program_id(0)

    def stage():
        _stage_weight(i, wg_ref, wg_s)
        _stage_weight(i, wu_ref, wu_s)
        _stage_weight(i, wd_ref, wd_s)
        if mix:
            _stage_weight(i, wo_ref, wo_s)

    def rows(which):
        n = h_refs[which].shape[0]
        parts = [slice(r, min(r + FFN_SUB_TILE, n)) for r in range(0, n, FFN_SUB_TILE)]
        xs, xns, acts = [], [], []
        for sl in parts:
            x = h_refs[which][sl, :]
            if mix:
                x = (x + jnp.dot(yr_refs[which][sl, :], wo_s[:RET_WIDTH, :],
                                 preferred_element_type=F32)
                     + jnp.dot(yd_refs[which][sl, :], wo_s[RET_WIDTH:, :],
                               preferred_element_type=F32))
            xs.append(x)
            xns.append(_rms(x, g_ref[layer:layer + 1, :]).astype(BF16))
        for xn in xns:
            gate = jnp.dot(xn, wg_s[...], preferred_element_type=F32)
            up = jnp.dot(xn, wu_s[...], preferred_element_type=F32)
            acts.append((jax.nn.silu(gate) * up).astype(BF16))
        for sl, x, act in zip(parts, xs, acts):
            y = x + 0.5 * jnp.dot(act, wd_s[...], preferred_element_type=F32)
            if final:
                y = _rms(y, fn_ref[...])
            o_refs[which][sl, :] = y

    grid.run(i, stage, functools.partial(rows, 0), functools.partial(rows, 1))


def _ffn(layer, h, norm_g, wg, wu, wd, mix=None, final_g=None):
    with_meta = final_g is None
    grid = _DenseGrid(tile=FFN_TILE, stage=STAGE_STEPS, meta=with_meta)
    _row_tile, _weight_chunk = grid.row_tile, grid.weight_chunk
    pair = lambda width: [_row_tile(width)] + ([_whole((CHUNK, width))] if with_meta else [])
    keep = lambda arrays: list(arrays) if with_meta else [arrays[0]]
    args, specs = keep(h), pair(D_MODEL)
    scratch = [pltpu.VMEM((D_MODEL, D_FF), BF16), pltpu.VMEM((D_MODEL, D_FF), BF16),
               pltpu.VMEM((D_FF, D_MODEL), BF16)]
    if mix is not None:
        y_ret, y_diff, w_out = mix
        args += keep(y_ret) + keep(y_diff) + [w_out]
        specs += pair(RET_WIDTH) + pair(DIFF_WIDTH) + [_weight_chunk(layer, D_MODEL, D_MODEL)]
        scratch.append(pltpu.VMEM((D_MODEL, D_MODEL), BF16))
    args += [norm_g, wg, wu, wd]
    specs += [_whole((DEPTH, D_MODEL)), _weight_chunk(layer, D_MODEL, D_FF),
              _weight_chunk(layer, D_MODEL, D_FF), _weight_chunk(layer, D_FF, D_MODEL)]
    if final_g is not None:
        args.append(final_g)
        specs.append(_whole((1, D_MODEL)))
    out_shape = [jax.ShapeDtypeStruct((FRAME_ROWS, D_MODEL), F32)]
    if with_meta:
        out_shape.append(jax.ShapeDtypeStruct((CHUNK, D_MODEL), F32))
    return pl.pallas_call(
        functools.partial(_ffn_kernel, grid=grid, layer=layer, mix=mix is not None,
                          final=final_g is not None),
        grid=(grid.steps,),
        in_specs=specs,
        out_specs=pair(D_MODEL),
        out_shape=out_shape,
        scratch_shapes=scratch,
        compiler_params=pltpu.CompilerParams(
            dimension_semantics=("arbitrary",), vmem_limit_bytes=VMEM_LIMIT),
        name="ffn",
    )(*args)


RET_ORDER = np.concatenate([np.arange(0, RET_DK, 2), np.arange(1, RET_DK, 2)])
HALF = ROPE_DIMS // 2
DIFF_ORDER = np.concatenate([np.arange(HALF), DIFF_D + np.arange(HALF),
                             np.arange(ROPE_DIMS, DIFF_D),
                             HALF + np.arange(HALF), DIFF_D + HALF + np.arange(HALF),
                             DIFF_D + np.arange(ROPE_DIMS, DIFF_D)])
_lanes = np.arange(LANES)
assert np.array_equal(DIFF_ORDER < DIFF_D,
                      (_lanes < HALF) | ((_lanes >= ROPE_DIMS) & (_lanes < DIFF_D + HALF)))
_ROTATIONS = (
    (0, 0, 0),
    (RET_WIDTH, 2, 0),
    (4 * RET_WIDTH, 4, 1),
    (4 * RET_WIDTH + DIFF_WIDTH, 6, 1),
)


RET_COLS = 4 * RET_WIDTH
DIFF_COLS = D_IN - RET_COLS
N_PROJ_TILES = FRAME_ROWS // PROJ_TILE
TILES_PER_BATCH = SEQ // PROJ_TILE


def _inproj_retention_kernel(cd_ref, hf_ref, hm_ref, g_ref, w_ref, perm_ref, tabf_ref, tabm_ref,
                             dintra_ref, qdec_ref, kdec_ref, gnw_ref,
                             pf_ref, pm_ref, yf_ref, ym_ref,
                             w_s, ring_ref, state_ref, meta_state_ref, *, layer):
    nt = (((1,), (1,)), ((), ()))
    tn = (((0,), (0,)), ((), ()))
    i = pl.program_id(0)
    t = i - (STAGE_STEPS + 1)

    def project_steps(h_ref, tab_ref, ret_ref, diff_ref):
        rotation = {base: rest for base, *rest in _ROTATIONS}
        cell = {}

        def tab(k):
            return tab_ref[:, k * LANES:(k + 1) * LANES]

        def group(base):
            if not cell:
                cell["xn"] = _rms(h_ref[...], g_ref[layer:layer + 1, :]).astype(BF16)
            p = jnp.dot(cell["xn"], w_s[:, base:base + RET_WIDTH], preferred_element_type=F32)
            dst, lo = (ret_ref, base) if base < RET_COLS else (diff_ref, base - RET_COLS)
            if base in rotation:
                t0, _ = rotation[base]
                for hd in range(RET_HEADS):
                    xs = p[:, hd * LANES:(hd + 1) * LANES]
                    rot = xs * tab(t0) + pltpu.roll(xs, LANES // 2, 1) * tab(t0 + 1)
                    dst[:, lo + hd * LANES:lo + (hd + 1) * LANES] = rot.astype(BF16)
            else:
                dst[:, lo:lo + RET_WIDTH] = p.astype(BF16)

        return [functools.partial(group, base) for base in range(0, D_IN, RET_WIDTH)]

    def retention_steps(src_ref, y_ref, state_in, state_out_refs):
        n_chunks = src_ref.shape[0] // CHUNK
        units = [(c, hd) for c in range(n_chunks) for hd in range(RET_HEADS)]
        scores, incs, outs, cell = {}, {}, {}, {}

        def col(c, hd, which):
            return src_ref[c * CHUNK:(c + 1) * CHUNK,
                           which * RET_WIDTH + hd * LANES:which * RET_WIDTH + (hd + 1) * LANES]

        def score(c, hd):
            s = lax.dot_general(col(c, hd, 0), col(c, hd, 1), nt, preferred_element_type=F32)
            scores[c, hd] = (s * dintra_ref[hd]).astype(BF16)

        def increment(c, hd):
            kd = (col(c, hd, 1).astype(F32) * kdec_ref[hd]).astype(BF16)
            incs[c, hd] = lax.dot_general(kd, col(c, hd, 2), tn, preferred_element_type=F32)

        def output(c, hd):
            if not cell:
                cell["state"] = state_in()
            state = cell["state"]
            if state is None:
                outs[c, hd] = jnp.dot(scores[c, hd], col(c, hd, 2), preferred_element_type=F32)
            else:
                qd = (col(c, hd, 0).astype(F32) * qdec_ref[hd]).astype(BF16)
                outs[c, hd] = jnp.dot(jnp.concatenate([scores[c, hd], qd], axis=1),
                                      jnp.concatenate([col(c, hd, 2), state[hd].astype(BF16)],
                                                      axis=0),
                                      preferred_element_type=F32)
            if hd == RET_HEADS - 1:
                if state is None:
                    cell["state"] = [incs[c, h] for h in range(RET_HEADS)]
                else:
                    cell["state"] = [state[h] * cd_ref[h] + incs[c, h] for h in range(RET_HEADS)]

        def store_state():
            for hd in range(RET_HEADS):
                for ref in state_out_refs:
                    ref[hd] = cell["state"][hd]

        def norm_gate(c, hd):
            o = outs[c, hd]
            mu = jnp.mean(o, axis=-1, keepdims=True)
            d = o - mu
            var = jnp.mean(d * d, axis=-1, keepdims=True)
            on = d * lax.rsqrt(var + EPS) * gnw_ref[layer:layer + 1, hd * LANES:(hd + 1) * LANES]
            y_ref[c * CHUNK:(c + 1) * CHUNK, hd * LANES:(hd + 1) * LANES] = (
                jax.nn.silu(col(c, hd, 3).astype(F32)) * on).astype(BF16)

        return ([functools.partial(stage, c, hd) for stage in (score, increment, output)
                 for c, hd in units] + [store_state]
                + [functools.partial(norm_gate, c, hd) for c, hd in units])

    def previous_tile_retention_steps():
        def state_in():
            first = (t + TILES_PER_BATCH - 1) % TILES_PER_BATCH == 0
            return [jnp.where(first, meta_state_ref[hd], state_ref[hd])
                    for hd in range(RET_HEADS)]

        return retention_steps(ring_ref.at[(t + 1) % 2], yf_ref, state_in, [state_ref])

    def emit(major, minor=()):
        per = -(-len(minor) // len(major))
        for k, step in enumerate(major):
            step()
            for extra in minor[k * per:(k + 1) * per]:
                extra()

    @pl.when(i < STAGE_STEPS)
    def _():
        rows = w_ref.shape[0]
        dst = w_s.at[pl.ds(pl.multiple_of(i * rows, 16), rows)]
        rotation = {base: which for base, _, which in _ROTATIONS}
        for base in range(0, D_IN, RET_WIDTH):
            chunk = w_ref[:, base:base + RET_WIDTH].astype(BF16)
            if base in rotation:
                perm = perm_ref[rotation[base]]
                chunk = jnp.concatenate(
                    [jnp.dot(chunk[:, hd * LANES:(hd + 1) * LANES], perm,
                             preferred_element_type=F32).astype(BF16)
                     for hd in range(RET_HEADS)], axis=1)
            dst[:, base:base + RET_WIDTH] = chunk

    @pl.when(i == STAGE_STEPS)
    def _():
        ring_ref[...] = jnp.zeros_like(ring_ref)
        state_ref[...] = jnp.zeros_like(state_ref)
        meta_ring = ring_ref.at[0, :CHUNK]
        emit(project_steps(hm_ref, tabm_ref, meta_ring, pm_ref))
        emit(retention_steps(meta_ring, ym_ref, lambda: None, [meta_state_ref]))

    @pl.when((t >= 0) & (t < N_PROJ_TILES))
    def _():
        steps = previous_tile_retention_steps()
        n_norm = (PROJ_TILE // CHUNK) * RET_HEADS
        emit(steps[:-n_norm])
        emit(project_steps(hf_ref, tabf_ref, ring_ref.at[t % 2], pf_ref), steps[-n_norm:])

    @pl.when(t == N_PROJ_TILES)
    def _():
        emit(previous_tile_retention_steps())


def _inproj_retention(layer, h, norm_g, w_in, perms, tab_frames, tab_meta, cd, dintra, qdec, kdec,
                      gn_w):
    steps = STAGE_STEPS + 1 + N_PROJ_TILES + 1
    tile = lambda i: jnp.clip(i - (STAGE_STEPS + 1), 0, N_PROJ_TILES - 1)
    lag = lambda i: jnp.clip(i - (STAGE_STEPS + 2), 0, N_PROJ_TILES - 1)
    const3 = _whole((RET_HEADS, CHUNK, LANES))
    outs = pl.pallas_call(
        functools.partial(_inproj_retention_kernel, layer=layer),
        grid=(steps,),
        in_specs=[pl.BlockSpec(memory_space=pltpu.SMEM),
                  pl.BlockSpec((PROJ_TILE, D_MODEL), lambda i: (tile(i), 0)),
                  _whole((CHUNK, D_MODEL)), _whole((DEPTH, D_MODEL)),
                  pl.BlockSpec((None, D_MODEL // STAGE_STEPS, D_IN),
                               lambda i: (layer, jnp.minimum(i, STAGE_STEPS - 1), 0)),
                  _whole((2, LANES, LANES)),
                  pl.BlockSpec((PROJ_TILE, N_TAB * LANES),
                               lambda i: (tile(i) % TILES_PER_BATCH, 0)),
                  _whole((CHUNK, N_TAB * LANES)),
                  const3, const3, const3, _whole((DEPTH, RET_WIDTH))],
        out_specs=[pl.BlockSpec((PROJ_TILE, DIFF_COLS), lambda i: (tile(i), 0)),
                   _whole((CHUNK, DIFF_COLS)),
                   pl.BlockSpec((PROJ_TILE, RET_WIDTH), lambda i: (lag(i), 0)),
                   _whole((CHUNK, RET_WIDTH))],
        out_shape=[jax.ShapeDtypeStruct((FRAME_ROWS, DIFF_COLS), BF16),
                   jax.ShapeDtypeStruct((CHUNK, DIFF_COLS), BF16),
                   jax.ShapeDtypeStruct((FRAME_ROWS, RET_WIDTH), BF16),
                   jax.ShapeDtypeStruct((CHUNK, RET_WIDTH), BF16)],
        scratch_shapes=[pltpu.VMEM((D_MODEL, D_IN), BF16),
                        pltpu.VMEM((2, PROJ_TILE, RET_COLS), BF16),
                        pltpu.VMEM((RET_HEADS, RET_DK, LANES), F32),
                        pltpu.VMEM((RET_HEADS, RET_DK, LANES), F32)],
        compiler_params=pltpu.CompilerParams(
            dimension_semantics=("arbitrary",), vmem_limit_bytes=VMEM_LIMIT),
        name="inproj_retention",
    )(cd, h[0], h[1], norm_g, w_in, perms, tab_frames, tab_meta, dintra, qdec, kdec, gn_w)
    return (outs[0], outs[1]), (outs[2], outs[3])


_QUERY_BLOCKS = ((0, CHUNK),) + tuple((r, 2 * CHUNK) for r in range(CHUNK, T_PAD, 2 * CHUNK))


def _key_tiles(nk):
    return ((PAD, N_META),) + tuple((r, 2 * CHUNK) for r in range(CHUNK, nk, 2 * CHUNK))


def _diffattn_kernel(qf_ref, kf_ref, vf_ref, qm_ref, km_ref, vm_ref, w_ref, lq1_ref, lk1_ref,
                     lq2_ref, lk2_ref, of_ref, om_ref, s_ref, p_ref, vt_ref, *, layer, lambda_init):
    row = slice(layer, layer + 1)
    lam = (jnp.exp(jnp.sum(lq1_ref[row, :] * lk1_ref[row, :], axis=-1, keepdims=True))
           - jnp.exp(jnp.sum(lq2_ref[row, :] * lk2_ref[row, :], axis=-1, keepdims=True))
           + lambda_init)
    w_out = w_ref[row, :] * (1.0 - lambda_init)
    nt = (((1,), (1,)), ((), ()))
    tn = (((0,), (0,)), ((), ()))

    def rows(meta_ref, frames_ref, r0, n):
        if r0 < CHUNK:
            return meta_ref[r0:r0 + n, :]
        return frames_ref[0, r0 - CHUNK:r0 - CHUNK + n, :]

    def fold(state, key, value, op):
        state[key] = value if key not in state else op(state[key], value)

    def score_tiles(blk, slot, state):
        r0, nq = _QUERY_BLOCKS[blk]
        nk = r0 + nq
        qt = rows(qm_ref, qf_ref, r0, nq).astype(F32).T
        feat = lax.broadcasted_iota(jnp.int32, (LANES, nq), 0)
        map0 = (feat < HALF) | ((feat >= ROPE_DIMS) & (feat < DIFF_D + HALF))
        qq = jnp.concatenate([jnp.where(map0, qt, 0.0),
                              jnp.where(map0, 0.0, qt)], axis=1).astype(BF16)

        def tile(t0, tk):
            s = jnp.dot(rows(km_ref, kf_ref, t0, tk), qq, preferred_element_type=F32)
            krow = lax.broadcasted_iota(jnp.int32, (tk, 2 * nq), 0)
            if t0 + tk > r0:
                qcol = lax.broadcasted_iota(jnp.int32, (tk, 2 * nq), 1) % nq
                s = jnp.where(krow + (t0 - r0) <= qcol, s, NEG)
            s_ref[slot, t0:t0 + tk, :2 * nq] = s
            fold(state, "max", jnp.max(s, axis=0, keepdims=True), jnp.maximum)

        return [functools.partial(tile, t0, tk) for t0, tk in _key_tiles(nk)]

    def prob_tiles(blk, slot, smax, state):
        r0, nq = _QUERY_BLOCKS[blk]
        nk = r0 + nq

        def tile(t0, tk):
            p = jnp.exp2(s_ref[slot, t0:t0 + tk, :2 * nq] - smax)
            fold(state, "pos", jnp.sum(p[:, :nq], axis=0, keepdims=True), jnp.add)
            fold(state, "neg", jnp.sum(p[:, nq:], axis=0, keepdims=True), jnp.add)
            p_ref[slot, t0:t0 + tk, :2 * nq] = p.astype(BF16)

        return [functools.partial(tile, t0, tk) for t0, tk in _key_tiles(nk)]

    def finish(blk, slot, sums):
        r0, nq = _QUERY_BLOCKS[blk]
        nk = r0 + nq
        ot = lax.dot_general(vm_ref[PAD:, :], p_ref[slot, PAD:CHUNK, :2 * nq], tn,
                             preferred_element_type=F32)
        if nk > CHUNK:
            ot = ot + jnp.dot(vt_ref[:, :nk - CHUNK], p_ref[slot, CHUNK:nk, :2 * nq],
                              preferred_element_type=F32)
        o = (ot[:, :nq] * (1.0 / sums["pos"]) - ot[:, nq:] * (lam / sums["neg"])).T
        o = o * lax.rsqrt(jnp.mean(o * o, axis=-1, keepdims=True) + EPS) * w_out
        if r0 == 0:
            orow = lax.broadcasted_iota(jnp.int32, (nq, LANES), 0)
            om_ref[...] = jnp.where(orow >= PAD, o, 0.0).astype(BF16)
        else:
            of_ref[0, r0 - CHUNK:nk - CHUNK, :] = o.astype(BF16)

    n_blocks = len(_QUERY_BLOCKS)
    order = list(range(0, n_blocks, 2)) + list(range(n_blocks - 1 - n_blocks % 2, 0, -2))
    vt_ref[...] = vf_ref[0].astype(F32).T.astype(BF16)
    score_state, pending = {}, None
    for run in score_tiles(order[0], 0, score_state):
        run()
    for r, blk in enumerate(order):
        smax, sums, score_state = score_state["max"], {}, {}
        ahead = score_tiles(order[r + 1], (r + 1) % 2, score_state) if r + 1 < n_blocks else []
        current = prob_tiles(blk, r % 2, smax, sums)
        for n in range(max(len(ahead), len(current))):
            for runs in (current, ahead):
                if n < len(runs):
                    runs[n]()
            if n == 0 and pending is not None:
                finish(*pending)
        pending = (blk, r % 2, sums)
    finish(*pending)


def _diffattn(layer, proj, subln_w, lq1, lk1, lq2, lk2):
    proj_frames, proj_meta = proj
    frames3 = proj_frames.reshape(BATCH, SEQ, DIFF_COLS)
    lambda_init = 0.8 - 0.6 * math.exp(-0.3 * layer)
    fblk = lambda c: pl.BlockSpec((1, SEQ, LANES), lambda hd, b, c=c: (b, 0, c + hd))
    mblk = lambda c: pl.BlockSpec((CHUNK, LANES), lambda hd, b, c=c: (0, c + hd))
    small = lambda n: _whole((DEPTH, n))
    y_frames, y_meta = pl.pallas_call(
        functools.partial(_diffattn_kernel, layer=layer, lambda_init=lambda_init),
        grid=(DIFF_HEADS, BATCH),
        in_specs=[fblk(0), fblk(DIFF_HEADS), fblk(2 * DIFF_HEADS),
                  mblk(0), mblk(DIFF_HEADS), mblk(2 * DIFF_HEADS),
                  small(DIFF_DV), small(DIFF_D), small(DIFF_D), small(DIFF_D), small(DIFF_D)],
        out_specs=[pl.BlockSpec((1, SEQ, LANES), lambda hd, b: (b, 0, hd)),
                   pl.BlockSpec((CHUNK, LANES), lambda hd, b: (0, hd))],
        out_shape=[jax.ShapeDtypeStruct((BATCH, SEQ, DIFF_WIDTH), BF16),
                   jax.ShapeDtypeStruct((CHUNK, DIFF_WIDTH), BF16)],
        scratch_shapes=[pltpu.VMEM((2, T_PAD, 4 * CHUNK), F32),
                        pltpu.VMEM((2, T_PAD, 4 * CHUNK), BF16),
                        pltpu.VMEM((DIFF_DV, SEQ), BF16)],
        compiler_params=pltpu.CompilerParams(
            dimension_semantics=("arbitrary", "arbitrary"), vmem_limit_bytes=VMEM_LIMIT),
        name="diffattn",
    )(frames3, frames3, frames3, proj_meta, proj_meta, proj_meta, subln_w, lq1, lk1, lq2, lk2)
    return y_frames.reshape(FRAME_ROWS, DIFF_WIDTH), y_meta


def _rotary_tables():
    f32 = np.float32
    pos = np.arange(T_PAD, dtype=f32) - f32(PAD)
    angle = (f32(RET_THETA) ** (-np.linspace(0.0, 1.0, RET_DK // 2, dtype=f32))).astype(f32)
    fr = pos[:, None] * angle[None, :]
    c, s = np.cos(fr), np.sin(fr)
    cos_r = np.concatenate([c, c], axis=-1)
    sin_r = np.concatenate([-s, s], axis=-1)
    ks = f32(RET_DK ** -0.5)
    inv = (f32(ROPE_THETA) ** (-np.arange(0, ROPE_DIMS, 2, dtype=f32) / f32(ROPE_DIMS))).astype(f32)
    fq = pos[:, None] * inv[None, :]
    emb = np.concatenate([fq, fq], axis=-1)
    ce, se = np.cos(emb), np.sin(emb)
    half, rest = ROPE_DIMS // 2, DIFF_D - ROPE_DIMS
    zeros = lambda n: np.zeros((T_PAD, n), f32)
    two = lambda a: np.concatenate([a, a], axis=-1)
    c_d = two(np.concatenate([ce, np.ones((T_PAD, rest), f32)], axis=-1))[:, DIFF_ORDER]
    s_d = two(np.concatenate([-se[:, :half], se[:, half:], zeros(rest)], axis=-1))[:, DIFF_ORDER]
    qs = f32((DIFF_D ** -0.5) * math.log2(math.e))
    tabs = [cos_r, sin_r, cos_r * ks, sin_r * ks, c_d * qs, s_d * qs, c_d, s_d]
    return np.concatenate(tabs, axis=-1).astype(f32)


def _column_permutations():
    perms = np.zeros((2, LANES, LANES), np.float32)
    for which, order in enumerate((RET_ORDER, DIFF_ORDER)):
        perms[which, order, np.arange(LANES)] = 1.0
    return perms


def _retention_consts():
    f32 = np.float32
    log_gamma = np.log(f32(1.0) - f32(2.0) ** (f32(-5.0) - np.arange(RET_HEADS, dtype=f32))).astype(f32)
    idx = np.arange(CHUNK, dtype=f32)
    rel = idx[:, None] - idx[None, :]
    dintra = np.where(rel >= 0, np.exp(log_gamma[:, None, None] * np.maximum(rel, f32(0.0))), f32(0.0))
    k_decay = np.exp(log_gamma[:, None] * (CHUNK - 1 - idx)[None, :])
    q_decay = np.exp(log_gamma[:, None] * (idx + f32(1.0))[None, :])
    cd = np.exp(log_gamma * f32(CHUNK))
    bc = lambda a: np.ascontiguousarray(np.broadcast_to(a[:, :, None], (RET_HEADS, CHUNK, LANES)))
    return cd.astype(f32), dintra.astype(f32), bc(q_decay).astype(f32), bc(k_decay).astype(f32)


def kernel(x, meta_tokens, ffn1_norm, ffn1_w_gate, ffn1_w_up, ffn1_w_down, mix_norm, w_in, ret_gn_w, diff_subln_w, diff_lambda_q1, diff_lambda_k1, diff_lambda_q2, diff_lambda_k2, w_out, ffn2_norm, ffn2_w_gate, ffn2_w_up, ffn2_w_down, final_norm):
    meta_chunk = jnp.concatenate([jnp.zeros((PAD, D_MODEL), F32), meta_tokens.astype(F32)], axis=0)
    h = (x.astype(F32).reshape(FRAME_ROWS, D_MODEL), meta_chunk)

    tab = _rotary_tables()
    tab_meta, tab_frames = jnp.asarray(tab[:CHUNK]), jnp.asarray(tab[CHUNK:])
    cd, dintra, qdec, kdec = (jnp.asarray(a) for a in _retention_consts())
    perms = jnp.asarray(_column_permutations(), BF16)

    for l in range(DEPTH):
        last = l == DEPTH - 1
        h = _ffn(l, h, ffn1_norm, ffn1_w_gate, ffn1_w_up, ffn1_w_down)
        proj_diff, y_ret = _inproj_retention(l, h, mix_norm, w_in, perms, tab_frames, tab_meta,
                                             cd, dintra, qdec, kdec, ret_gn_w)
        y_diff = _diffattn(l, proj_diff, diff_subln_w, diff_lambda_q1, diff_lambda_k1,
                           diff_lambda_q2, diff_lambda_k2)
        h = _ffn(l, h, ffn2_norm, ffn2_w_gate, ffn2_w_up, ffn2_w_down,
                 mix=(y_ret, y_diff, w_out),
                 final_g=final_norm.reshape(1, D_MODEL) if last else None)

    return h[0].reshape(BATCH, SEQ, D_MODEL).astype(x.dtype)
```

```python
import functools
import math

import numpy as np
import jax
import jax.numpy as jnp
from jax import lax
from jax.experimental import pallas as pl
from jax.experimental.pallas import tpu as pltpu

D_MODEL = 1024
BATCH = 8
SEQ = 2048
DEPTH = 2
N_META = 16
CHUNK = 128
RET_HEADS = 4
RET_DK = 128
RET_WIDTH = 512
RET_THETA = 10000.0
DIFF_HEADS = 4
DIFF_D = 64
DIFF_DV = 128
DIFF_WIDTH = 512
ROPE_THETA = 500000.0
ROPE_DIMS = 16
D_FF = 2816
EPS = 1e-6
D_IN = 3584

PAD = CHUNK - N_META
T_PAD = CHUNK + SEQ
FRAME_ROWS = BATCH * SEQ
N_TAB = 12
LANES = 128
VMEM_LIMIT = 56 * 1024 * 1024
NEG = -1e30

F32 = jnp.float32
BF16 = jnp.bfloat16


def _rms(x, g):
    return x * lax.rsqrt(jnp.mean(x * x, axis=-1, keepdims=True) + EPS) * g


def _whole(shape):
    return pl.BlockSpec(shape, lambda *_: (0,) * len(shape))


class _DenseGrid:
    def __init__(self, tile, stage, meta):
        self.tile, self.stage, self.meta = tile, stage, meta
        self.n_tiles = FRAME_ROWS // tile
        self.steps = stage + self.n_tiles + int(meta)

    def tile_index(self, i):
        return jnp.clip(i - self.stage, 0, self.n_tiles - 1)

    def row_tile(self, width):
        return pl.BlockSpec((self.tile, width), lambda i: (self.tile_index(i), 0))

    def weight_chunk(self, layer, rows, cols):
        return pl.BlockSpec((None, rows // self.stage, cols),
                            lambda i: (layer, jnp.minimum(i, self.stage - 1), 0))

    def run(self, i, stage, frames, meta):
        pl.when(i < self.stage)(stage)
        pl.when((i >= self.stage) & (i < self.stage + self.n_tiles))(frames)
        if self.meta:
            pl.when(i == self.stage + self.n_tiles)(meta)


def _stage_weight(i, src_ref, dst_ref):
    rows = src_ref.shape[0]
    dst_ref[pl.ds(pl.multiple_of(i * rows, 16), rows), :] = src_ref[...].astype(BF16)


FFN_TILE = 512
FFN_SUB_TILE = 256
STAGE_STEPS = 8
PROJ_TILE = 512


def _ffn_kernel(*refs, grid, layer, mix, final):
    refs = list(refs)
    take = lambda n: [refs.pop(0) for _ in range(n)]
    with_meta = not final
    h_refs = take(2 if with_meta else 1)
    if mix:
        yr_refs, yd_refs = take(len(h_refs)), take(len(h_refs))
        (wo_ref,) = take(1)
    g_ref, wg_ref, wu_ref, wd_ref = take(4)
    if final:
        (fn_ref,) = take(1)
    o_refs = take(len(h_refs))
    wg_s, wu_s, wd_s = take(3)
    if mix:
        (wo_s,) = take(1)
    i = pl.program_id(0)

    def stage():
        _stage_weight(i, wg_ref, wg_s)
        _stage_weight(i, wu_ref, wu_s)
        _stage_weight(i, wd_ref, wd_s)
        if mix:
            _stage_weight(i, wo_ref, wo_s)

    def rows(which):
        n = h_refs[which].shape[0]
        parts = [slice(r, min(r + FFN_SUB_TILE, n)) for r in range(0, n, FFN_SUB_TILE)]
        xs, xns, acts = [], [], []
        for sl in parts:
            x = h_refs[which][sl, :]
            if mix:
                x = (x + jnp.dot(yr_refs[which][sl, :], wo_s[:RET_WIDTH, :],
                                 preferred_element_type=F32)
                     + jnp.dot(yd_refs[which][sl, :], wo_s[RET_WIDTH:, :],
                               preferred_element_type=F32))
            xs.append(x)
            xns.append(_rms(x, g_ref[layer:layer + 1, :]).astype(BF16))
        for xn in xns:
            gate = jnp.dot(xn, wg_s[...], preferred_element_type=F32)
            up = jnp.dot(xn, wu_s[...], preferred_element_type=F32)
            acts.append((jax.nn.silu(gate) * up).astype(BF16))
        for sl, x, act in zip(parts, xs, acts):
            y = x + 0.5 * jnp.dot(act, wd_s[...], preferred_element_type=F32)
            if final:
                y = _rms(y, fn_ref[...])
            o_refs[which][sl, :] = y

    grid.run(i, stage, functools.partial(rows, 0), functools.partial(rows, 1))


def _ffn(layer, h, norm_g, wg, wu, wd, mix=None, final_g=None):
    with_meta = final_g is None
    grid = _DenseGrid(tile=FFN_TILE, stage=STAGE_STEPS, meta=with_meta)
    _row_tile, _weight_chunk = grid.row_tile, grid.weight_chunk
    pair = lambda width: [_row_tile(width)] + ([_whole((CHUNK, width))] if with_meta else [])
    keep = lambda arrays: list(arrays) if with_meta else [arrays[0]]
    args, specs = keep(h), pair(D_MODEL)
    scratch = [pltpu.VMEM((D_MODEL, D_FF), BF16), pltpu.VMEM((D_MODEL, D_FF), BF16),
               pltpu.VMEM((D_FF, D_MODEL), BF16)]
    if mix is not None:
        y_ret, y_diff, w_out = mix
        args += keep(y_ret) + keep(y_diff) + [w_out]
        specs += pair(RET_WIDTH) + pair(DIFF_WIDTH) + [_weight_chunk(layer, D_MODEL, D_MODEL)]
        scratch.append(pltpu.VMEM((D_MODEL, D_MODEL), BF16))
    args += [norm_g, wg, wu, wd]
    specs += [_whole((DEPTH, D_MODEL)), _weight_chunk(layer, D_MODEL, D_FF),
              _weight_chunk(layer, D_MODEL, D_FF), _weight_chunk(layer, D_FF, D_MODEL)]
    if final_g is not None:
        args.append(final_g)
        specs.append(_whole((1, D_MODEL)))
    out_shape = [jax.ShapeDtypeStruct((FRAME_ROWS, D_MODEL), F32)]
    if with_meta:
        out_shape.append(jax.ShapeDtypeStruct((CHUNK, D_MODEL), F32))
    return pl.pallas_call(
        functools.partial(_ffn_kernel, grid=grid, layer=layer, mix=mix is not None,
                          final=final_g is not None),
        grid=(grid.steps,),
        in_specs=specs,
        out_specs=pair(D_MODEL),
        out_shape=out_shape,
        scratch_shapes=scratch,
        compiler_params=pltpu.CompilerParams(
            dimension_semantics=("arbitrary",), vmem_limit_bytes=VMEM_LIMIT),
        name="ffn",
    )(*args)


_ROTATIONS = (
    (0, 0, LANES - 1, 1),
    (RET_WIDTH, 3, LANES - 1, 1),
    (4 * RET_WIDTH, 6, 8, LANES - 8),
    (4 * RET_WIDTH + DIFF_WIDTH, 9, 8, LANES - 8),
)


RET_COLS = 4 * RET_WIDTH
DIFF_COLS = D_IN - RET_COLS
N_PROJ_TILES = FRAME_ROWS // PROJ_TILE
TILES_PER_BATCH = SEQ // PROJ_TILE


def _inproj_retention_kernel(cd_ref, hf_ref, hm_ref, g_ref, w_ref, tabf_ref, tabm_ref,
                             dintra_ref, qdec_ref, kdec_ref, gnw_ref,
                             pf_ref, pm_ref, yf_ref, ym_ref,
                             w_s, ring_ref, state_ref, meta_state_ref, *, layer):
    nt = (((1,), (1,)), ((), ()))
    tn = (((0,), (0,)), ((), ()))
    i = pl.program_id(0)
    t = i - (STAGE_STEPS + 1)

    def project_steps(h_ref, tab_ref, ret_ref, diff_ref):
        rotation = {base: rest for base, *rest in _ROTATIONS}
        cell = {}

        def tab(k):
            return tab_ref[:, k * LANES:(k + 1) * LANES]

        def group(base):
            if not cell:
                cell["xn"] = _rms(h_ref[...], g_ref[layer:layer + 1, :]).astype(BF16)
            p = jnp.dot(cell["xn"], w_s[:, base:base + RET_WIDTH], preferred_element_type=F32)
            dst, lo = (ret_ref, base) if base < RET_COLS else (diff_ref, base - RET_COLS)
            if base in rotation:
                t0, roll_a, roll_b = rotation[base]
                for hd in range(RET_HEADS):
                    xs = p[:, hd * LANES:(hd + 1) * LANES]
                    rot = (xs * tab(t0) + pltpu.roll(xs, roll_a, 1) * tab(t0 + 1)
                           + pltpu.roll(xs, roll_b, 1) * tab(t0 + 2))
                    dst[:, lo + hd * LANES:lo + (hd + 1) * LANES] = rot.astype(BF16)
            else:
                dst[:, lo:lo + RET_WIDTH] = p.astype(BF16)

        return [functools.partial(group, base) for base in range(0, D_IN, RET_WIDTH)]

    def retention_steps(src_ref, y_ref, state_in, state_out_refs):
        n_chunks = src_ref.shape[0] // CHUNK
        units = [(c, hd) for c in range(n_chunks) for hd in range(RET_HEADS)]
        scores, incs, outs, cell = {}, {}, {}, {}

        def col(c, hd, which):
            return src_ref[c * CHUNK:(c + 1) * CHUNK,
                           which * RET_WIDTH + hd * LANES:which * RET_WIDTH + (hd + 1) * LANES]

        def score(c, hd):
            s = lax.dot_general(col(c, hd, 0), col(c, hd, 1), nt, preferred_element_type=F32)
            scores[c, hd] = (s * dintra_ref[hd]).astype(BF16)

        def increment(c, hd):
            kd = (col(c, hd, 1).astype(F32) * kdec_ref[hd]).astype(BF16)
            incs[c, hd] = lax.dot_general(kd, col(c, hd, 2), tn, preferred_element_type=F32)

        def output(c, hd):
            if not cell:
                cell["state"] = state_in()
            state = cell["state"]
            if state is None:
                outs[c, hd] = jnp.dot(scores[c, hd], col(c, hd, 2), preferred_element_type=F32)
            else:
                qd = (col(c, hd, 0).astype(F32) * qdec_ref[hd]).astype(BF16)
                outs[c, hd] = jnp.dot(jnp.concatenate([scores[c, hd], qd], axis=1),
                                      jnp.concatenate([col(c, hd, 2), state[hd].astype(BF16)],
                                                      axis=0),
                                      preferred_element_type=F32)
            if hd == RET_HEADS - 1:
                if state is None:
                    cell["state"] = [incs[c, h] for h in range(RET_HEADS)]
                else:
                    cell["state"] = [state[h] * cd_ref[h] + incs[c, h] for h in range(RET_HEADS)]

        def store_state():
            for hd in range(RET_HEADS):
                for ref in state_out_refs:
                    ref[hd] = cell["state"][hd]

        def norm_gate(c, hd):
            o = outs[c, hd]
            mu = jnp.mean(o, axis=-1, keepdims=True)
            d = o - mu
            var = jnp.mean(d * d, axis=-1, keepdims=True)
            on = d * lax.rsqrt(var + EPS) * gnw_ref[layer:layer + 1, hd * LANES:(hd + 1) * LANES]
            y_ref[c * CHUNK:(c + 1) * CHUNK, hd * LANES:(hd + 1) * LANES] = (
                jax.nn.silu(col(c, hd, 3).astype(F32)) * on).astype(BF16)

        return ([functools.partial(stage, c, hd) for stage in (score, increment, output)
                 for c, hd in units] + [store_state]
                + [functools.partial(norm_gate, c, hd) for c, hd in units])

    def previous_tile_retention_steps():
        def state_in():
            first = (t + TILES_PER_BATCH - 1) % TILES_PER_BATCH == 0
            return [jnp.where(first, meta_state_ref[hd], state_ref[hd])
                    for hd in range(RET_HEADS)]

        return retention_steps(ring_ref.at[(t + 1) % 2], yf_ref, state_in, [state_ref])

    def emit(major, minor=()):
        per = -(-len(minor) // len(major))
        for k, step in enumerate(major):
            step()
            for extra in minor[k * per:(k + 1) * per]:
                extra()

    @pl.when(i < STAGE_STEPS)
    def _():
        _stage_weight(i, w_ref, w_s)

    @pl.when(i == STAGE_STEPS)
    def _():
        ring_ref[...] = jnp.zeros_like(ring_ref)
        state_ref[...] = jnp.zeros_like(state_ref)
        meta_ring = ring_ref.at[0, :CHUNK]
        emit(project_steps(hm_ref, tabm_ref, meta_ring, pm_ref))
        emit(retention_steps(meta_ring, ym_ref, lambda: None, [meta_state_ref]))

    @pl.when((t >= 0) & (t < N_PROJ_TILES))
    def _():
        steps = previous_tile_retention_steps()
        n_norm = (PROJ_TILE // CHUNK) * RET_HEADS
        emit(steps[:-n_norm])
        emit(project_steps(hf_ref, tabf_ref, ring_ref.at[t % 2], pf_ref), steps[-n_norm:])

    @pl.when(t == N_PROJ_TILES)
    def _():
        emit(previous_tile_retention_steps())


def _inproj_retention(layer, h, norm_g, w_in, tab_frames, tab_meta, cd, dintra, qdec, kdec, gn_w):
    steps = STAGE_STEPS + 1 + N_PROJ_TILES + 1
    tile = lambda i: jnp.clip(i - (STAGE_STEPS + 1), 0, N_PROJ_TILES - 1)
    lag = lambda i: jnp.clip(i - (STAGE_STEPS + 2), 0, N_PROJ_TILES - 1)
    const3 = _whole((RET_HEADS, CHUNK, LANES))
    outs = pl.pallas_call(
        functools.partial(_inproj_retention_kernel, layer=layer),
        grid=(steps,),
        in_specs=[pl.BlockSpec(memory_space=pltpu.SMEM),
                  pl.BlockSpec((PROJ_TILE, D_MODEL), lambda i: (tile(i), 0)),
                  _whole((CHUNK, D_MODEL)), _whole((DEPTH, D_MODEL)),
                  pl.BlockSpec((None, D_MODEL // STAGE_STEPS, D_IN),
                               lambda i: (layer, jnp.minimum(i, STAGE_STEPS - 1), 0)),
                  pl.BlockSpec((PROJ_TILE, N_TAB * LANES),
                               lambda i: (tile(i) % TILES_PER_BATCH, 0)),
                  _whole((CHUNK, N_TAB * LANES)),
                  const3, const3, const3, _whole((DEPTH, RET_WIDTH))],
        out_specs=[pl.BlockSpec((PROJ_TILE, DIFF_COLS), lambda i: (tile(i), 0)),
                   _whole((CHUNK, DIFF_COLS)),
                   pl.BlockSpec((PROJ_TILE, RET_WIDTH), lambda i: (lag(i), 0)),
                   _whole((CHUNK, RET_WIDTH))],
        out_shape=[jax.ShapeDtypeStruct((FRAME_ROWS, DIFF_COLS), BF16),
                   jax.ShapeDtypeStruct((CHUNK, DIFF_COLS), BF16),
                   jax.ShapeDtypeStruct((FRAME_ROWS, RET_WIDTH), BF16),
                   jax.ShapeDtypeStruct((CHUNK, RET_WIDTH), BF16)],
        scratch_shapes=[pltpu.VMEM((D_MODEL, D_IN), BF16),
                        pltpu.VMEM((2, PROJ_TILE, RET_COLS), BF16),
                        pltpu.VMEM((RET_HEADS, RET_DK, LANES), F32),
                        pltpu.VMEM((RET_HEADS, RET_DK, LANES), F32)],
        compiler_params=pltpu.CompilerParams(
            dimension_semantics=("arbitrary",), vmem_limit_bytes=VMEM_LIMIT),
        name="inproj_retention",
    )(cd, h[0], h[1], norm_g, w_in, tab_frames, tab_meta, dintra, qdec, kdec, gn_w)
    return (outs[0], outs[1]), (outs[2], outs[3])


_QUERY_BLOCKS = ((0, CHUNK),) + tuple((r, 2 * CHUNK) for r in range(CHUNK, T_PAD, 2 * CHUNK))


def _key_tiles(nk):
    return ((PAD, N_META),) + tuple((r, 2 * CHUNK) for r in range(CHUNK, nk, 2 * CHUNK))


def _diffattn_kernel(qf_ref, kf_ref, vf_ref, qm_ref, km_ref, vm_ref, w_ref, lq1_ref, lk1_ref,
                     lq2_ref, lk2_ref, of_ref, om_ref, s_ref, p_ref, vt_ref, *, layer, lambda_init):
    row = slice(layer, layer + 1)
    lam = (jnp.exp(jnp.sum(lq1_ref[row, :] * lk1_ref[row, :], axis=-1, keepdims=True))
           - jnp.exp(jnp.sum(lq2_ref[row, :] * lk2_ref[row, :], axis=-1, keepdims=True))
           + lambda_init)
    w_out = w_ref[row, :] * (1.0 - lambda_init)
    tn = (((0,), (0,)), ((), ()))

    def rows(meta_ref, frames_ref, r0, n):
        if r0 < CHUNK:
            return meta_ref[r0:r0 + n, :]
        return frames_ref[0, r0 - CHUNK:r0 - CHUNK + n, :]

    def fold(state, key, value, op):
        state[key] = value if key not in state else op(state[key], value)

    def score_tiles(blk, slot, state):
        r0, nq = _QUERY_BLOCKS[blk]
        nk = r0 + nq
        qt = rows(qm_ref, qf_ref, r0, nq).astype(F32).T
        feat = lax.broadcasted_iota(jnp.int32, (LANES, nq), 0)
        qq = jnp.concatenate([jnp.where(feat < DIFF_D, qt, 0.0),
                              jnp.where(feat >= DIFF_D, qt, 0.0)], axis=1).astype(BF16)

        def tile(t0, tk):
            s = jnp.dot(rows(km_ref, kf_ref, t0, tk), qq, preferred_element_type=F32)
            krow = lax.broadcasted_iota(jnp.int32, (tk, 2 * nq), 0)
            if t0 + tk > r0:
                qcol = lax.broadcasted_iota(jnp.int32, (tk, 2 * nq), 1) % nq
                s = jnp.where(krow + (t0 - r0) <= qcol, s, NEG)
            s_ref[slot, t0:t0 + tk, :2 * nq] = s
            fold(state, "max", jnp.max(s, axis=0, keepdims=True), jnp.maximum)

        return [functools.partial(tile, t0, tk) for t0, tk in _key_tiles(nk)]

    def prob_tiles(blk, slot, smax, state):
        r0, nq = _QUERY_BLOCKS[blk]
        nk = r0 + nq

        def tile(t0, tk):
            p = jnp.exp2(s_ref[slot, t0:t0 + tk, :2 * nq] - smax)
            fold(state, "pos", jnp.sum(p[:, :nq], axis=0, keepdims=True), jnp.add)
            fold(state, "neg", jnp.sum(p[:, nq:], axis=0, keepdims=True), jnp.add)
            p_ref[slot, t0:t0 + tk, :2 * nq] = p.astype(BF16)

        return [functools.partial(tile, t0, tk) for t0, tk in _key_tiles(nk)]

    def finish(blk, slot, sums):
        r0, nq = _QUERY_BLOCKS[blk]
        nk = r0 + nq
        ot = lax.dot_general(vm_ref[PAD:, :], p_ref[slot, PAD:CHUNK, :2 * nq], tn,
                             preferred_element_type=F32)
        if nk > CHUNK:
            ot = ot + jnp.dot(vt_ref[:, :nk - CHUNK], p_ref[slot, CHUNK:nk, :2 * nq],
                              preferred_element_type=F32)
        o = (ot[:, :nq] * (1.0 / sums["pos"]) - ot[:, nq:] * (lam / sums["neg"])).T
        o = o * lax.rsqrt(jnp.mean(o * o, axis=-1, keepdims=True) + EPS) * w_out
        if r0 == 0:
            orow = lax.broadcasted_iota(jnp.int32, (nq, LANES), 0)
            om_ref[...] = jnp.where(orow >= PAD, o, 0.0).astype(BF16)
        else:
            of_ref[0, r0 - CHUNK:nk - CHUNK, :] = o.astype(BF16)

    n_blocks = len(_QUERY_BLOCKS)
    order = list(range(0, n_blocks, 2)) + list(range(n_blocks - 1 - n_blocks % 2, 0, -2))
    vt_ref[...] = vf_ref[0].astype(F32).T.astype(BF16)
    score_state, pending = {}, None
    for run in score_tiles(order[0], 0, score_state):
        run()
    for r, blk in enumerate(order):
        smax, sums, score_state = score_state["max"], {}, {}
        ahead = score_tiles(order[r + 1], (r + 1) % 2, score_state) if r + 1 < n_blocks else []
        current = prob_tiles(blk, r % 2, smax, sums)
        for n in range(max(len(ahead), len(current))):
            for runs in (current, ahead):
                if n < len(runs):
                    runs[n]()
            if n == 0 and pending is not None:
                finish(*pending)
        pending = (blk, r % 2, sums)
    finish(*pending)


def _diffattn(layer, proj, subln_w, lq1, lk1, lq2, lk2):
    proj_frames, proj_meta = proj
    frames3 = proj_frames.reshape(BATCH, SEQ, DIFF_COLS)
    lambda_init = 0.8 - 0.6 * math.exp(-0.3 * layer)
    fblk = lambda c: pl.BlockSpec((1, SEQ, LANES), lambda hd, b, c=c: (b, 0, c + hd))
    mblk = lambda c: pl.BlockSpec((CHUNK, LANES), lambda hd, b, c=c: (0, c + hd))
    small = lambda n: _whole((DEPTH, n))
    y_frames, y_meta = pl.pallas_call(
        functools.partial(_diffattn_kernel, layer=layer, lambda_init=lambda_init),
        grid=(DIFF_HEADS, BATCH),
        in_specs=[fblk(0), fblk(DIFF_HEADS), fblk(2 * DIFF_HEADS),
                  mblk(0), mblk(DIFF_HEADS), mblk(2 * DIFF_HEADS),
                  small(DIFF_DV), small(DIFF_D), small(DIFF_D), small(DIFF_D), small(DIFF_D)],
        out_specs=[pl.BlockSpec((1, SEQ, LANES), lambda hd, b: (b, 0, hd)),
                   pl.BlockSpec((CHUNK, LANES), lambda hd, b: (0, hd))],
        out_shape=[jax.ShapeDtypeStruct((BATCH, SEQ, DIFF_WIDTH), BF16),
                   jax.ShapeDtypeStruct((CHUNK, DIFF_WIDTH), BF16)],
        scratch_shapes=[pltpu.VMEM((2, T_PAD, 4 * CHUNK), F32),
                        pltpu.VMEM((2, T_PAD, 4 * CHUNK), BF16),
                        pltpu.VMEM((DIFF_DV, SEQ), BF16)],
        compiler_params=pltpu.CompilerParams(
            dimension_semantics=("arbitrary", "arbitrary"), vmem_limit_bytes=VMEM_LIMIT),
        name="diffattn",
    )(frames3, frames3, frames3, proj_meta, proj_meta, proj_meta, subln_w, lq1, lk1, lq2, lk2)
    return y_frames.reshape(FRAME_ROWS, DIFF_WIDTH), y_meta


def _rotary_tables():
    f32 = np.float32
    pos = np.arange(T_PAD, dtype=f32) - f32(PAD)
    angle = (f32(RET_THETA) ** (-np.linspace(0.0, 1.0, RET_DK // 2, dtype=f32))).astype(f32)
    fr = pos[:, None] * angle[None, :]
    c, s = np.cos(fr), np.sin(fr)
    zero = np.zeros_like(s)
    cos_i = np.repeat(c, 2, axis=-1)
    sin_even = np.stack([-s, zero], axis=-1).reshape(T_PAD, RET_DK)
    sin_odd = np.stack([zero, s], axis=-1).reshape(T_PAD, RET_DK)
    ks = f32(RET_DK ** -0.5)
    inv = (f32(ROPE_THETA) ** (-np.arange(0, ROPE_DIMS, 2, dtype=f32) / f32(ROPE_DIMS))).astype(f32)
    fq = pos[:, None] * inv[None, :]
    emb = np.concatenate([fq, fq], axis=-1)
    ce, se = np.cos(emb), np.sin(emb)
    half, rest = ROPE_DIMS // 2, DIFF_D - ROPE_DIMS
    zeros = lambda n: np.zeros((T_PAD, n), f32)
    two = lambda a: np.concatenate([a, a], axis=-1)
    c_d = two(np.concatenate([ce, np.ones((T_PAD, rest), f32)], axis=-1))
    s_lo = two(np.concatenate([zeros(half), se[:, half:], zeros(rest)], axis=-1))
    s_hi = two(np.concatenate([-se[:, :half], zeros(half + rest)], axis=-1))
    qs = f32((DIFF_D ** -0.5) * math.log2(math.e))
    tabs = [cos_i, sin_even, sin_odd, cos_i * ks, sin_even * ks, sin_odd * ks,
            c_d * qs, s_lo * qs, s_hi * qs, c_d, s_lo, s_hi]
    return np.concatenate(tabs, axis=-1).astype(f32)


def _retention_consts():
    f32 = np.float32
    log_gamma = np.log(f32(1.0) - f32(2.0) ** (f32(-5.0) - np.arange(RET_HEADS, dtype=f32))).astype(f32)
    idx = np.arange(CHUNK, dtype=f32)
    rel = idx[:, None] - idx[None, :]
    dintra = np.where(rel >= 0, np.exp(log_gamma[:, None, None] * np.maximum(rel, f32(0.0))), f32(0.0))
    k_decay = np.exp(log_gamma[:, None] * (CHUNK - 1 - idx)[None, :])
    q_decay = np.exp(log_gamma[:, None] * (idx + f32(1.0))[None, :])
    cd = np.exp(log_gamma * f32(CHUNK))
    bc = lambda a: np.ascontiguousarray(np.broadcast_to(a[:, :, None], (RET_HEADS, CHUNK, LANES)))
    return cd.astype(f32), dintra.astype(f32), bc(q_decay).astype(f32), bc(k_decay).astype(f32)


def kernel(x, meta_tokens, ffn1_norm, ffn1_w_gate, ffn1_w_up, ffn1_w_down, mix_norm, w_in, ret_gn_w, diff_subln_w, diff_lambda_q1, diff_lambda_k1, diff_lambda_q2, diff_lambda_k2, w_out, ffn2_norm, ffn2_w_gate, ffn2_w_up, ffn2_w_down, final_norm):
    meta_chunk = jnp.concatenate([jnp.zeros((PAD, D_MODEL), F32), meta_tokens.astype(F32)], axis=0)
    h = (x.astype(F32).reshape(FRAME_ROWS, D_MODEL), meta_chunk)

    tab = _rotary_tables()
    tab_meta, tab_frames = jnp.asarray(tab[:CHUNK]), jnp.asarray(tab[CHUNK:])
    cd, dintra, qdec, kdec = (jnp.asarray(a) for a in _retention_consts())

    for l in range(DEPTH):
        last = l == DEPTH - 1
        h = _ffn(l, h, ffn1_norm, ffn1_w_gate, ffn1_w_up, ffn1_w_down)
        proj_diff, y_ret = _inproj_retention(l, h, mix_norm, w_in, tab_frames, tab_meta,
                                             cd, dintra, qdec, kdec, ret_gn_w)
        y_diff = _diffattn(l, proj_diff, diff_subln_w, diff_lambda_q1, diff_lambda_k1,
                           diff_lambda_q2, diff_lambda_k2)
        h = _ffn(l, h, ffn2_norm, ffn2_w_gate, ffn2_w_up, ffn2_w_down,
                 mix=(y_ret, y_diff, w_out),
                 final_g=final_norm.reshape(1, D_MODEL) if last else None)

    return h[0].reshape(BATCH, SEQ, D_MODEL).astype(x.dtype)
```

```python
import functools
import math

import numpy as np
import jax
import jax.numpy as jnp
from jax import lax
from jax.experimental import pallas as pl
from jax.experimental.pallas import tpu as pltpu

D_MODEL = 1024
BATCH = 8
SEQ = 2048
DEPTH = 2
N_META = 16
CHUNK = 128
RET_HEADS = 4
RET_DK = 128
RET_WIDTH = 512
RET_THETA = 10000.0
DIFF_HEADS = 4
DIFF_D = 64
DIFF_DV = 128
DIFF_WIDTH = 512
ROPE_THETA = 500000.0
ROPE_DIMS = 16
D_FF = 2816
EPS = 1e-6
D_IN = 3584

PAD = CHUNK - N_META
T_PAD = CHUNK + SEQ
FRAME_ROWS = BATCH * SEQ
N_TAB = 12
LANES = 128
VMEM_LIMIT = 56 * 1024 * 1024
NEG = -1e30

F32 = jnp.float32
BF16 = jnp.bfloat16


def _rms(x, g):
    return x * lax.rsqrt(jnp.mean(x * x, axis=-1, keepdims=True) + EPS) * g


def _whole(shape):
    return pl.BlockSpec(shape, lambda *_: (0,) * len(shape))


class _DenseGrid:
    def __init__(self, tile, stage, meta):
        self.tile, self.stage, self.meta = tile, stage, meta
        self.n_tiles = FRAME_ROWS // tile
        self.steps = stage + self.n_tiles + int(meta)

    def tile_index(self, i):
        return jnp.clip(i - self.stage, 0, self.n_tiles - 1)

    def row_tile(self, width):
        return pl.BlockSpec((self.tile, width), lambda i: (self.tile_index(i), 0))

    def weight_chunk(self, layer, rows, cols):
        return pl.BlockSpec((None, rows // self.stage, cols),
                            lambda i: (layer, jnp.minimum(i, self.stage - 1), 0))

    def run(self, i, stage, frames, meta):
        pl.when(i < self.stage)(stage)
        pl.when((i >= self.stage) & (i < self.stage + self.n_tiles))(frames)
        if self.meta:
            pl.when(i == self.stage + self.n_tiles)(meta)


def _stage_weight(i, src_ref, dst_ref):
    rows = src_ref.shape[0]
    dst_ref[pl.ds(pl.multiple_of(i * rows, 16), rows), :] = src_ref[...].astype(BF16)


FFN_TILE = 512
FFN_SUB_TILE = 256
STAGE_STEPS = 8
PROJ_TILE = 512


def _ffn_kernel(*refs, grid, layer, mix, final):
    refs = list(refs)
    take = lambda n: [refs.pop(0) for _ in range(n)]
    with_meta = not final
    h_refs = take(2 if with_meta else 1)
    if mix:
        yr_refs, yd_refs = take(len(h_refs)), take(len(h_refs))
        (wo_ref,) = take(1)
    g_ref, wg_ref, wu_ref, wd_ref = take(4)
    if final:
        (fn_ref,) = take(1)
    o_refs = take(len(h_refs))
    wg_s, wu_s, wd_s = take(3)
    if mix:
        (wo_s,) = take(1)
    i = pl.program_id(0)

    def stage():
        _stage_weight(i, wg_ref, wg_s)
        _stage_weight(i, wu_ref, wu_s)
        _stage_weight(i, wd_ref, wd_s)
        if mix:
            _stage_weight(i, wo_ref, wo_s)

    def rows(which):
        n = h_refs[which].shape[0]
        parts = [slice(r, min(r + FFN_SUB_TILE, n)) for r in range(0, n, FFN_SUB_TILE)]
        xs, xns, acts = [], [], []
        for sl in parts:
            x = h_refs[which][sl, :]
            if mix:
                x = (x + jnp.dot(yr_refs[which][sl, :], wo_s[:RET_WIDTH, :],
                                 preferred_element_type=F32)
                     + jnp.dot(yd_refs[which][sl, :], wo_s[RET_WIDTH:, :],
                               preferred_element_type=F32))
            xs.append(x)
            xns.append(_rms(x, g_ref[layer:layer + 1, :]).astype(BF16))
        for xn in xns:
            gate = jnp.dot(xn, wg_s[...], preferred_element_type=F32)
            up = jnp.dot(xn, wu_s[...], preferred_element_type=F32)
            acts.append((jax.nn.silu(gate) * up).astype(BF16))
        for sl, x, act in zip(parts, xs, acts):
            y = x + 0.5 * jnp.dot(act, wd_s[...], preferred_element_type=F32)
            if final:
                y = _rms(y, fn_ref[...])
            o_refs[which][sl, :] = y

    grid.run(i, stage, functools.partial(rows, 0), functools.partial(rows, 1))


def _ffn(layer, h, norm_g, wg, wu, wd, mix=None, final_g=None):
    with_meta = final_g is None
    grid = _DenseGrid(tile=FFN_TILE, stage=STAGE_STEPS, meta=with_meta)
    _row_tile, _weight_chunk = grid.row_tile, grid.weight_chunk
    pair = lambda width: [_row_tile(width)] + ([_whole((CHUNK, width))] if with_meta else [])
    keep = lambda arrays: list(arrays) if with_meta else [arrays[0]]
    args, specs = keep(h), pair(D_MODEL)
    scratch = [pltpu.VMEM((D_MODEL, D_FF), BF16), pltpu.VMEM((D_MODEL, D_FF), BF16),
               pltpu.VMEM((D_FF, D_MODEL), BF16)]
    if mix is not None:
        y_ret, y_diff, w_out = mix
        args += keep(y_ret) + keep(y_diff) + [w_out]
        specs += pair(RET_WIDTH) + pair(DIFF_WIDTH) + [_weight_chunk(layer, D_MODEL, D_MODEL)]
        scratch.append(pltpu.VMEM((D_MODEL, D_MODEL), BF16))
    args += [norm_g, wg, wu, wd]
    specs += [_whole((DEPTH, D_MODEL)), _weight_chunk(layer, D_MODEL, D_FF),
              _weight_chunk(layer, D_MODEL, D_FF), _weight_chunk(layer, D_FF, D_MODEL)]
    if final_g is not None:
        args.append(final_g)
        specs.append(_whole((1, D_MODEL)))
    out_shape = [jax.ShapeDtypeStruct((FRAME_ROWS, D_MODEL), F32)]
    if with_meta:
        out_shape.append(jax.ShapeDtypeStruct((CHUNK, D_MODEL), F32))
    return pl.pallas_call(
        functools.partial(_ffn_kernel, grid=grid, layer=layer, mix=mix is not None,
                          final=final_g is not None),
        grid=(grid.steps,),
        in_specs=specs,
        out_specs=pair(D_MODEL),
        out_shape=out_shape,
        scratch_shapes=scratch,
        compiler_params=pltpu.CompilerParams(
            dimension_semantics=("arbitrary",), vmem_limit_bytes=VMEM_LIMIT),
        name="ffn",
    )(*args)


_ROTATIONS = (
    (0, 0, LANES - 1, 1),
    (RET_WIDTH, 3, LANES - 1, 1),
    (4 * RET_WIDTH, 6, 8, LANES - 8),
    (4 * RET_WIDTH + DIFF_WIDTH, 9, 8, LANES - 8),
)


RET_COLS = 4 * RET_WIDTH
DIFF_COLS = D_IN - RET_COLS
N_PROJ_TILES = FRAME_ROWS // PROJ_TILE
TILES_PER_BATCH = SEQ // PROJ_TILE


def _inproj_retention_kernel(cd_ref, hf_ref, hm_ref, g_ref, w_ref, tabf_ref, tabm_ref,
                             dintra_ref, qdec_ref, kdec_ref, gnw_ref,
                             pf_ref, pm_ref, yf_ref, ym_ref,
                             w_s, ring_ref, state_ref, meta_state_ref, *, layer):
    nt = (((1,), (1,)), ((), ()))
    tn = (((0,), (0,)), ((), ()))
    i = pl.program_id(0)
    t = i - (STAGE_STEPS + 1)

    def project_steps(h_ref, tab_ref, ret_ref, diff_ref):
        rotation = {base: rest for base, *rest in _ROTATIONS}
        cell = {}

        def tab(k):
            return tab_ref[:, k * LANES:(k + 1) * LANES]

        def group(base):
            if not cell:
                cell["xn"] = _rms(h_ref[...], g_ref[layer:layer + 1, :]).astype(BF16)
            p = jnp.dot(cell["xn"], w_s[:, base:base + RET_WIDTH], preferred_element_type=F32)
            dst, lo = (ret_ref, base) if base < RET_COLS else (diff_ref, base - RET_COLS)
            if base in rotation:
                t0, roll_a, roll_b = rotation[base]
                for hd in range(RET_HEADS):
                    xs = p[:, hd * LANES:(hd + 1) * LANES]
                    rot = (xs * tab(t0) + pltpu.roll(xs, roll_a, 1) * tab(t0 + 1)
                           + pltpu.roll(xs, roll_b, 1) * tab(t0 + 2))
                    dst[:, lo + hd * LANES:lo + (hd + 1) * LANES] = rot.astype(BF16)
            else:
                dst[:, lo:lo + RET_WIDTH] = p.astype(BF16)

        return [functools.partial(group, base) for base in range(0, D_IN, RET_WIDTH)]

    def retention_steps(src_ref, y_ref, state_in, state_out_refs):
        n_chunks = src_ref.shape[0] // CHUNK
        units = [(c, hd) for c in range(n_chunks) for hd in range(RET_HEADS)]
        scores, incs, outs, cell = {}, {}, {}, {}

        def col(c, hd, which):
            return src_ref[c * CHUNK:(c + 1) * CHUNK,
                           which * RET_WIDTH + hd * LANES:which * RET_WIDTH + (hd + 1) * LANES]

        def score(c, hd):
            s = lax.dot_general(col(c, hd, 0), col(c, hd, 1), nt, preferred_element_type=F32)
            scores[c, hd] = (s * dintra_ref[hd]).astype(BF16)

        def increment(c, hd):
            kd = (col(c, hd, 1).astype(F32) * kdec_ref[hd]).astype(BF16)
            incs[c, hd] = lax.dot_general(kd, col(c, hd, 2), tn, preferred_element_type=F32)

        def output(c, hd):
            if not cell:
                cell["state"] = state_in()
            state = cell["state"]
            if state is None:
                outs[c, hd] = jnp.dot(scores[c, hd], col(c, hd, 2), preferred_element_type=F32)
            else:
                qd = (col(c, hd, 0).astype(F32) * qdec_ref[hd]).astype(BF16)
                outs[c, hd] = jnp.dot(jnp.concatenate([scores[c, hd], qd], axis=1),
                                      jnp.concatenate([col(c, hd, 2), state[hd].astype(BF16)],
                                                      axis=0),
                                      preferred_element_type=F32)
            if hd == RET_HEADS - 1:
                if state is None:
                    cell["state"] = [incs[c, h] for h in range(RET_HEADS)]
                else:
                    cell["state"] = [state[h] * cd_ref[h] + incs[c, h] for h in range(RET_HEADS)]

        def store_state():
            for hd in range(RET_HEADS):
                for ref in state_out_refs:
                    ref[hd] = cell["state"][hd]

        def norm_gate(c, hd):
            o = outs[c, hd]
            mu = jnp.mean(o, axis=-1, keepdims=True)
            d = o - mu
            var = jnp.mean(d * d, axis=-1, keepdims=True)
            on = d * lax.rsqrt(var + EPS) * gnw_ref[layer:layer + 1, hd * LANES:(hd + 1) * LANES]
            y_ref[c * CHUNK:(c + 1) * CHUNK, hd * LANES:(hd + 1) * LANES] = (
                jax.nn.silu(col(c, hd, 3).astype(F32)) * on).astype(BF16)

        return ([functools.partial(stage, c, hd) for stage in (score, increment, output)
                 for c, hd in units] + [store_state]
                + [functools.partial(norm_gate, c, hd) for c, hd in units])

    def previous_tile_retention_steps():
        def state_in():
            first = (t + TILES_PER_BATCH - 1) % TILES_PER_BATCH == 0
            return [jnp.where(first, meta_state_ref[hd], state_ref[hd])
                    for hd in range(RET_HEADS)]

        return retention_steps(ring_ref.at[(t + 1) % 2], yf_ref, state_in, [state_ref])

    def emit(major, minor=()):
        per = -(-len(minor) // len(major))
        for k, step in enumerate(major):
            step()
            for extra in minor[k * per:(k + 1) * per]:
                extra()

    @pl.when(i < STAGE_STEPS)
    def _():
        _stage_weight(i, w_ref, w_s)

    @pl.when(i == STAGE_STEPS)
    def _():
        ring_ref[...] = jnp.zeros_like(ring_ref)
        state_ref[...] = jnp.zeros_like(state_ref)
        meta_ring = ring_ref.at[0, :CHUNK]
        emit(project_steps(hm_ref, tabm_ref, meta_ring, pm_ref))
        emit(retention_steps(meta_ring, ym_ref, lambda: None, [meta_state_ref]))

    @pl.when((t >= 0) & (t < N_PROJ_TILES))
    def _():
        steps = previous_tile_retention_steps()
        n_norm = (PROJ_TILE // CHUNK) * RET_HEADS
        emit(steps[:-n_norm])
        emit(project_steps(hf_ref, tabf_ref, ring_ref.at[t % 2], pf_ref), steps[-n_norm:])

    @pl.when(t == N_PROJ_TILES)
    def _():
        emit(previous_tile_retention_steps())


def _inproj_retention(layer, h, norm_g, w_in, tab_frames, tab_meta, cd, dintra, qdec, kdec, gn_w):
    steps = STAGE_STEPS + 1 + N_PROJ_TILES + 1
    tile = lambda i: jnp.clip(i - (STAGE_STEPS + 1), 0, N_PROJ_TILES - 1)
    lag = lambda i: jnp.clip(i - (STAGE_STEPS + 2), 0, N_PROJ_TILES - 1)
    const3 = _whole((RET_HEADS, CHUNK, LANES))
    outs = pl.pallas_call(
        functools.partial(_inproj_retention_kernel, layer=layer),
        grid=(steps,),
        in_specs=[pl.BlockSpec(memory_space=pltpu.SMEM),
                  pl.BlockSpec((PROJ_TILE, D_MODEL), lambda i: (tile(i), 0)),
                  _whole((CHUNK, D_MODEL)), _whole((DEPTH, D_MODEL)),
                  pl.BlockSpec((None, D_MODEL // STAGE_STEPS, D_IN),
                               lambda i: (layer, jnp.minimum(i, STAGE_STEPS - 1), 0)),
                  pl.BlockSpec((PROJ_TILE, N_TAB * LANES),
                               lambda i: (tile(i) % TILES_PER_BATCH, 0)),
                  _whole((CHUNK, N_TAB * LANES)),
                  const3, const3, const3, _whole((DEPTH, RET_WIDTH))],
        out_specs=[pl.BlockSpec((PROJ_TILE, DIFF_COLS), lambda i: (tile(i), 0)),
                   _whole((CHUNK, DIFF_COLS)),
                   pl.BlockSpec((PROJ_TILE, RET_WIDTH), lambda i: (lag(i), 0)),
                   _whole((CHUNK, RET_WIDTH))],
        out_shape=[jax.ShapeDtypeStruct((FRAME_ROWS, DIFF_COLS), BF16),
                   jax.ShapeDtypeStruct((CHUNK, DIFF_COLS), BF16),
                   jax.ShapeDtypeStruct((FRAME_ROWS, RET_WIDTH), BF16),
                   jax.ShapeDtypeStruct((CHUNK, RET_WIDTH), BF16)],
        scratch_shapes=[pltpu.VMEM((D_MODEL, D_IN), BF16),
                        pltpu.VMEM((2, PROJ_TILE, RET_COLS), BF16),
                        pltpu.VMEM((RET_HEADS, RET_DK, LANES), F32),
                        pltpu.VMEM((RET_HEADS, RET_DK, LANES), F32)],
        compiler_params=pltpu.CompilerParams(
            dimension_semantics=("arbitrary",), vmem_limit_bytes=VMEM_LIMIT),
        name="inproj_retention",
    )(cd, h[0], h[1], norm_g, w_in, tab_frames, tab_meta, dintra, qdec, kdec, gn_w)
    return (outs[0], outs[1]), (outs[2], outs[3])


_QUERY_BLOCKS = ((0, CHUNK),) + tuple((r, 2 * CHUNK) for r in range(CHUNK, T_PAD, 2 * CHUNK))


def _key_tiles(nk):
    return ((PAD, N_META),) + tuple((r, 2 * CHUNK) for r in range(CHUNK, nk, 2 * CHUNK))


def _diffattn_kernel(qf_ref, kf_ref, vf_ref, qm_ref, km_ref, vm_ref, w_ref, lq1_ref, lk1_ref,
                     lq2_ref, lk2_ref, of_ref, om_ref, s_ref, p_ref, vt_ref, *, layer, lambda_init):
    row = slice(layer, layer + 1)
    lam = (jnp.exp(jnp.sum(lq1_ref[row, :] * lk1_ref[row, :], axis=-1, keepdims=True))
           - jnp.exp(jnp.sum(lq2_ref[row, :] * lk2_ref[row, :], axis=-1, keepdims=True))
           + lambda_init)
    w_out = w_ref[row, :] * (1.0 - lambda_init)
    tn = (((0,), (0,)), ((), ()))

    def rows(meta_ref, frames_ref, r0, n):
        if r0 < CHUNK:
            return meta_ref[r0:r0 + n, :]
        return frames_ref[0, r0 - CHUNK:r0 - CHUNK + n, :]

    def fold(state, key, value, op):
        state[key] = value if key not in state else op(state[key], value)

    def score_tiles(blk, slot, state):
        r0, nq = _QUERY_BLOCKS[blk]
        nk = r0 + nq
        qt = rows(qm_ref, qf_ref, r0, nq).astype(F32).T
        feat = lax.broadcasted_iota(jnp.int32, (LANES, nq), 0)
        qq = jnp.concatenate([jnp.where(feat < DIFF_D, qt, 0.0),
                              jnp.where(feat >= DIFF_D, qt, 0.0)], axis=1).astype(BF16)

        def tile(t0, tk):
            s = jnp.dot(rows(km_ref, kf_ref, t0, tk), qq, preferred_element_type=F32)
            krow = lax.broadcasted_iota(jnp.int32, (tk, 2 * nq), 0)
            if t0 + tk > r0:
                qcol = lax.broadcasted_iota(jnp.int32, (tk, 2 * nq), 1) % nq
                s = jnp.where(krow + (t0 - r0) <= qcol, s, NEG)
            s_ref[slot, t0:t0 + tk, :2 * nq] = s
            fold(state, "max", jnp.max(s, axis=0, keepdims=True), jnp.maximum)

        return [functools.partial(tile, t0, tk) for t0, tk in _key_tiles(nk)]

    def prob_tiles(blk, slot, smax, state):
        r0, nq = _QUERY_BLOCKS[blk]
        nk = r0 + nq

        def tile(t0, tk):
            p = jnp.exp2(s_ref[slot, t0:t0 + tk, :2 * nq] - smax)
            fold(state, "pos", jnp.sum(p[:, :nq], axis=0, keepdims=True), jnp.add)
            fold(state, "neg", jnp.sum(p[:, nq:], axis=0, keepdims=True), jnp.add)
            p_ref[slot, t0:t0 + tk, :2 * nq] = p.astype(BF16)

        return [functools.partial(tile, t0, tk) for t0, tk in _key_tiles(nk)]

    def finish(blk, slot, sums):
        r0, nq = _QUERY_BLOCKS[blk]
        nk = r0 + nq
        ot = lax.dot_general(vm_ref[PAD:, :], p_ref[slot, PAD:CHUNK, :2 * nq], tn,
                             preferred_element_type=F32)
        if nk > CHUNK:
            ot = ot + jnp.dot(vt_ref[:, :nk - CHUNK], p_ref[slot, CHUNK:nk, :2 * nq],
                              preferred_element_type=F32)
        o = (ot[:, :nq] * (1.0 / sums["pos"]) - ot[:, nq:] * (lam / sums["neg"])).T
        o = o * lax.rsqrt(jnp.mean(o * o, axis=-1, keepdims=True) + EPS) * w_out
        if r0 == 0:
            orow = lax.broadcasted_iota(jnp.int32, (nq, LANES), 0)
            om_ref[...] = jnp.where(orow >= PAD, o, 0.0).astype(BF16)
        else:
            of_ref[0, r0 - CHUNK:nk - CHUNK, :] = o.astype(BF16)

    n_blocks = len(_QUERY_BLOCKS)
    order = list(range(0, n_blocks, 2)) + list(range(n_blocks - 1 - n_blocks % 2, 0, -2))
    vt_ref[...] = vf_ref[0].astype(F32).T.astype(BF16)
    score_state, pending = {}, None
    for run in score_tiles(order[0], 0, score_state):
        run()
    for r, blk in enumerate(order):
        smax, sums, score_state = score_state["max"], {}, {}
        ahead = score_tiles(order[r + 1], (r + 1) % 2, score_state) if r + 1 < n_blocks else []
        current = prob_tiles(blk, r % 2, smax, sums)
        for n in range(max(len(ahead), len(current))):
            for runs in (current, ahead):
                if n < len(runs):
                    runs[n]()
            if n == min(1, len(current) - 1) and pending is not None:
                finish(*pending)
        pending = (blk, r % 2, sums)
    finish(*pending)


def _diffattn(layer, proj, subln_w, lq1, lk1, lq2, lk2):
    proj_frames, proj_meta = proj
    frames3 = proj_frames.reshape(BATCH, SEQ, DIFF_COLS)
    lambda_init = 0.8 - 0.6 * math.exp(-0.3 * layer)
    fblk = lambda c: pl.BlockSpec((1, SEQ, LANES), lambda hd, b, c=c: (b, 0, c + hd))
    mblk = lambda c: pl.BlockSpec((CHUNK, LANES), lambda hd, b, c=c: (0, c + hd))
    small = lambda n: _whole((DEPTH, n))
    y_frames, y_meta = pl.pallas_call(
        functools.partial(_diffattn_kernel, layer=layer, lambda_init=lambda_init),
        grid=(DIFF_HEADS, BATCH),
        in_specs=[fblk(0), fblk(DIFF_HEADS), fblk(2 * DIFF_HEADS),
                  mblk(0), mblk(DIFF_HEADS), mblk(2 * DIFF_HEADS),
                  small(DIFF_DV), small(DIFF_D), small(DIFF_D), small(DIFF_D), small(DIFF_D)],
        out_specs=[pl.BlockSpec((1, SEQ, LANES), lambda hd, b: (b, 0, hd)),
                   pl.BlockSpec((CHUNK, LANES), lambda hd, b: (0, hd))],
        out_shape=[jax.ShapeDtypeStruct((BATCH, SEQ, DIFF_WIDTH), BF16),
                   jax.ShapeDtypeStruct((CHUNK, DIFF_WIDTH), BF16)],
        scratch_shapes=[pltpu.VMEM((2, T_PAD, 4 * CHUNK), F32),
                        pltpu.VMEM((2, T_PAD, 4 * CHUNK), BF16),
                        pltpu.VMEM((DIFF_DV, SEQ), BF16)],
        compiler_params=pltpu.CompilerParams(
            dimension_semantics=("arbitrary", "arbitrary"), vmem_limit_bytes=VMEM_LIMIT),
        name="diffattn",
    )(frames3, frames3, frames3, proj_meta, proj_meta, proj_meta, subln_w, lq1, lk1, lq2, lk2)
    return y_frames.reshape(FRAME_ROWS, DIFF_WIDTH), y_meta


def _rotary_tables():
    f32 = np.float32
    pos = np.arange(T_PAD, dtype=f32) - f32(PAD)
    angle = (f32(RET_THETA) ** (-np.linspace(0.0, 1.0, RET_DK // 2, dtype=f32))).astype(f32)
    fr = pos[:, None] * angle[None, :]
    c, s = np.cos(fr), np.sin(fr)
    zero = np.zeros_like(s)
    cos_i = np.repeat(c, 2, axis=-1)
    sin_even = np.stack([-s, zero], axis=-1).reshape(T_PAD, RET_DK)
    sin_odd = np.stack([zero, s], axis=-1).reshape(T_PAD, RET_DK)
    ks = f32(RET_DK ** -0.5)
    inv = (f32(ROPE_THETA) ** (-np.arange(0, ROPE_DIMS, 2, dtype=f32) / f32(ROPE_DIMS))).astype(f32)
    fq = pos[:, None] * inv[None, :]
    emb = np.concatenate([fq, fq], axis=-1)
    ce, se = np.cos(emb), np.sin(emb)
    half, rest = ROPE_DIMS // 2, DIFF_D - ROPE_DIMS
    zeros = lambda n: np.zeros((T_PAD, n), f32)
    two = lambda a: np.concatenate([a, a], axis=-1)
    c_d = two(np.concatenate([ce, np.ones((T_PAD, rest), f32)], axis=-1))
    s_lo = two(np.concatenate([zeros(half), se[:, half:], zeros(rest)], axis=-1))
    s_hi = two(np.concatenate([-se[:, :half], zeros(half + rest)], axis=-1))
    qs = f32((DIFF_D ** -0.5) * math.log2(math.e))
    tabs = [cos_i, sin_even, sin_odd, cos_i * ks, sin_even * ks, sin_odd * ks,
            c_d * qs, s_lo * qs, s_hi * qs, c_d, s_lo, s_hi]
    return np.concatenate(tabs, axis=-1).astype(f32)


def _retention_consts():
    f32 = np.float32
    log_gamma = np.log(f32(1.0) - f32(2.0) ** (f32(-5.0) - np.arange(RET_HEADS, dtype=f32))).astype(f32)
    idx = np.arange(CHUNK, dtype=f32)
    rel = idx[:, None] - idx[None, :]
    dintra = np.where(rel >= 0, np.exp(log_gamma[:, None, None] * np.maximum(rel, f32(0.0))), f32(0.0))
    k_decay = np.exp(log_gamma[:, None] * (CHUNK - 1 - idx)[None, :])
    q_decay = np.exp(log_gamma[:, None] * (idx + f32(1.0))[None, :])
    cd = np.exp(log_gamma * f32(CHUNK))
    bc = lambda a: np.ascontiguousarray(np.broadcast_to(a[:, :, None], (RET_HEADS, CHUNK, LANES)))
    return cd.astype(f32), dintra.astype(f32), bc(q_decay).astype(f32), bc(k_decay).astype(f32)


def kernel(x, meta_tokens, ffn1_norm, ffn1_w_gate, ffn1_w_up, ffn1_w_down, mix_norm, w_in, ret_gn_w, diff_subln_w, diff_lambda_q1, diff_lambda_k1, diff_lambda_q2, diff_lambda_k2, w_out, ffn2_norm, ffn2_w_gate, ffn2_w_up, ffn2_w_down, final_norm):
    meta_chunk = jnp.concatenate([jnp.zeros((PAD, D_MODEL), F32), meta_tokens.astype(F32)], axis=0)
    h = (x.astype(F32).reshape(FRAME_ROWS, D_MODEL), meta_chunk)

    tab = _rotary_tables()
    tab_meta, tab_frames = jnp.asarray(tab[:CHUNK]), jnp.asarray(tab[CHUNK:])
    cd, dintra, qdec, kdec = (jnp.asarray(a) for a in _retention_consts())

    for l in range(DEPTH):
        last = l == DEPTH - 1
        h = _ffn(l, h, ffn1_norm, ffn1_w_gate, ffn1_w_up, ffn1_w_down)
        proj_diff, y_ret = _inproj_retention(l, h, mix_norm, w_in, tab_frames, tab_meta,
                                             cd, dintra, qdec, kdec, ret_gn_w)
        y_diff = _diffattn(l, proj_diff, diff_subln_w, diff_lambda_q1, diff_lambda_k1,
                           diff_lambda_q2, diff_lambda_k2)
        h = _ffn(l, h, ffn2_norm, ffn2_w_gate, ffn2_w_up, ffn2_w_down,
                 mix=(y_ret, y_diff, w_out),
                 final_g=final_norm.reshape(1, D_MODEL) if last else None)

    return h[0].reshape(BATCH, SEQ, D_MODEL).astype(x.dtype)
```

```python
import functools
import math

import numpy as np
import jax
import jax.numpy as jnp
from jax import lax
from jax.experimental import pallas as pl
from jax.experimental.pallas import tpu as pltpu

D_MODEL = 1024
BATCH = 8
SEQ = 2048
DEPTH = 2
N_META = 16
CHUNK = 128
RET_HEADS = 4
RET_DK = 128
RET_WIDTH = 512
RET_THETA = 10000.0
DIFF_HEADS = 4
DIFF_D = 64
DIFF_DV = 128
DIFF_WIDTH = 512
ROPE_THETA = 500000.0
ROPE_DIMS = 16
D_FF = 2816
EPS = 1e-6
D_IN = 3584

PAD = CHUNK - N_META
T_PAD = CHUNK + SEQ
FRAME_ROWS = BATCH * SEQ
N_TAB = 12
LANES = 128
VMEM_LIMIT = 56 * 1024 * 1024
NEG = -1e30

F32 = jnp.float32
BF16 = jnp.bfloat16


def _rms(x, g):
    return x * lax.rsqrt(jnp.mean(x * x, axis=-1, keepdims=True) + EPS) * g


def _whole(shape):
    return pl.BlockSpec(shape, lambda *_: (0,) * len(shape))


class _DenseGrid:
    def __init__(self, tile, stage, meta):
        self.tile, self.stage, self.meta = tile, stage, meta
        self.n_tiles = FRAME_ROWS // tile
        self.steps = stage + self.n_tiles + int(meta)

    def tile_index(self, i):
        return jnp.clip(i - self.stage, 0, self.n_tiles - 1)

    def row_tile(self, width):
        return pl.BlockSpec((self.tile, width), lambda i: (self.tile_index(i), 0))

    def weight_chunk(self, layer, rows, cols):
        return pl.BlockSpec((None, rows // self.stage, cols),
                            lambda i: (layer, jnp.minimum(i, self.stage - 1), 0))

    def run(self, i, stage, frames, meta):
        pl.when(i < self.stage)(stage)
        pl.when((i >= self.stage) & (i < self.stage + self.n_tiles))(frames)
        if self.meta:
            pl.when(i == self.stage + self.n_tiles)(meta)


def _stage_weight(i, src_ref, dst_ref):
    rows = src_ref.shape[0]
    dst_ref[pl.ds(pl.multiple_of(i * rows, 16), rows), :] = src_ref[...].astype(BF16)


FFN_TILE = 512
FFN_SUB_TILE = 256
STAGE_STEPS = 8
PROJ_TILE = 512


def _ffn_kernel(*refs, grid, layer, mix, final):
    refs = list(refs)
    take = lambda n: [refs.pop(0) for _ in range(n)]
    with_meta = not final
    h_refs = take(2 if with_meta else 1)
    if mix:
        yr_refs, yd_refs = take(len(h_refs)), take(len(h_refs))
        (wo_ref,) = take(1)
    g_ref, wg_ref, wu_ref, wd_ref = take(4)
    if final:
        (fn_ref,) = take(1)
    o_refs = take(len(h_refs))
    wg_s, wu_s, wd_s = take(3)
    if mix:
        (wo_s,) = take(1)
    i = pl.program_id(0)

    def stage():
        _stage_weight(i, wg_ref, wg_s)
        _stage_weight(i, wu_ref, wu_s)
        _stage_weight(i, wd_ref, wd_s)
        if mix:
            _stage_weight(i, wo_ref, wo_s)

    def rows(which):
        n = h_refs[which].shape[0]
        parts = [slice(r, min(r + FFN_SUB_TILE, n)) for r in range(0, n, FFN_SUB_TILE)]
        xs, xns, acts = [], [], []
        for sl in parts:
            x = h_refs[which][sl, :]
            if mix:
                x = (x + jnp.dot(yr_refs[which][sl, :], wo_s[:RET_WIDTH, :],
                                 preferred_element_type=F32)
                     + jnp.dot(yd_refs[which][sl, :], wo_s[RET_WIDTH:, :],
                               preferred_element_type=F32))
            xs.append(x)
            xns.append(_rms(x, g_ref[layer:layer + 1, :]).astype(BF16))
        for xn in xns:
            gate = jnp.dot(xn, wg_s[...], preferred_element_type=F32)
            up = jnp.dot(xn, wu_s[...], preferred_element_type=F32)
            acts.append((jax.nn.silu(gate) * up).astype(BF16))
        for sl, x, act in zip(parts, xs, acts):
            y = x + 0.5 * jnp.dot(act, wd_s[...], preferred_element_type=F32)
            if final:
                y = _rms(y, fn_ref[...])
            o_refs[which][sl, :] = y

    grid.run(i, stage, functools.partial(rows, 0), functools.partial(rows, 1))


def _ffn(layer, h, norm_g, wg, wu, wd, mix=None, final_g=None):
    with_meta = final_g is None
    grid = _DenseGrid(tile=FFN_TILE, stage=STAGE_STEPS, meta=with_meta)
    _row_tile, _weight_chunk = grid.row_tile, grid.weight_chunk
    pair = lambda width: [_row_tile(width)] + ([_whole((CHUNK, width))] if with_meta else [])
    keep = lambda arrays: list(arrays) if with_meta else [arrays[0]]
    args, specs = keep(h), pair(D_MODEL)
    scratch = [pltpu.VMEM((D_MODEL, D_FF), BF16), pltpu.VMEM((D_MODEL, D_FF), BF16),
               pltpu.VMEM((D_FF, D_MODEL), BF16)]
    if mix is not None:
        y_ret, y_diff, w_out = mix
        args += keep(y_ret) + keep(y_diff) + [w_out]
        specs += pair(RET_WIDTH) + pair(DIFF_WIDTH) + [_weight_chunk(layer, D_MODEL, D_MODEL)]
        scratch.append(pltpu.VMEM((D_MODEL, D_MODEL), BF16))
    args += [norm_g, wg, wu, wd]
    specs += [_whole((DEPTH, D_MODEL)), _weight_chunk(layer, D_MODEL, D_FF),
              _weight_chunk(layer, D_MODEL, D_FF), _weight_chunk(layer, D_FF, D_MODEL)]
    if final_g is not None:
        args.append(final_g)
        specs.append(_whole((1, D_MODEL)))
    out_shape = [jax.ShapeDtypeStruct((FRAME_ROWS, D_MODEL), F32)]
    if with_meta:
        out_shape.append(jax.ShapeDtypeStruct((CHUNK, D_MODEL), F32))
    return pl.pallas_call(
        functools.partial(_ffn_kernel, grid=grid, layer=layer, mix=mix is not None,
                          final=final_g is not None),
        grid=(grid.steps,),
        in_specs=specs,
        out_specs=pair(D_MODEL),
        out_shape=out_shape,
        scratch_shapes=scratch,
        compiler_params=pltpu.CompilerParams(
            dimension_semantics=("arbitrary",), vmem_limit_bytes=VMEM_LIMIT),
        name="ffn",
    )(*args)


_ROTATIONS = (
    (0, 0, LANES - 1, 1),
    (RET_WIDTH, 3, LANES - 1, 1),
    (4 * RET_WIDTH, 6, 8, LANES - 8),
    (4 * RET_WIDTH + DIFF_WIDTH, 9, 8, LANES - 8),
)


RET_COLS = 4 * RET_WIDTH
DIFF_COLS = D_IN - RET_COLS
N_PROJ_TILES = FRAME_ROWS // PROJ_TILE
TILES_PER_BATCH = SEQ // PROJ_TILE


def _inproj_retention_kernel(cd_ref, hf_ref, hm_ref, g_ref, w_ref, tabf_ref, tabm_ref,
                             dintra_ref, qdec_ref, kdec_ref, gnw_ref,
                             pf_ref, pm_ref, yf_ref, ym_ref,
                             w_s, ring_ref, state_ref, meta_state_ref, *, layer):
    nt = (((1,), (1,)), ((), ()))
    tn = (((0,), (0,)), ((), ()))
    i = pl.program_id(0)
    t = i - (STAGE_STEPS + 1)

    def project_steps(h_ref, tab_ref, ret_ref, diff_ref):
        rotation = {base: rest for base, *rest in _ROTATIONS}
        cell = {}

        def tab(k):
            return tab_ref[:, k * LANES:(k + 1) * LANES]

        def group(base):
            if not cell:
                cell["xn"] = _rms(h_ref[...], g_ref[layer:layer + 1, :]).astype(BF16)
            p = jnp.dot(cell["xn"], w_s[:, base:base + RET_WIDTH], preferred_element_type=F32)
            dst, lo = (ret_ref, base) if base < RET_COLS else (diff_ref, base - RET_COLS)
            if base in rotation:
                t0, roll_a, roll_b = rotation[base]
                for hd in range(RET_HEADS):
                    xs = p[:, hd * LANES:(hd + 1) * LANES]
                    rot = (xs * tab(t0) + pltpu.roll(xs, roll_a, 1) * tab(t0 + 1)
                           + pltpu.roll(xs, roll_b, 1) * tab(t0 + 2))
                    dst[:, lo + hd * LANES:lo + (hd + 1) * LANES] = rot.astype(BF16)
            else:
                dst[:, lo:lo + RET_WIDTH] = p.astype(BF16)

        return [functools.partial(group, base) for base in range(0, D_IN, RET_WIDTH)]

    def retention_steps(src_ref, y_ref, state_in, state_out_refs):
        n_chunks = src_ref.shape[0] // CHUNK
        units = [(c, hd) for c in range(n_chunks) for hd in range(RET_HEADS)]
        scores, incs, outs, cell = {}, {}, {}, {}

        def col(c, hd, which):
            return src_ref[c * CHUNK:(c + 1) * CHUNK,
                           which * RET_WIDTH + hd * LANES:which * RET_WIDTH + (hd + 1) * LANES]

        def score(c, hd):
            s = lax.dot_general(col(c, hd, 0), col(c, hd, 1), nt, preferred_element_type=F32)
            scores[c, hd] = (s * dintra_ref[hd]).astype(BF16)

        def increment(c, hd):
            kd = (col(c, hd, 1).astype(F32) * kdec_ref[hd]).astype(BF16)
            incs[c, hd] = lax.dot_general(kd, col(c, hd, 2), tn, preferred_element_type=F32)

        def output(c, hd):
            if not cell:
                cell["state"] = state_in()
            state = cell["state"]
            if state is None:
                outs[c, hd] = jnp.dot(scores[c, hd], col(c, hd, 2), preferred_element_type=F32)
            else:
                qd = (col(c, hd, 0).astype(F32) * qdec_ref[hd]).astype(BF16)
                outs[c, hd] = jnp.dot(jnp.concatenate([scores[c, hd], qd], axis=1),
                                      jnp.concatenate([col(c, hd, 2), state[hd].astype(BF16)],
                                                      axis=0),
                                      preferred_element_type=F32)
            if hd == RET_HEADS - 1:
                if state is None:
                    cell["state"] = [incs[c, h] for h in range(RET_HEADS)]
                else:
                    cell["state"] = [state[h] * cd_ref[h] + incs[c, h] for h in range(RET_HEADS)]

        def store_state():
            for hd in range(RET_HEADS):
                for ref in state_out_refs:
                    ref[hd] = cell["state"][hd]

        def norm_gate(c, hd):
            o = outs[c, hd]
            mu = jnp.mean(o, axis=-1, keepdims=True)
            d = o - mu
            var = jnp.mean(d * d, axis=-1, keepdims=True)
            on = d * lax.rsqrt(var + EPS) * gnw_ref[layer:layer + 1, hd * LANES:(hd + 1) * LANES]
            y_ref[c * CHUNK:(c + 1) * CHUNK, hd * LANES:(hd + 1) * LANES] = (
                jax.nn.silu(col(c, hd, 3).astype(F32)) * on).astype(BF16)

        return ([functools.partial(stage, c, hd) for stage in (score, increment, output)
                 for c, hd in units] + [store_state]
                + [functools.partial(norm_gate, c, hd) for c, hd in units])

    def previous_tile_retention_steps():
        def state_in():
            first = (t + TILES_PER_BATCH - 1) % TILES_PER_BATCH == 0
            return [jnp.where(first, meta_state_ref[hd], state_ref[hd])
                    for hd in range(RET_HEADS)]

        return retention_steps(ring_ref.at[(t + 1) % 2], yf_ref, state_in, [state_ref])

    def emit(major, minor=()):
        per = -(-len(minor) // len(major))
        for k, step in enumerate(major):
            step()
            for extra in minor[k * per:(k + 1) * per]:
                extra()

    @pl.when(i < STAGE_STEPS)
    def _():
        _stage_weight(i, w_ref, w_s)

    @pl.when(i == STAGE_STEPS)
    def _():
        ring_ref[...] = jnp.zeros_like(ring_ref)
        state_ref[...] = jnp.zeros_like(state_ref)
        meta_ring = ring_ref.at[0, :CHUNK]
        emit(project_steps(hm_ref, tabm_ref, meta_ring, pm_ref))
        emit(retention_steps(meta_ring, ym_ref, lambda: None, [meta_state_ref]))

    @pl.when((t >= 0) & (t < N_PROJ_TILES))
    def _():
        steps = previous_tile_retention_steps()
        n_norm = (PROJ_TILE // CHUNK) * RET_HEADS
        emit(steps[:-n_norm])
        emit(project_steps(hf_ref, tabf_ref, ring_ref.at[t % 2], pf_ref), steps[-n_norm:])

    @pl.when(t == N_PROJ_TILES)
    def _():
        emit(previous_tile_retention_steps())


def _inproj_retention(layer, h, norm_g, w_in, tab_frames, tab_meta, cd, dintra, qdec, kdec, gn_w):
    steps = STAGE_STEPS + 1 + N_PROJ_TILES + 1
    tile = lambda i: jnp.clip(i - (STAGE_STEPS + 1), 0, N_PROJ_TILES - 1)
    lag = lambda i: jnp.clip(i - (STAGE_STEPS + 2), 0, N_PROJ_TILES - 1)
    const3 = _whole((RET_HEADS, CHUNK, LANES))
    outs = pl.pallas_call(
        functools.partial(_inproj_retention_kernel, layer=layer),
        grid=(steps,),
        in_specs=[pl.BlockSpec(memory_space=pltpu.SMEM),
                  pl.BlockSpec((PROJ_TILE, D_MODEL), lambda i: (tile(i), 0)),
                  _whole((CHUNK, D_MODEL)), _whole((DEPTH, D_MODEL)),
                  pl.BlockSpec((None, D_MODEL // STAGE_STEPS, D_IN),
                               lambda i: (layer, jnp.minimum(i, STAGE_STEPS - 1), 0)),
                  pl.BlockSpec((PROJ_TILE, N_TAB * LANES),
                               lambda i: (tile(i) % TILES_PER_BATCH, 0)),
                  _whole((CHUNK, N_TAB * LANES)),
                  const3, const3, const3, _whole((DEPTH, RET_WIDTH))],
        out_specs=[pl.BlockSpec((PROJ_TILE, DIFF_COLS), lambda i: (tile(i), 0)),
                   _whole((CHUNK, DIFF_COLS)),
                   pl.BlockSpec((PROJ_TILE, RET_WIDTH), lambda i: (lag(i), 0)),
                   _whole((CHUNK, RET_WIDTH))],
        out_shape=[jax.ShapeDtypeStruct((FRAME_ROWS, DIFF_COLS), BF16),
                   jax.ShapeDtypeStruct((CHUNK, DIFF_COLS), BF16),
                   jax.ShapeDtypeStruct((FRAME_ROWS, RET_WIDTH), BF16),
                   jax.ShapeDtypeStruct((CHUNK, RET_WIDTH), BF16)],
        scratch_shapes=[pltpu.VMEM((D_MODEL, D_IN), BF16),
                        pltpu.VMEM((2, PROJ_TILE, RET_COLS), BF16),
                        pltpu.VMEM((RET_HEADS, RET_DK, LANES), F32),
                        pltpu.VMEM((RET_HEADS, RET_DK, LANES), F32)],
        compiler_params=pltpu.CompilerParams(
            dimension_semantics=("arbitrary",), vmem_limit_bytes=VMEM_LIMIT),
        name="inproj_retention",
    )(cd, h[0], h[1], norm_g, w_in, tab_frames, tab_meta, dintra, qdec, kdec, gn_w)
    return (outs[0], outs[1]), (outs[2], outs[3])


_QUERY_BLOCKS = ((0, CHUNK),) + tuple((r, 2 * CHUNK) for r in range(CHUNK, T_PAD, 2 * CHUNK))


def _key_tiles(nk):
    full = tuple((r, 2 * CHUNK) for r in range(CHUNK, nk - 2 * CHUNK, 2 * CHUNK))
    diagonal = ((nk - 2 * CHUNK, CHUNK), (nk - CHUNK, CHUNK)) if nk > CHUNK else ()
    return ((PAD, N_META),) + full + diagonal


def _diffattn_kernel(qf_ref, kf_ref, vf_ref, qm_ref, km_ref, vm_ref, w_ref, lq1_ref, lk1_ref,
                     lq2_ref, lk2_ref, of_ref, om_ref, s_ref, p_ref, vt_ref, *, layer, lambda_init):
    row = slice(layer, layer + 1)
    lam = (jnp.exp(jnp.sum(lq1_ref[row, :] * lk1_ref[row, :], axis=-1, keepdims=True))
           - jnp.exp(jnp.sum(lq2_ref[row, :] * lk2_ref[row, :], axis=-1, keepdims=True))
           + lambda_init)
    w_out = w_ref[row, :] * (1.0 - lambda_init)
    tn = (((0,), (0,)), ((), ()))

    def rows(meta_ref, frames_ref, r0, n):
        if r0 < CHUNK:
            return meta_ref[r0:r0 + n, :]
        return frames_ref[0, r0 - CHUNK:r0 - CHUNK + n, :]

    def fold(state, key, value, op):
        state[key] = value if key not in state else op(state[key], value)

    def score_tiles(blk, slot, state):
        r0, nq = _QUERY_BLOCKS[blk]
        nk = r0 + nq
        qt = rows(qm_ref, qf_ref, r0, nq).astype(F32).T
        feat = lax.broadcasted_iota(jnp.int32, (LANES, nq), 0)
        qq = jnp.concatenate([jnp.where(feat < DIFF_D, qt, 0.0),
                              jnp.where(feat >= DIFF_D, qt, 0.0)], axis=1).astype(BF16)

        def tile(t0, tk):
            keys = rows(km_ref, kf_ref, t0, tk)
            if r0 >= CHUNK and t0 > r0:
                late = jnp.concatenate([qq[:, nq // 2:nq], qq[:, nq + nq // 2:]], axis=1)
                s = jnp.dot(keys, late, preferred_element_type=F32)
                krow = lax.broadcasted_iota(jnp.int32, (tk, nq), 0)
                qcol = lax.broadcasted_iota(jnp.int32, (tk, nq), 1) % (nq // 2)
                s = jnp.where(krow <= qcol, s, NEG)
                hidden = jnp.full((tk, nq // 2), NEG, F32)
                s = jnp.concatenate([hidden, s[:, :nq // 2], hidden, s[:, nq // 2:]], axis=1)
            else:
                s = jnp.dot(keys, qq, preferred_element_type=F32)
                if t0 + tk > r0:
                    krow = lax.broadcasted_iota(jnp.int32, (tk, 2 * nq), 0)
                    qcol = lax.broadcasted_iota(jnp.int32, (tk, 2 * nq), 1) % nq
                    s = jnp.where(krow + (t0 - r0) <= qcol, s, NEG)
            s_ref[slot, t0:t0 + tk, :2 * nq] = s
            fold(state, "max", jnp.max(s, axis=0, keepdims=True), jnp.maximum)

        return [functools.partial(tile, t0, tk) for t0, tk in _key_tiles(nk)]

    def prob_tiles(blk, slot, smax, state):
        r0, nq = _QUERY_BLOCKS[blk]
        nk = r0 + nq

        def tile(t0, tk):
            p = jnp.exp2(s_ref[slot, t0:t0 + tk, :2 * nq] - smax)
            fold(state, "pos", jnp.sum(p[:, :nq], axis=0, keepdims=True), jnp.add)
            fold(state, "neg", jnp.sum(p[:, nq:], axis=0, keepdims=True), jnp.add)
            p_ref[slot, t0:t0 + tk, :2 * nq] = p.astype(BF16)

        return [functools.partial(tile, t0, tk) for t0, tk in _key_tiles(nk)]

    def finish(blk, slot, sums):
        r0, nq = _QUERY_BLOCKS[blk]
        nk = r0 + nq
        ot = lax.dot_general(vm_ref[PAD:, :], p_ref[slot, PAD:CHUNK, :2 * nq], tn,
                             preferred_element_type=F32)
        if nk > CHUNK:
            ot = ot + jnp.dot(vt_ref[:, :nk - CHUNK], p_ref[slot, CHUNK:nk, :2 * nq],
                              preferred_element_type=F32)
        o = (ot[:, :nq] * (1.0 / sums["pos"]) - ot[:, nq:] * (lam / sums["neg"])).T
        o = o * lax.rsqrt(jnp.mean(o * o, axis=-1, keepdims=True) + EPS) * w_out
        if r0 == 0:
            orow = lax.broadcasted_iota(jnp.int32, (nq, LANES), 0)
            om_ref[...] = jnp.where(orow >= PAD, o, 0.0).astype(BF16)
        else:
            of_ref[0, r0 - CHUNK:nk - CHUNK, :] = o.astype(BF16)

    n_blocks = len(_QUERY_BLOCKS)
    order = list(range(0, n_blocks, 2)) + list(range(n_blocks - 1 - n_blocks % 2, 0, -2))
    vt_ref[...] = vf_ref[0].astype(F32).T.astype(BF16)
    score_state, pending = {}, None
    for run in score_tiles(order[0], 0, score_state):
        run()
    for r, blk in enumerate(order):
        smax, sums, score_state = score_state["max"], {}, {}
        ahead = score_tiles(order[r + 1], (r + 1) % 2, score_state) if r + 1 < n_blocks else []
        current = prob_tiles(blk, r % 2, smax, sums)
        for n in range(max(len(ahead), len(current))):
            for runs in (current, ahead):
                if n < len(runs):
                    runs[n]()
            if n == 0 and pending is not None:
                finish(*pending)
        pending = (blk, r % 2, sums)
    finish(*pending)


def _diffattn(layer, proj, subln_w, lq1, lk1, lq2, lk2):
    proj_frames, proj_meta = proj
    frames3 = proj_frames.reshape(BATCH, SEQ, DIFF_COLS)
    lambda_init = 0.8 - 0.6 * math.exp(-0.3 * layer)
    fblk = lambda c: pl.BlockSpec((1, SEQ, LANES), lambda hd, b, c=c: (b, 0, c + hd))
    mblk = lambda c: pl.BlockSpec((CHUNK, LANES), lambda hd, b, c=c: (0, c + hd))
    small = lambda n: _whole((DEPTH, n))
    y_frames, y_meta = pl.pallas_call(
        functools.partial(_diffattn_kernel, layer=layer, lambda_init=lambda_init),
        grid=(DIFF_HEADS, BATCH),
        in_specs=[fblk(0), fblk(DIFF_HEADS), fblk(2 * DIFF_HEADS),
                  mblk(0), mblk(DIFF_HEADS), mblk(2 * DIFF_HEADS),
                  small(DIFF_DV), small(DIFF_D), small(DIFF_D), small(DIFF_D), small(DIFF_D)],
        out_specs=[pl.BlockSpec((1, SEQ, LANES), lambda hd, b: (b, 0, hd)),
                   pl.BlockSpec((CHUNK, LANES), lambda hd, b: (0, hd))],
        out_shape=[jax.ShapeDtypeStruct((BATCH, SEQ, DIFF_WIDTH), BF16),
                   jax.ShapeDtypeStruct((CHUNK, DIFF_WIDTH), BF16)],
        scratch_shapes=[pltpu.VMEM((2, T_PAD, 4 * CHUNK), F32),
                        pltpu.VMEM((2, T_PAD, 4 * CHUNK), BF16),
                        pltpu.VMEM((DIFF_DV, SEQ), BF16)],
        compiler_params=pltpu.CompilerParams(
            dimension_semantics=("arbitrary", "arbitrary"), vmem_limit_bytes=VMEM_LIMIT),
        name="diffattn",
    )(frames3, frames3, frames3, proj_meta, proj_meta, proj_meta, subln_w, lq1, lk1, lq2, lk2)
    return y_frames.reshape(FRAME_ROWS, DIFF_WIDTH), y_meta


def _rotary_tables():
    f32 = np.float32
    pos = np.arange(T_PAD, dtype=f32) - f32(PAD)
    angle = (f32(RET_THETA) ** (-np.linspace(0.0, 1.0, RET_DK // 2, dtype=f32))).astype(f32)
    fr = pos[:, None] * angle[None, :]
    c, s = np.cos(fr), np.sin(fr)
    zero = np.zeros_like(s)
    cos_i = np.repeat(c, 2, axis=-1)
    sin_even = np.stack([-s, zero], axis=-1).reshape(T_PAD, RET_DK)
    sin_odd = np.stack([zero, s], axis=-1).reshape(T_PAD, RET_DK)
    ks = f32(RET_DK ** -0.5)
    inv = (f32(ROPE_THETA) ** (-np.arange(0, ROPE_DIMS, 2, dtype=f32) / f32(ROPE_DIMS))).astype(f32)
    fq = pos[:, None] * inv[None, :]
    emb = np.concatenate([fq, fq], axis=-1)
    ce, se = np.cos(emb), np.sin(emb)
    half, rest = ROPE_DIMS // 2, DIFF_D - ROPE_DIMS
    zeros = lambda n: np.zeros((T_PAD, n), f32)
    two = lambda a: np.concatenate([a, a], axis=-1)
    c_d = two(np.concatenate([ce, np.ones((T_PAD, rest), f32)], axis=-1))
    s_lo = two(np.concatenate([zeros(half), se[:, half:], zeros(rest)], axis=-1))
    s_hi = two(np.concatenate([-se[:, :half], zeros(half + rest)], axis=-1))
    qs = f32((DIFF_D ** -0.5) * math.log2(math.e))
    tabs = [cos_i, sin_even, sin_odd, cos_i * ks, sin_even * ks, sin_odd * ks,
            c_d * qs, s_lo * qs, s_hi * qs, c_d, s_lo, s_hi]
    return np.concatenate(tabs, axis=-1).astype(f32)


def _retention_consts():
    f32 = np.float32
    log_gamma = np.log(f32(1.0) - f32(2.0) ** (f32(-5.0) - np.arange(RET_HEADS, dtype=f32))).astype(f32)
    idx = np.arange(CHUNK, dtype=f32)
    rel = idx[:, None] - idx[None, :]
    dintra = np.where(rel >= 0, np.exp(log_gamma[:, None, None] * np.maximum(rel, f32(0.0))), f32(0.0))
    k_decay = np.exp(log_gamma[:, None] * (CHUNK - 1 - idx)[None, :])
    q_decay = np.exp(log_gamma[:, None] * (idx + f32(1.0))[None, :])
    cd = np.exp(log_gamma * f32(CHUNK))
    bc = lambda a: np.ascontiguousarray(np.broadcast_to(a[:, :, None], (RET_HEADS, CHUNK, LANES)))
    return cd.astype(f32), dintra.astype(f32), bc(q_decay).astype(f32), bc(k_decay).astype(f32)


def kernel(x, meta_tokens, ffn1_norm, ffn1_w_gate, ffn1_w_up, ffn1_w_down, mix_norm, w_in, ret_gn_w, diff_subln_w, diff_lambda_q1, diff_lambda_k1, diff_lambda_q2, diff_lambda_k2, w_out, ffn2_norm, ffn2_w_gate, ffn2_w_up, ffn2_w_down, final_norm):
    meta_chunk = jnp.concatenate([jnp.zeros((PAD, D_MODEL), F32), meta_tokens.astype(F32)], axis=0)
    h = (x.astype(F32).reshape(FRAME_ROWS, D_MODEL), meta_chunk)

    tab = _rotary_tables()
    tab_meta, tab_frames = jnp.asarray(tab[:CHUNK]), jnp.asarray(tab[CHUNK:])
    cd, dintra, qdec, kdec = (jnp.asarray(a) for a in _retention_consts())

    for l in range(DEPTH):
        last = l == DEPTH - 1
        h = _ffn(l, h, ffn1_norm, ffn1_w_gate, ffn1_w_up, ffn1_w_down)
        proj_diff, y_ret = _inproj_retention(l, h, mix_norm, w_in, tab_frames, tab_meta,
                                             cd, dintra, qdec, kdec, ret_gn_w)
        y_diff = _diffattn(l, proj_diff, diff_subln_w, diff_lambda_q1, diff_lambda_k1,
                           diff_lambda_q2, diff_lambda_k2)
        h = _ffn(l, h, ffn2_norm, ffn2_w_gate, ffn2_w_up, ffn2_w_down,
                 mix=(y_ret, y_diff, w_out),
                 final_g=final_norm.reshape(1, D_MODEL) if last else None)

    return h[0].reshape(BATCH, SEQ, D_MODEL).astype(x.dtype)
```

```python
import functools
import math

import numpy as np
import jax
import jax.numpy as jnp
from jax import lax
from jax.experimental import pallas as pl
from jax.experimental.pallas import tpu as pltpu

D_MODEL = 1024
BATCH = 8
SEQ = 2048
DEPTH = 2
N_META = 16
CHUNK = 128
RET_HEADS = 4
RET_DK = 128
RET_WIDTH = 512
RET_THETA = 10000.0
DIFF_HEADS = 4
DIFF_D = 64
DIFF_DV = 128
DIFF_WIDTH = 512
ROPE_THETA = 500000.0
ROPE_DIMS = 16
D_FF = 2816
EPS = 1e-6
D_IN = 3584

PAD = CHUNK - N_META
T_PAD = CHUNK + SEQ
FRAME_ROWS = BATCH * SEQ
N_TAB = 12
LANES = 128
VMEM_LIMIT = 56 * 1024 * 1024
NEG = -1e30

F32 = jnp.float32
BF16 = jnp.bfloat16


def _rms(x, g):
    return x * lax.rsqrt(jnp.mean(x * x, axis=-1, keepdims=True) + EPS) * g


def _whole(shape):
    return pl.BlockSpec(shape, lambda *_: (0,) * len(shape))


class _DenseGrid:
    def __init__(self, tile, stage, meta):
        self.tile, self.stage, self.meta = tile, stage, meta
        self.n_tiles = FRAME_ROWS // tile
        self.steps = stage + self.n_tiles + int(meta)

    def tile_index(self, i):
        return jnp.clip(i - self.stage, 0, self.n_tiles - 1)

    def row_tile(self, width):
        return pl.BlockSpec((self.tile, width), lambda i: (self.tile_index(i), 0))

    def weight_chunk(self, layer, rows, cols):
        return pl.BlockSpec((None, rows // self.stage, cols),
                            lambda i: (layer, jnp.minimum(i, self.stage - 1), 0))

    def run(self, i, stage, frames, meta):
        pl.when(i < self.stage)(stage)
        pl.when((i >= self.stage) & (i < self.stage + self.n_tiles))(frames)
        if self.meta:
            pl.when(i == self.stage + self.n_tiles)(meta)


def _stage_weight(i, src_ref, dst_ref):
    rows = src_ref.shape[0]
    dst_ref[pl.ds(pl.multiple_of(i * rows, 16), rows), :] = src_ref[...].astype(BF16)


FFN_TILE = 512
FFN_SUB_TILE = 256
STAGE_STEPS = 8
PROJ_TILE = 512


def _ffn_kernel(*refs, grid, layer, mix, final):
    refs = list(refs)
    take = lambda n: [refs.pop(0) for _ in range(n)]
    with_meta = not final
    h_refs = take(2 if with_meta else 1)
    if mix:
        yr_refs, yd_refs = take(len(h_refs)), take(len(h_refs))
        (wo_ref,) = take(1)
    g_ref, wg_ref, wu_ref, wd_ref = take(4)
    if final:
        (fn_ref,) = take(1)
    o_refs = take(len(h_refs))
    wg_s, wu_s, wd_s = take(3)
    if mix:
        (wo_s,) = take(1)
    i = pl.program_id(0)

    def stage():
        _stage_weight(i, wg_ref, wg_s)
        _stage_weight(i, wu_ref, wu_s)
        _stage_weight(i, wd_ref, wd_s)
        if mix:
            _stage_weight(i, wo_ref, wo_s)

    def rows(which):
        n = h_refs[which].shape[0]
        parts = [slice(r, min(r + FFN_SUB_TILE, n)) for r in range(0, n, FFN_SUB_TILE)]
        xs, xns, acts = [], [], []
        for sl in parts:
            x = h_refs[which][sl, :]
            if mix:
                x = (x + jnp.dot(yr_refs[which][sl, :], wo_s[:RET_WIDTH, :],
                                 preferred_element_type=F32)
                     + jnp.dot(yd_refs[which][sl, :], wo_s[RET_WIDTH:, :],
                               preferred_element_type=F32))
            xs.append(x)
            xns.append(_rms(x, g_ref[layer:layer + 1, :]).astype(BF16))
        for xn in xns:
            gate = jnp.dot(xn, wg_s[...], preferred_element_type=F32)
            up = jnp.dot(xn, wu_s[...], preferred_element_type=F32)
            acts.append((jax.nn.silu(gate) * up).astype(BF16))
        for sl, x, act in zip(parts, xs, acts):
            y = x + 0.5 * jnp.dot(act, wd_s[...], preferred_element_type=F32)
            if final:
                y = _rms(y, fn_ref[...])
            o_refs[which][sl, :] = y

    grid.run(i, stage, functools.partial(rows, 0), functools.partial(rows, 1))


def _ffn(layer, h, norm_g, wg, wu, wd, mix=None, final_g=None):
    with_meta = final_g is None
    grid = _DenseGrid(tile=FFN_TILE, stage=STAGE_STEPS, meta=with_meta)
    _row_tile, _weight_chunk = grid.row_tile, grid.weight_chunk
    pair = lambda width: [_row_tile(width)] + ([_whole((CHUNK, width))] if with_meta else [])
    keep = lambda arrays: list(arrays) if with_meta else [arrays[0]]
    args, specs = keep(h), pair(D_MODEL)
    scratch = [pltpu.VMEM((D_MODEL, D_FF), BF16), pltpu.VMEM((D_MODEL, D_FF), BF16),
               pltpu.VMEM((D_FF, D_MODEL), BF16)]
    if mix is not None:
        y_ret, y_diff, w_out = mix
        args += keep(y_ret) + keep(y_diff) + [w_out]
        specs += pair(RET_WIDTH) + pair(DIFF_WIDTH) + [_weight_chunk(layer, D_MODEL, D_MODEL)]
        scratch.append(pltpu.VMEM((D_MODEL, D_MODEL), BF16))
    args += [norm_g, wg, wu, wd]
    specs += [_whole((DEPTH, D_MODEL)), _weight_chunk(layer, D_MODEL, D_FF),
              _weight_chunk(layer, D_MODEL, D_FF), _weight_chunk(layer, D_FF, D_MODEL)]
    if final_g is not None:
        args.append(final_g)
        specs.append(_whole((1, D_MODEL)))
    out_shape = [jax.ShapeDtypeStruct((FRAME_ROWS, D_MODEL), F32)]
    if with_meta:
        out_shape.append(jax.ShapeDtypeStruct((CHUNK, D_MODEL), F32))
    return pl.pallas_call(
        functools.partial(_ffn_kernel, grid=grid, layer=layer, mix=mix is not None,
                          final=final_g is not None),
        grid=(grid.steps,),
        in_specs=specs,
        out_specs=pair(D_MODEL),
        out_shape=out_shape,
        scratch_shapes=scratch,
        compiler_params=pltpu.CompilerParams(
            dimension_semantics=("arbitrary",), vmem_limit_bytes=VMEM_LIMIT),
        name="ffn",
    )(*args)


_ROTATIONS = (
    (0, 0, LANES - 1, 1),
    (RET_WIDTH, 3, LANES - 1, 1),
    (4 * RET_WIDTH, 6, 8, LANES - 8),
    (4 * RET_WIDTH + DIFF_WIDTH, 9, 8, LANES - 8),
)


RET_COLS = 4 * RET_WIDTH
DIFF_COLS = D_IN - RET_COLS
N_PROJ_TILES = FRAME_ROWS // PROJ_TILE
TILES_PER_BATCH = SEQ // PROJ_TILE


def _inproj_retention_kernel(cd_ref, hf_ref, hm_ref, g_ref, w_ref, tabf_ref, tabm_ref,
                             dintra_ref, qdec_ref, kdec_ref, gnw_ref,
                             pf_ref, pm_ref, yf_ref, ym_ref,
                             w_s, ring_ref, state_ref, meta_state_ref, *, layer):
    nt = (((1,), (1,)), ((), ()))
    tn = (((0,), (0,)), ((), ()))
    i = pl.program_id(0)
    t = i - (STAGE_STEPS + 1)

    def project_steps(h_ref, tab_ref, ret_ref, diff_ref):
        rotation = {base: rest for base, *rest in _ROTATIONS}
        cell = {}

        def tab(k):
            return tab_ref[:, k * LANES:(k + 1) * LANES]

        def group(base):
            if not cell:
                cell["xn"] = _rms(h_ref[...], g_ref[layer:layer + 1, :]).astype(BF16)
            p = jnp.dot(cell["xn"], w_s[:, base:base + RET_WIDTH], preferred_element_type=F32)
            dst, lo = (ret_ref, base) if base < RET_COLS else (diff_ref, base - RET_COLS)
            if base in rotation:
                t0, roll_a, roll_b = rotation[base]
                for hd in range(RET_HEADS):
                    xs = p[:, hd * LANES:(hd + 1) * LANES]
                    rot = (xs * tab(t0) + pltpu.roll(xs, roll_a, 1) * tab(t0 + 1)
                           + pltpu.roll(xs, roll_b, 1) * tab(t0 + 2))
                    dst[:, lo + hd * LANES:lo + (hd + 1) * LANES] = rot.astype(BF16)
            else:
                dst[:, lo:lo + RET_WIDTH] = p.astype(BF16)

        return [functools.partial(group, base) for base in range(0, D_IN, RET_WIDTH)]

    def retention_steps(src_ref, y_ref, state_in, state_out_refs):
        n_chunks = src_ref.shape[0] // CHUNK
        units = [(c, hd) for c in range(n_chunks) for hd in range(RET_HEADS)]
        scores, incs, outs, cell = {}, {}, {}, {}

        def col(c, hd, which):
            return src_ref[c * CHUNK:(c + 1) * CHUNK,
                           which * RET_WIDTH + hd * LANES:which * RET_WIDTH + (hd + 1) * LANES]

        def score(c, hd):
            if hd % 2:
                return
            pair = slice(hd * LANES, (hd + 2) * LANES)
            q2 = src_ref[c * CHUNK:(c + 1) * CHUNK, pair]
            k2 = src_ref[c * CHUNK:(c + 1) * CHUNK, RET_WIDTH + pair.start:RET_WIDTH + pair.stop]
            lane = lax.broadcasted_iota(jnp.int32, k2.shape, 1)
            zero = jnp.zeros_like(k2)
            kk = jnp.concatenate([jnp.where(lane < LANES, k2, zero),
                                  jnp.where(lane >= LANES, k2, zero)], axis=0)
            s2 = lax.dot_general(q2, kk, nt, preferred_element_type=F32)
            for j in range(2):
                s = s2[:, j * LANES:(j + 1) * LANES]
                scores[c, hd + j] = (s * dintra_ref[hd + j]).astype(BF16)

        def increment(c, hd):
            kd = (col(c, hd, 1).astype(F32) * kdec_ref[hd]).astype(BF16)
            incs[c, hd] = lax.dot_general(kd, col(c, hd, 2), tn, preferred_element_type=F32)

        def output(c, hd):
            if not cell:
                cell["state"] = state_in()
            state = cell["state"]
            if state is None:
                outs[c, hd] = jnp.dot(scores[c, hd], col(c, hd, 2), preferred_element_type=F32)
            else:
                qd = (col(c, hd, 0).astype(F32) * qdec_ref[hd]).astype(BF16)
                outs[c, hd] = jnp.dot(jnp.concatenate([scores[c, hd], qd], axis=1),
                                      jnp.concatenate([col(c, hd, 2), state[hd].astype(BF16)],
                                                      axis=0),
                                      preferred_element_type=F32)
            if hd == RET_HEADS - 1:
                if state is None:
                    cell["state"] = [incs[c, h] for h in range(RET_HEADS)]
                else:
                    cell["state"] = [state[h] * cd_ref[h] + incs[c, h] for h in range(RET_HEADS)]

        def store_state():
            for hd in range(RET_HEADS):
                for ref in state_out_refs:
                    ref[hd] = cell["state"][hd]

        def norm_gate(c, hd):
            o = outs[c, hd]
            mu = jnp.mean(o, axis=-1, keepdims=True)
            d = o - mu
            var = jnp.mean(d * d, axis=-1, keepdims=True)
            on = d * lax.rsqrt(var + EPS) * gnw_ref[layer:layer + 1, hd * LANES:(hd + 1) * LANES]
            y_ref[c * CHUNK:(c + 1) * CHUNK, hd * LANES:(hd + 1) * LANES] = (
                jax.nn.silu(col(c, hd, 3).astype(F32)) * on).astype(BF16)

        return ([functools.partial(stage, c, hd) for stage in (score, increment, output)
                 for c, hd in units] + [store_state]
                + [functools.partial(norm_gate, c, hd) for c, hd in units])

    def previous_tile_retention_steps():
        def state_in():
            first = (t + TILES_PER_BATCH - 1) % TILES_PER_BATCH == 0
            return [jnp.where(first, meta_state_ref[hd], state_ref[hd])
                    for hd in range(RET_HEADS)]

        return retention_steps(ring_ref.at[(t + 1) % 2], yf_ref, state_in, [state_ref])

    def emit(major, minor=()):
        per = -(-len(minor) // len(major))
        for k, step in enumerate(major):
            step()
            for extra in minor[k * per:(k + 1) * per]:
                extra()

    @pl.when(i < STAGE_STEPS)
    def _():
        _stage_weight(i, w_ref, w_s)

    @pl.when(i == STAGE_STEPS)
    def _():
        ring_ref[...] = jnp.zeros_like(ring_ref)
        state_ref[...] = jnp.zeros_like(state_ref)
        meta_ring = ring_ref.at[0, :CHUNK]
        emit(project_steps(hm_ref, tabm_ref, meta_ring, pm_ref))
        emit(retention_steps(meta_ring, ym_ref, lambda: None, [meta_state_ref]))

    @pl.when((t >= 0) & (t < N_PROJ_TILES))
    def _():
        steps = previous_tile_retention_steps()
        n_norm = (PROJ_TILE // CHUNK) * RET_HEADS
        emit(steps[:-n_norm])
        emit(project_steps(hf_ref, tabf_ref, ring_ref.at[t % 2], pf_ref), steps[-n_norm:])

    @pl.when(t == N_PROJ_TILES)
    def _():
        emit(previous_tile_retention_steps())


def _inproj_retention(layer, h, norm_g, w_in, tab_frames, tab_meta, cd, dintra, qdec, kdec, gn_w):
    steps = STAGE_STEPS + 1 + N_PROJ_TILES + 1
    tile = lambda i: jnp.clip(i - (STAGE_STEPS + 1), 0, N_PROJ_TILES - 1)
    lag = lambda i: jnp.clip(i - (STAGE_STEPS + 2), 0, N_PROJ_TILES - 1)
    const3 = _whole((RET_HEADS, CHUNK, LANES))
    outs = pl.pallas_call(
        functools.partial(_inproj_retention_kernel, layer=layer),
        grid=(steps,),
        in_specs=[pl.BlockSpec(memory_space=pltpu.SMEM),
                  pl.BlockSpec((PROJ_TILE, D_MODEL), lambda i: (tile(i), 0)),
                  _whole((CHUNK, D_MODEL)), _whole((DEPTH, D_MODEL)),
                  pl.BlockSpec((None, D_MODEL // STAGE_STEPS, D_IN),
                               lambda i: (layer, jnp.minimum(i, STAGE_STEPS - 1), 0)),
                  pl.BlockSpec((PROJ_TILE, N_TAB * LANES),
                               lambda i: (tile(i) % TILES_PER_BATCH, 0)),
                  _whole((CHUNK, N_TAB * LANES)),
                  const3, const3, const3, _whole((DEPTH, RET_WIDTH))],
        out_specs=[pl.BlockSpec((PROJ_TILE, DIFF_COLS), lambda i: (tile(i), 0)),
                   _whole((CHUNK, DIFF_COLS)),
                   pl.BlockSpec((PROJ_TILE, RET_WIDTH), lambda i: (lag(i), 0)),
                   _whole((CHUNK, RET_WIDTH))],
        out_shape=[jax.ShapeDtypeStruct((FRAME_ROWS, DIFF_COLS), BF16),
                   jax.ShapeDtypeStruct((CHUNK, DIFF_COLS), BF16),
                   jax.ShapeDtypeStruct((FRAME_ROWS, RET_WIDTH), BF16),
                   jax.ShapeDtypeStruct((CHUNK, RET_WIDTH), BF16)],
        scratch_shapes=[pltpu.VMEM((D_MODEL, D_IN), BF16),
                        pltpu.VMEM((2, PROJ_TILE, RET_COLS), BF16),
                        pltpu.VMEM((RET_HEADS, RET_DK, LANES), F32),
                        pltpu.VMEM((RET_HEADS, RET_DK, LANES), F32)],
        compiler_params=pltpu.CompilerParams(
            dimension_semantics=("arbitrary",), vmem_limit_bytes=VMEM_LIMIT),
        name="inproj_retention",
    )(cd, h[0], h[1], norm_g, w_in, tab_frames, tab_meta, dintra, qdec, kdec, gn_w)
    return (outs[0], outs[1]), (outs[2], outs[3])


_QUERY_BLOCKS = ((0, CHUNK),) + tuple((r, 2 * CHUNK) for r in range(CHUNK, T_PAD, 2 * CHUNK))


def _key_tiles(nk):
    full = tuple((r, 2 * CHUNK) for r in range(CHUNK, nk - 2 * CHUNK, 2 * CHUNK))
    diagonal = ((nk - 2 * CHUNK, CHUNK), (nk - CHUNK, CHUNK)) if nk > CHUNK else ()
    return ((PAD, N_META),) + full + diagonal


def _diffattn_kernel(qf_ref, kf_ref, vf_ref, qm_ref, km_ref, vm_ref, w_ref, lq1_ref, lk1_ref,
                     lq2_ref, lk2_ref, of_ref, om_ref, s_ref, p_ref, vt_ref, *, layer, lambda_init):
    row = slice(layer, layer + 1)
    lam = (jnp.exp(jnp.sum(lq1_ref[row, :] * lk1_ref[row, :], axis=-1, keepdims=True))
           - jnp.exp(jnp.sum(lq2_ref[row, :] * lk2_ref[row, :], axis=-1, keepdims=True))
           + lambda_init)
    w_out = w_ref[row, :] * (1.0 - lambda_init)
    tn = (((0,), (0,)), ((), ()))

    def rows(meta_ref, frames_ref, r0, n):
        if r0 < CHUNK:
            return meta_ref[r0:r0 + n, :]
        return frames_ref[0, r0 - CHUNK:r0 - CHUNK + n, :]

    def fold(state, key, value, op):
        state[key] = value if key not in state else op(state[key], value)

    def score_tiles(blk, slot, state):
        r0, nq = _QUERY_BLOCKS[blk]
        nk = r0 + nq
        qt = rows(qm_ref, qf_ref, r0, nq).astype(F32).T
        feat = lax.broadcasted_iota(jnp.int32, (LANES, nq), 0)
        qq = jnp.concatenate([jnp.where(feat < DIFF_D, qt, 0.0),
                              jnp.where(feat >= DIFF_D, qt, 0.0)], axis=1).astype(BF16)

        def tile(t0, tk):
            keys = rows(km_ref, kf_ref, t0, tk)
            if r0 >= CHUNK and t0 > r0:
                late = jnp.concatenate([qq[:, nq // 2:nq], qq[:, nq + nq // 2:]], axis=1)
                s = jnp.dot(keys, late, preferred_element_type=F32)
                krow = lax.broadcasted_iota(jnp.int32, (tk, nq), 0)
                qcol = lax.broadcasted_iota(jnp.int32, (tk, nq), 1) % (nq // 2)
                s = jnp.where(krow <= qcol, s, NEG)
                hidden = jnp.full((tk, nq // 2), NEG, F32)
                s = jnp.concatenate([hidden, s[:, :nq // 2], hidden, s[:, nq // 2:]], axis=1)
            else:
                s = jnp.dot(keys, qq, preferred_element_type=F32)
                if t0 + tk > r0:
                    krow = lax.broadcasted_iota(jnp.int32, (tk, 2 * nq), 0)
                    qcol = lax.broadcasted_iota(jnp.int32, (tk, 2 * nq), 1) % nq
                    s = jnp.where(krow + (t0 - r0) <= qcol, s, NEG)
            s_ref[slot, t0:t0 + tk, :2 * nq] = s
            fold(state, "max", jnp.max(s, axis=0, keepdims=True), jnp.maximum)

        return [functools.partial(tile, t0, tk) for t0, tk in _key_tiles(nk)]

    def prob_tiles(blk, slot, smax, state):
        r0, nq = _QUERY_BLOCKS[blk]
        nk = r0 + nq

        def tile(t0, tk):
            p = jnp.exp2(s_ref[slot, t0:t0 + tk, :2 * nq] - smax)
            fold(state, "pos", jnp.sum(p[:, :nq], axis=0, keepdims=True), jnp.add)
            fold(state, "neg", jnp.sum(p[:, nq:], axis=0, keepdims=True), jnp.add)
            p_ref[slot, t0:t0 + tk, :2 * nq] = p.astype(BF16)

        return [functools.partial(tile, t0, tk) for t0, tk in _key_tiles(nk)]

    def finish(blk, slot, sums):
        r0, nq = _QUERY_BLOCKS[blk]
        nk = r0 + nq
        ot = lax.dot_general(vm_ref[PAD:, :], p_ref[slot, PAD:CHUNK, :2 * nq], tn,
                             preferred_element_type=F32)
        if nk > CHUNK:
            ot = ot + jnp.dot(vt_ref[:, :nk - CHUNK], p_ref[slot, CHUNK:nk, :2 * nq],
                              preferred_element_type=F32)
        o = (ot[:, :nq] * (1.0 / sums["pos"]) - ot[:, nq:] * (lam / sums["neg"])).T
        o = o * lax.rsqrt(jnp.mean(o * o, axis=-1, keepdims=True) + EPS) * w_out
        if r0 == 0:
            orow = lax.broadcasted_iota(jnp.int32, (nq, LANES), 0)
            om_ref[...] = jnp.where(orow >= PAD, o, 0.0).astype(BF16)
        else:
            of_ref[0, r0 - CHUNK:nk - CHUNK, :] = o.astype(BF16)

    n_blocks = len(_QUERY_BLOCKS)
    order = list(range(0, n_blocks, 2)) + list(range(n_blocks - 1 - n_blocks % 2, 0, -2))
    vt_ref[...] = vf_ref[0].astype(F32).T.astype(BF16)
    score_state, pending = {}, None
    for run in score_tiles(order[0], 0, score_state):
        run()
    for r, blk in enumerate(order):
        smax, sums, score_state = score_state["max"], {}, {}
        ahead = score_tiles(order[r + 1], (r + 1) % 2, score_state) if r + 1 < n_blocks else []
        current = prob_tiles(blk, r % 2, smax, sums)
        for n in range(max(len(ahead), len(current))):
            for runs in (current, ahead):
                if n < len(runs):
                    runs[n]()
            if n == 0 and pending is not None:
                finish(*pending)
        pending = (blk, r % 2, sums)
    finish(*pending)


def _diffattn(layer, proj, subln_w, lq1, lk1, lq2, lk2):
    proj_frames, proj_meta = proj
    frames3 = proj_frames.reshape(BATCH, SEQ, DIFF_COLS)
    lambda_init = 0.8 - 0.6 * math.exp(-0.3 * layer)
    fblk = lambda c: pl.BlockSpec((1, SEQ, LANES), lambda hd, b, c=c: (b, 0, c + hd))
    mblk = lambda c: pl.BlockSpec((CHUNK, LANES), lambda hd, b, c=c: (0, c + hd))
    small = lambda n: _whole((DEPTH, n))
    y_frames, y_meta = pl.pallas_call(
        functools.partial(_diffattn_kernel, layer=layer, lambda_init=lambda_init),
        grid=(DIFF_HEADS, BATCH),
        in_specs=[fblk(0), fblk(DIFF_HEADS), fblk(2 * DIFF_HEADS),
                  mblk(0), mblk(DIFF_HEADS), mblk(2 * DIFF_HEADS),
                  small(DIFF_DV), small(DIFF_D), small(DIFF_D), small(DIFF_D), small(DIFF_D)],
        out_specs=[pl.BlockSpec((1, SEQ, LANES), lambda hd, b: (b, 0, hd)),
                   pl.BlockSpec((CHUNK, LANES), lambda hd, b: (0, hd))],
        out_shape=[jax.ShapeDtypeStruct((BATCH, SEQ, DIFF_WIDTH), BF16),
                   jax.ShapeDtypeStruct((CHUNK, DIFF_WIDTH), BF16)],
        scratch_shapes=[pltpu.VMEM((2, T_PAD, 4 * CHUNK), F32),
                        pltpu.VMEM((2, T_PAD, 4 * CHUNK), BF16),
                        pltpu.VMEM((DIFF_DV, SEQ), BF16)],
        compiler_params=pltpu.CompilerParams(
            dimension_semantics=("arbitrary", "arbitrary"), vmem_limit_bytes=VMEM_LIMIT),
        name="diffattn",
    )(frames3, frames3, frames3, proj_meta, proj_meta, proj_meta, subln_w, lq1, lk1, lq2, lk2)
    return y_frames.reshape(FRAME_ROWS, DIFF_WIDTH), y_meta


def _rotary_tables():
    f32 = np.float32
    pos = np.arange(T_PAD, dtype=f32) - f32(PAD)
    angle = (f32(RET_THETA) ** (-np.linspace(0.0, 1.0, RET_DK // 2, dtype=f32))).astype(f32)
    fr = pos[:, None] * angle[None, :]
    c, s = np.cos(fr), np.sin(fr)
    zero = np.zeros_like(s)
    cos_i = np.repeat(c, 2, axis=-1)
    sin_even = np.stack([-s, zero], axis=-1).reshape(T_PAD, RET_DK)
    sin_odd = np.stack([zero, s], axis=-1).reshape(T_PAD, RET_DK)
    ks = f32(RET_DK ** -0.5)
    inv = (f32(ROPE_THETA) ** (-np.arange(0, ROPE_DIMS, 2, dtype=f32) / f32(ROPE_DIMS))).astype(f32)
    fq = pos[:, None] * inv[None, :]
    emb = np.concatenate([fq, fq], axis=-1)
    ce, se = np.cos(emb), np.sin(emb)
    half, rest = ROPE_DIMS // 2, DIFF_D - ROPE_DIMS
    zeros = lambda n: np.zeros((T_PAD, n), f32)
    two = lambda a: np.concatenate([a, a], axis=-1)
    c_d = two(np.concatenate([ce, np.ones((T_PAD, rest), f32)], axis=-1))
    s_lo = two(np.concatenate([zeros(half), se[:, half:], zeros(rest)], axis=-1))
    s_hi = two(np.concatenate([-se[:, :half], zeros(half + rest)], axis=-1))
    qs = f32((DIFF_D ** -0.5) * math.log2(math.e))
    tabs = [cos_i, sin_even, sin_odd, cos_i * ks, sin_even * ks, sin_odd * ks,
            c_d * qs, s_lo * qs, s_hi * qs, c_d, s_lo, s_hi]
    return np.concatenate(tabs, axis=-1).astype(f32)


def _retention_consts():
    f32 = np.float32
    log_gamma = np.log(f32(1.0) - f32(2.0) ** (f32(-5.0) - np.arange(RET_HEADS, dtype=f32))).astype(f32)
    idx = np.arange(CHUNK, dtype=f32)
    rel = idx[:, None] - idx[None, :]
    dintra = np.where(rel >= 0, np.exp(log_gamma[:, None, None] * np.maximum(rel, f32(0.0))), f32(0.0))
    k_decay = np.exp(log_gamma[:, None] * (CHUNK - 1 - idx)[None, :])
    q_decay = np.exp(log_gamma[:, None] * (idx + f32(1.0))[None, :])
    cd = np.exp(log_gamma * f32(CHUNK))
    bc = lambda a: np.ascontiguousarray(np.broadcast_to(a[:, :, None], (RET_HEADS, CHUNK, LANES)))
    return cd.astype(f32), dintra.astype(f32), bc(q_decay).astype(f32), bc(k_decay).astype(f32)


def kernel(x, meta_tokens, ffn1_norm, ffn1_w_gate, ffn1_w_up, ffn1_w_down, mix_norm, w_in, ret_gn_w, diff_subln_w, diff_lambda_q1, diff_lambda_k1, diff_lambda_q2, diff_lambda_k2, w_out, ffn2_norm, ffn2_w_gate, ffn2_w_up, ffn2_w_down, final_norm):
    meta_chunk = jnp.concatenate([jnp.zeros((PAD, D_MODEL), F32), meta_tokens.astype(F32)], axis=0)
    h = (x.astype(F32).reshape(FRAME_ROWS, D_MODEL), meta_chunk)

    tab = _rotary_tables()
    tab_meta, tab_frames = jnp.asarray(tab[:CHUNK]), jnp.asarray(tab[CHUNK:])
    cd, dintra, qdec, kdec = (jnp.asarray(a) for a in _retention_consts())

    for l in range(DEPTH):
        last = l == DEPTH - 1
        h = _ffn(l, h, ffn1_norm, ffn1_w_gate, ffn1_w_up, ffn1_w_down)
        proj_diff, y_ret = _inproj_retention(l, h, mix_norm, w_in, tab_frames, tab_meta,
                                             cd, dintra, qdec, kdec, ret_gn_w)
        y_diff = _diffattn(l, proj_diff, diff_subln_w, diff_lambda_q1, diff_lambda_k1,
                           diff_lambda_q2, diff_lambda_k2)
        h = _ffn(l, h, ffn2_norm, ffn2_w_gate, ffn2_w_up, ffn2_w_down,
                 mix=(y_ret, y_diff, w_out),
                 final_g=final_norm.reshape(1, D_MODEL) if last else None)

    return h[0].reshape(BATCH, SEQ, D_MODEL).astype(x.dtype)
```

```python
import functools
import math

import numpy as np
import jax
import jax.numpy as jnp
from jax import lax
from jax.experimental import pallas as pl
from jax.experimental.pallas import tpu as pltpu

D_MODEL = 1024
BATCH = 8
SEQ = 2048
DEPTH = 2
N_META = 16
CHUNK = 128
RET_HEADS = 4
RET_DK = 128
RET_WIDTH = 512
RET_THETA = 10000.0
DIFF_HEADS = 4
DIFF_D = 64
DIFF_DV = 128
DIFF_WIDTH = 512
ROPE_THETA = 500000.0
ROPE_DIMS = 16
D_FF = 2816
EPS = 1e-6
D_IN = 3584

PAD = CHUNK - N_META
T_PAD = CHUNK + SEQ
FRAME_ROWS = BATCH * SEQ
N_TAB = 12
LANES = 128
VMEM_LIMIT = 56 * 1024 * 1024
NEG = -1e30

F32 = jnp.float32
BF16 = jnp.bfloat16


def _rms(x, g):
    return x * lax.rsqrt(jnp.mean(x * x, axis=-1, keepdims=True) + EPS) * g


def _whole(shape):
    return pl.BlockSpec(shape, lambda *_: (0,) * len(shape))


class _DenseGrid:
    def __init__(self, tile, stage, meta):
        self.tile, self.stage, self.meta = tile, stage, meta
        self.n_tiles = FRAME_ROWS // tile
        self.steps = stage + self.n_tiles + int(meta)

    def tile_index(self, i):
        return jnp.clip(i - self.stage, 0, self.n_tiles - 1)

    def row_tile(self, width):
        return pl.BlockSpec((self.tile, width), lambda i: (self.tile_index(i), 0))

    def weight_chunk(self, layer, rows, cols):
        return pl.BlockSpec((None, rows // self.stage, cols),
                            lambda i: (layer, jnp.minimum(i, self.stage - 1), 0))

    def run(self, i, stage, frames, meta):
        pl.when(i < self.stage)(stage)
        pl.when((i >= self.stage) & (i < self.stage + self.n_tiles))(frames)
        if self.meta:
            pl.when(i == self.stage + self.n_tiles)(meta)


def _stage_weight(i, src_ref, dst_ref):
    rows = src_ref.shape[0]
    dst_ref[pl.ds(pl.multiple_of(i * rows, 16), rows), :] = src_ref[...].astype(BF16)


FFN_TILE = 512
FFN_SUB_TILE = 256
STAGE_STEPS = 8
PROJ_TILE = 512


def _ffn_kernel(*refs, grid, layer, mix, final):
    refs = list(refs)
    take = lambda n: [refs.pop(0) for _ in range(n)]
    with_meta = not final
    h_refs = take(2 if with_meta else 1)
    if mix:
        yr_refs, yd_refs = take(len(h_refs)), take(len(h_refs))
        (wo_ref,) = take(1)
    g_ref, wg_ref, wu_ref, wd_ref = take(4)
    if final:
        (fn_ref,) = take(1)
    o_refs = take(len(h_refs))
    wg_s, wu_s, wd_s = take(3)
    if mix:
        (wo_s,) = take(1)
    i = pl.program_id(0)

    def stage():
        _stage_weight(i, wg_ref, wg_s)
        _stage_weight(i, wu_ref, wu_s)
        _stage_weight(i, wd_ref, wd_s)
        if mix:
            _stage_weight(i, wo_ref, wo_s)

    def rows(which):
        n = h_refs[which].shape[0]
        parts = [slice(r, min(r + FFN_SUB_TILE, n)) for r in range(0, n, FFN_SUB_TILE)]
        xs, xns, acts = [], [], []
        for sl in parts:
            x = h_refs[which][sl, :]
            if mix:
                x = (x + jnp.dot(yr_refs[which][sl, :], wo_s[:RET_WIDTH, :],
                                 preferred_element_type=F32)
                     + jnp.dot(yd_refs[which][sl, :], wo_s[RET_WIDTH:, :],
                               preferred_element_type=F32))
            xs.append(x)
            xns.append(_rms(x, g_ref[layer:layer + 1, :]).astype(BF16))
        for xn in xns:
            gate = jnp.dot(xn, wg_s[...], preferred_element_type=F32)
            up = jnp.dot(xn, wu_s[...], preferred_element_type=F32)
            acts.append((jax.nn.silu(gate) * up).astype(BF16))
        for sl, x, act in zip(parts, xs, acts):
            y = x + 0.5 * jnp.dot(act, wd_s[...], preferred_element_type=F32)
            if final:
                y = _rms(y, fn_ref[...])
            o_refs[which][sl, :] = y

    grid.run(i, stage, functools.partial(rows, 0), functools.partial(rows, 1))


def _ffn(layer, h, norm_g, wg, wu, wd, mix=None, final_g=None):
    with_meta = final_g is None
    grid = _DenseGrid(tile=FFN_TILE, stage=STAGE_STEPS, meta=with_meta)
    _row_tile, _weight_chunk = grid.row_tile, grid.weight_chunk
    pair = lambda width: [_row_tile(width)] + ([_whole((CHUNK, width))] if with_meta else [])
    keep = lambda arrays: list(arrays) if with_meta else [arrays[0]]
    args, specs = keep(h), pair(D_MODEL)
    scratch = [pltpu.VMEM((D_MODEL, D_FF), BF16), pltpu.VMEM((D_MODEL, D_FF), BF16),
               pltpu.VMEM((D_FF, D_MODEL), BF16)]
    if mix is not None:
        y_ret, y_diff, w_out = mix
        args += keep(y_ret) + keep(y_diff) + [w_out]
        specs += pair(RET_WIDTH) + pair(DIFF_WIDTH) + [_weight_chunk(layer, D_MODEL, D_MODEL)]
        scratch.append(pltpu.VMEM((D_MODEL, D_MODEL), BF16))
    args += [norm_g, wg, wu, wd]
    specs += [_whole((DEPTH, D_MODEL)), _weight_chunk(layer, D_MODEL, D_FF),
              _weight_chunk(layer, D_MODEL, D_FF), _weight_chunk(layer, D_FF, D_MODEL)]
    if final_g is not None:
        args.append(final_g)
        specs.append(_whole((1, D_MODEL)))
    out_shape = [jax.ShapeDtypeStruct((FRAME_ROWS, D_MODEL), F32)]
    if with_meta:
        out_shape.append(jax.ShapeDtypeStruct((CHUNK, D_MODEL), F32))
    return pl.pallas_call(
        functools.partial(_ffn_kernel, grid=grid, layer=layer, mix=mix is not None,
                          final=final_g is not None),
        grid=(grid.steps,),
        in_specs=specs,
        out_specs=pair(D_MODEL),
        out_shape=out_shape,
        scratch_shapes=scratch,
        compiler_params=pltpu.CompilerParams(
            dimension_semantics=("arbitrary",), vmem_limit_bytes=VMEM_LIMIT),
        name="ffn",
    )(*args)


_ROTATIONS = (
    (0, 0, LANES - 1, 1),
    (RET_WIDTH, 3, LANES - 1, 1),
    (4 * RET_WIDTH, 6, 8, LANES - 8),
    (4 * RET_WIDTH + DIFF_WIDTH, 9, 8, LANES - 8),
)


RET_COLS = 4 * RET_WIDTH
DIFF_COLS = D_IN - RET_COLS
N_PROJ_TILES = FRAME_ROWS // PROJ_TILE
TILES_PER_BATCH = SEQ // PROJ_TILE


def _inproj_retention_kernel(cd_ref, hf_ref, hm_ref, g_ref, w_ref, tabf_ref, tabm_ref,
                             dintra_ref, qdec_ref, kdec_ref, gnw_ref,
                             pf_ref, pm_ref, yf_ref, ym_ref,
                             w_s, ring_ref, state_ref, meta_state_ref, *, layer):
    nt = (((1,), (1,)), ((), ()))
    tn = (((0,), (0,)), ((), ()))
    i = pl.program_id(0)
    t = i - (STAGE_STEPS + 1)

    def project_steps(h_ref, tab_ref, ret_ref, diff_ref):
        rotation = {base: rest for base, *rest in _ROTATIONS}
        cell = {}

        def tab(k):
            return tab_ref[:, k * LANES:(k + 1) * LANES]

        def group(base):
            if not cell:
                cell["xn"] = _rms(h_ref[...], g_ref[layer:layer + 1, :]).astype(BF16)
            p = jnp.dot(cell["xn"], w_s[:, base:base + RET_WIDTH], preferred_element_type=F32)
            dst, lo = (ret_ref, base) if base < RET_COLS else (diff_ref, base - RET_COLS)
            if base in rotation:
                t0, roll_a, roll_b = rotation[base]
                for hd in range(RET_HEADS):
                    xs = p[:, hd * LANES:(hd + 1) * LANES]
                    rot = (xs * tab(t0) + pltpu.roll(xs, roll_a, 1) * tab(t0 + 1)
                           + pltpu.roll(xs, roll_b, 1) * tab(t0 + 2))
                    dst[:, lo + hd * LANES:lo + (hd + 1) * LANES] = rot.astype(BF16)
            else:
                dst[:, lo:lo + RET_WIDTH] = p.astype(BF16)

        return [functools.partial(group, base) for base in range(0, D_IN, RET_WIDTH)]

    def retention_steps(src_ref, y_ref, state_in, state_out_refs):
        n_chunks = src_ref.shape[0] // CHUNK
        units = [(c, hd) for c in range(n_chunks) for hd in range(RET_HEADS)]
        scores, incs, outs, cell = {}, {}, {}, {}

        def col(c, hd, which):
            return src_ref[c * CHUNK:(c + 1) * CHUNK,
                           which * RET_WIDTH + hd * LANES:which * RET_WIDTH + (hd + 1) * LANES]

        def score(c, hd):
            if hd % 2:
                return
            pair = slice(hd * LANES, (hd + 2) * LANES)
            q2 = src_ref[c * CHUNK:(c + 1) * CHUNK, pair]
            k2 = src_ref[c * CHUNK:(c + 1) * CHUNK, RET_WIDTH + pair.start:RET_WIDTH + pair.stop]
            lane = lax.broadcasted_iota(jnp.int32, k2.shape, 1)
            zero = jnp.zeros_like(k2)
            kk = jnp.concatenate([jnp.where(lane < LANES, k2, zero),
                                  jnp.where(lane >= LANES, k2, zero)], axis=0)
            s2 = lax.dot_general(q2, kk, nt, preferred_element_type=F32)
            for j in range(2):
                s = s2[:, j * LANES:(j + 1) * LANES]
                scores[c, hd + j] = (s * dintra_ref[hd + j]).astype(BF16)

        def increment(c, hd):
            if hd % 2:
                return
            kd2 = jnp.concatenate([(col(c, hd + j, 1).astype(F32) * kdec_ref[hd + j]).astype(BF16)
                                   for j in range(2)], axis=0)
            v2 = src_ref[c * CHUNK:(c + 1) * CHUNK,
                         2 * RET_WIDTH + hd * LANES:2 * RET_WIDTH + (hd + 2) * LANES]
            lane = lax.broadcasted_iota(jnp.int32, v2.shape, 1)
            zero = jnp.zeros_like(v2)
            vv = jnp.concatenate([jnp.where(lane < LANES, v2, zero),
                                  jnp.where(lane >= LANES, v2, zero)], axis=0)
            inc2 = lax.dot_general(kd2, vv, tn, preferred_element_type=F32)
            for j in range(2):
                incs[c, hd + j] = inc2[:, j * LANES:(j + 1) * LANES]

        def output(c, hd):
            if not cell:
                cell["state"] = state_in()
            state = cell["state"]
            if state is None:
                outs[c, hd] = jnp.dot(scores[c, hd], col(c, hd, 2), preferred_element_type=F32)
            else:
                qd = (col(c, hd, 0).astype(F32) * qdec_ref[hd]).astype(BF16)
                outs[c, hd] = jnp.dot(jnp.concatenate([scores[c, hd], qd], axis=1),
                                      jnp.concatenate([col(c, hd, 2), state[hd].astype(BF16)],
                                                      axis=0),
                                      preferred_element_type=F32)
            if hd == RET_HEADS - 1:
                if state is None:
                    cell["state"] = [incs[c, h] for h in range(RET_HEADS)]
                else:
                    cell["state"] = [state[h] * cd_ref[h] + incs[c, h] for h in range(RET_HEADS)]

        def store_state():
            for hd in range(RET_HEADS):
                for ref in state_out_refs:
                    ref[hd] = cell["state"][hd]

        def norm_gate(c, hd):
            o = outs[c, hd]
            mu = jnp.mean(o, axis=-1, keepdims=True)
            d = o - mu
            var = jnp.mean(d * d, axis=-1, keepdims=True)
            on = d * lax.rsqrt(var + EPS) * gnw_ref[layer:layer + 1, hd * LANES:(hd + 1) * LANES]
            y_ref[c * CHUNK:(c + 1) * CHUNK, hd * LANES:(hd + 1) * LANES] = (
                jax.nn.silu(col(c, hd, 3).astype(F32)) * on).astype(BF16)

        return ([functools.partial(stage, c, hd) for stage in (score, increment, output)
                 for c, hd in units] + [store_state]
                + [functools.partial(norm_gate, c, hd) for c, hd in units])

    def previous_tile_retention_steps():
        def state_in():
            first = (t + TILES_PER_BATCH - 1) % TILES_PER_BATCH == 0
            return [jnp.where(first, meta_state_ref[hd], state_ref[hd])
                    for hd in range(RET_HEADS)]

        return retention_steps(ring_ref.at[(t + 1) % 2], yf_ref, state_in, [state_ref])

    def emit(major, minor=()):
        per = -(-len(minor) // len(major))
        for k, step in enumerate(major):
            step()
            for extra in minor[k * per:(k + 1) * per]:
                extra()

    @pl.when(i < STAGE_STEPS)
    def _():
        _stage_weight(i, w_ref, w_s)

    @pl.when(i == STAGE_STEPS)
    def _():
        ring_ref[...] = jnp.zeros_like(ring_ref)
        state_ref[...] = jnp.zeros_like(state_ref)
        meta_ring = ring_ref.at[0, :CHUNK]
        emit(project_steps(hm_ref, tabm_ref, meta_ring, pm_ref))
        emit(retention_steps(meta_ring, ym_ref, lambda: None, [meta_state_ref]))

    @pl.when((t >= 0) & (t < N_PROJ_TILES))
    def _():
        steps = previous_tile_retention_steps()
        n_norm = (PROJ_TILE // CHUNK) * RET_HEADS
        emit(steps[:-n_norm])
        emit(project_steps(hf_ref, tabf_ref, ring_ref.at[t % 2], pf_ref), steps[-n_norm:])

    @pl.when(t == N_PROJ_TILES)
    def _():
        emit(previous_tile_retention_steps())


def _inproj_retention(layer, h, norm_g, w_in, tab_frames, tab_meta, cd, dintra, qdec, kdec, gn_w):
    steps = STAGE_STEPS + 1 + N_PROJ_TILES + 1
    tile = lambda i: jnp.clip(i - (STAGE_STEPS + 1), 0, N_PROJ_TILES - 1)
    lag = lambda i: jnp.clip(i - (STAGE_STEPS + 2), 0, N_PROJ_TILES - 1)
    const3 = _whole((RET_HEADS, CHUNK, LANES))
    outs = pl.pallas_call(
        functools.partial(_inproj_retention_kernel, layer=layer),
        grid=(steps,),
        in_specs=[pl.BlockSpec(memory_space=pltpu.SMEM),
                  pl.BlockSpec((PROJ_TILE, D_MODEL), lambda i: (tile(i), 0)),
                  _whole((CHUNK, D_MODEL)), _whole((DEPTH, D_MODEL)),
                  pl.BlockSpec((None, D_MODEL // STAGE_STEPS, D_IN),
                               lambda i: (layer, jnp.minimum(i, STAGE_STEPS - 1), 0)),
                  pl.BlockSpec((PROJ_TILE, N_TAB * LANES),
                               lambda i: (tile(i) % TILES_PER_BATCH, 0)),
                  _whole((CHUNK, N_TAB * LANES)),
                  const3, const3, const3, _whole((DEPTH, RET_WIDTH))],
        out_specs=[pl.BlockSpec((PROJ_TILE, DIFF_COLS), lambda i: (tile(i), 0)),
                   _whole((CHUNK, DIFF_COLS)),
                   pl.BlockSpec((PROJ_TILE, RET_WIDTH), lambda i: (lag(i), 0)),
                   _whole((CHUNK, RET_WIDTH))],
        out_shape=[jax.ShapeDtypeStruct((FRAME_ROWS, DIFF_COLS), BF16),
                   jax.ShapeDtypeStruct((CHUNK, DIFF_COLS), BF16),
                   jax.ShapeDtypeStruct((FRAME_ROWS, RET_WIDTH), BF16),
                   jax.ShapeDtypeStruct((CHUNK, RET_WIDTH), BF16)],
        scratch_shapes=[pltpu.VMEM((D_MODEL, D_IN), BF16),
                        pltpu.VMEM((2, PROJ_TILE, RET_COLS), BF16),
                        pltpu.VMEM((RET_HEADS, RET_DK, LANES), F32),
                        pltpu.VMEM((RET_HEADS, RET_DK, LANES), F32)],
        compiler_params=pltpu.CompilerParams(
            dimension_semantics=("arbitrary",), vmem_limit_bytes=VMEM_LIMIT),
        name="inproj_retention",
    )(cd, h[0], h[1], norm_g, w_in, tab_frames, tab_meta, dintra, qdec, kdec, gn_w)
    return (outs[0], outs[1]), (outs[2], outs[3])


_QUERY_BLOCKS = ((0, CHUNK),) + tuple((r, 2 * CHUNK) for r in range(CHUNK, T_PAD, 2 * CHUNK))


def _key_tiles(nk):
    full = tuple((r, 2 * CHUNK) for r in range(CHUNK, nk - 2 * CHUNK, 2 * CHUNK))
    diagonal = ((nk - 2 * CHUNK, CHUNK), (nk - CHUNK, CHUNK)) if nk > CHUNK else ()
    return ((PAD, N_META),) + full + diagonal


def _diffattn_kernel(qf_ref, kf_ref, vf_ref, qm_ref, km_ref, vm_ref, w_ref, lq1_ref, lk1_ref,
                     lq2_ref, lk2_ref, of_ref, om_ref, s_ref, p_ref, vt_ref, *, layer, lambda_init):
    row = slice(layer, layer + 1)
    lam = (jnp.exp(jnp.sum(lq1_ref[row, :] * lk1_ref[row, :], axis=-1, keepdims=True))
           - jnp.exp(jnp.sum(lq2_ref[row, :] * lk2_ref[row, :], axis=-1, keepdims=True))
           + lambda_init)
    w_out = w_ref[row, :] * (1.0 - lambda_init)
    tn = (((0,), (0,)), ((), ()))

    def rows(meta_ref, frames_ref, r0, n):
        if r0 < CHUNK:
            return meta_ref[r0:r0 + n, :]
        return frames_ref[0, r0 - CHUNK:r0 - CHUNK + n, :]

    def fold(state, key, value, op):
        state[key] = value if key not in state else op(state[key], value)

    def score_tiles(blk, slot, state):
        r0, nq = _QUERY_BLOCKS[blk]
        nk = r0 + nq
        qt = rows(qm_ref, qf_ref, r0, nq).astype(F32).T
        feat = lax.broadcasted_iota(jnp.int32, (LANES, nq), 0)
        qq = jnp.concatenate([jnp.where(feat < DIFF_D, qt, 0.0),
                              jnp.where(feat >= DIFF_D, qt, 0.0)], axis=1).astype(BF16)

        def tile(t0, tk):
            keys = rows(km_ref, kf_ref, t0, tk)
            if r0 >= CHUNK and t0 > r0:
                late = jnp.concatenate([qq[:, nq // 2:nq], qq[:, nq + nq // 2:]], axis=1)
                s = jnp.dot(keys, late, preferred_element_type=F32)
                krow = lax.broadcasted_iota(jnp.int32, (tk, nq), 0)
                qcol = lax.broadcasted_iota(jnp.int32, (tk, nq), 1) % (nq // 2)
                s = jnp.where(krow <= qcol, s, NEG)
                hidden = jnp.full((tk, nq // 2), NEG, F32)
                s = jnp.concatenate([hidden, s[:, :nq // 2], hidden, s[:, nq // 2:]], axis=1)
            else:
                s = jnp.dot(keys, qq, preferred_element_type=F32)
                if t0 + tk > r0:
                    krow = lax.broadcasted_iota(jnp.int32, (tk, 2 * nq), 0)
                    qcol = lax.broadcasted_iota(jnp.int32, (tk, 2 * nq), 1) % nq
                    s = jnp.where(krow + (t0 - r0) <= qcol, s, NEG)
            s_ref[slot, t0:t0 + tk, :2 * nq] = s
            fold(state, "max", jnp.max(s, axis=0, keepdims=True), jnp.maximum)

        return [functools.partial(tile, t0, tk) for t0, tk in _key_tiles(nk)]

    def prob_tiles(blk, slot, smax, state):
        r0, nq = _QUERY_BLOCKS[blk]
        nk = r0 + nq

        def tile(t0, tk):
            p = jnp.exp2(s_ref[slot, t0:t0 + tk, :2 * nq] - smax)
            fold(state, "pos", jnp.sum(p[:, :nq], axis=0, keepdims=True), jnp.add)
            fold(state, "neg", jnp.sum(p[:, nq:], axis=0, keepdims=True), jnp.add)
            p_ref[slot, t0:t0 + tk, :2 * nq] = p.astype(BF16)

        return [functools.partial(tile, t0, tk) for t0, tk in _key_tiles(nk)]

    def finish(blk, slot, sums):
        r0, nq = _QUERY_BLOCKS[blk]
        nk = r0 + nq
        ot = lax.dot_general(vm_ref[PAD:, :], p_ref[slot, PAD:CHUNK, :2 * nq], tn,
                             preferred_element_type=F32)
        if nk > CHUNK:
            ot = ot + jnp.dot(vt_ref[:, :nk - CHUNK], p_ref[slot, CHUNK:nk, :2 * nq],
                              preferred_element_type=F32)
        o = (ot[:, :nq] * (1.0 / sums["pos"]) - ot[:, nq:] * (lam / sums["neg"])).T
        o = o * lax.rsqrt(jnp.mean(o * o, axis=-1, keepdims=True) + EPS) * w_out
        if r0 == 0:
            orow = lax.broadcasted_iota(jnp.int32, (nq, LANES), 0)
            om_ref[...] = jnp.where(orow >= PAD, o, 0.0).astype(BF16)
        else:
            of_ref[0, r0 - CHUNK:nk - CHUNK, :] = o.astype(BF16)

    n_blocks = len(_QUERY_BLOCKS)
    order = list(range(0, n_blocks, 2)) + list(range(n_blocks - 1 - n_blocks % 2, 0, -2))
    vt_ref[...] = vf_ref[0].astype(F32).T.astype(BF16)
    score_state, pending = {}, None
    for run in score_tiles(order[0], 0, score_state):
        run()
    for r, blk in enumerate(order):
        smax, sums, score_state = score_state["max"], {}, {}
        ahead = score_tiles(order[r + 1], (r + 1) % 2, score_state) if r + 1 < n_blocks else []
        current = prob_tiles(blk, r % 2, smax, sums)
        for n in range(max(len(ahead), len(current))):
            for runs in (current, ahead):
                if n < len(runs):
                    runs[n]()
            if n == 0 and pending is not None:
                finish(*pending)
        pending = (blk, r % 2, sums)
    finish(*pending)


def _diffattn(layer, proj, subln_w, lq1, lk1, lq2, lk2):
    proj_frames, proj_meta = proj
    frames3 = proj_frames.reshape(BATCH, SEQ, DIFF_COLS)
    lambda_init = 0.8 - 0.6 * math.exp(-0.3 * layer)
    fblk = lambda c: pl.BlockSpec((1, SEQ, LANES), lambda hd, b, c=c: (b, 0, c + hd))
    mblk = lambda c: pl.BlockSpec((CHUNK, LANES), lambda hd, b, c=c: (0, c + hd))
    small = lambda n: _whole((DEPTH, n))
    y_frames, y_meta = pl.pallas_call(
        functools.partial(_diffattn_kernel, layer=layer, lambda_init=lambda_init),
        grid=(DIFF_HEADS, BATCH),
        in_specs=[fblk(0), fblk(DIFF_HEADS), fblk(2 * DIFF_HEADS),
                  mblk(0), mblk(DIFF_HEADS), mblk(2 * DIFF_HEADS),
                  small(DIFF_DV), small(DIFF_D), small(DIFF_D), small(DIFF_D), small(DIFF_D)],
        out_specs=[pl.BlockSpec((1, SEQ, LANES), lambda hd, b: (b, 0, hd)),
                   pl.BlockSpec((CHUNK, LANES), lambda hd, b: (0, hd))],
        out_shape=[jax.ShapeDtypeStruct((BATCH, SEQ, DIFF_WIDTH), BF16),
                   jax.ShapeDtypeStruct((CHUNK, DIFF_WIDTH), BF16)],
        scratch_shapes=[pltpu.VMEM((2, T_PAD, 4 * CHUNK), F32),
                        pltpu.VMEM((2, T_PAD, 4 * CHUNK), BF16),
                        pltpu.VMEM((DIFF_DV, SEQ), BF16)],
        compiler_params=pltpu.CompilerParams(
            dimension_semantics=("arbitrary", "arbitrary"), vmem_limit_bytes=VMEM_LIMIT),
        name="diffattn",
    )(frames3, frames3, frames3, proj_meta, proj_meta, proj_meta, subln_w, lq1, lk1, lq2, lk2)
    return y_frames.reshape(FRAME_ROWS, DIFF_WIDTH), y_meta


def _rotary_tables():
    f32 = np.float32
    pos = np.arange(T_PAD, dtype=f32) - f32(PAD)
    angle = (f32(RET_THETA) ** (-np.linspace(0.0, 1.0, RET_DK // 2, dtype=f32))).astype(f32)
    fr = pos[:, None] * angle[None, :]
    c, s = np.cos(fr), np.sin(fr)
    zero = np.zeros_like(s)
    cos_i = np.repeat(c, 2, axis=-1)
    sin_even = np.stack([-s, zero], axis=-1).reshape(T_PAD, RET_DK)
    sin_odd = np.stack([zero, s], axis=-1).reshape(T_PAD, RET_DK)
    ks = f32(RET_DK ** -0.5)
    inv = (f32(ROPE_THETA) ** (-np.arange(0, ROPE_DIMS, 2, dtype=f32) / f32(ROPE_DIMS))).astype(f32)
    fq = pos[:, None] * inv[None, :]
    emb = np.concatenate([fq, fq], axis=-1)
    ce, se = np.cos(emb), np.sin(emb)
    half, rest = ROPE_DIMS // 2, DIFF_D - ROPE_DIMS
    zeros = lambda n: np.zeros((T_PAD, n), f32)
    two = lambda a: np.concatenate([a, a], axis=-1)
    c_d = two(np.concatenate([ce, np.ones((T_PAD, rest), f32)], axis=-1))
    s_lo = two(np.concatenate([zeros(half), se[:, half:], zeros(rest)], axis=-1))
    s_hi = two(np.concatenate([-se[:, :half], zeros(half + rest)], axis=-1))
    qs = f32((DIFF_D ** -0.5) * math.log2(math.e))
    tabs = [cos_i, sin_even, sin_odd, cos_i * ks, sin_even * ks, sin_odd * ks,
            c_d * qs, s_lo * qs, s_hi * qs, c_d, s_lo, s_hi]
    return np.concatenate(tabs, axis=-1).astype(f32)


def _retention_consts():
    f32 = np.float32
    log_gamma = np.log(f32(1.0) - f32(2.0) ** (f32(-5.0) - np.arange(RET_HEADS, dtype=f32))).astype(f32)
    idx = np.arange(CHUNK, dtype=f32)
    rel = idx[:, None] - idx[None, :]
    dintra = np.where(rel >= 0, np.exp(log_gamma[:, None, None] * np.maximum(rel, f32(0.0))), f32(0.0))
    k_decay = np.exp(log_gamma[:, None] * (CHUNK - 1 - idx)[None, :])
    q_decay = np.exp(log_gamma[:, None] * (idx + f32(1.0))[None, :])
    cd = np.exp(log_gamma * f32(CHUNK))
    bc = lambda a: np.ascontiguousarray(np.broadcast_to(a[:, :, None], (RET_HEADS, CHUNK, LANES)))
    return cd.astype(f32), dintra.astype(f32), bc(q_decay).astype(f32), bc(k_decay).astype(f32)


def kernel(x, meta_tokens, ffn1_norm, ffn1_w_gate, ffn1_w_up, ffn1_w_down, mix_norm, w_in, ret_gn_w, diff_subln_w, diff_lambda_q1, diff_lambda_k1, diff_lambda_q2, diff_lambda_k2, w_out, ffn2_norm, ffn2_w_gate, ffn2_w_up, ffn2_w_down, final_norm):
    meta_chunk = jnp.concatenate([jnp.zeros((PAD, D_MODEL), F32), meta_tokens.astype(F32)], axis=0)
    h = (x.astype(F32).reshape(FRAME_ROWS, D_MODEL), meta_chunk)

    tab = _rotary_tables()
    tab_meta, tab_frames = jnp.asarray(tab[:CHUNK]), jnp.asarray(tab[CHUNK:])
    cd, dintra, qdec, kdec = (jnp.asarray(a) for a in _retention_consts())

    for l in range(DEPTH):
        last = l == DEPTH - 1
        h = _ffn(l, h, ffn1_norm, ffn1_w_gate, ffn1_w_up, ffn1_w_down)
        proj_diff, y_ret = _inproj_retention(l, h, mix_norm, w_in, tab_frames, tab_meta,
                                             cd, dintra, qdec, kdec, ret_gn_w)
        y_diff = _diffattn(l, proj_diff, diff_subln_w, diff_lambda_q1, diff_lambda_k1,
                           diff_lambda_q2, diff_lambda_k2)
        h = _ffn(l, h, ffn2_norm, ffn2_w_gate, ffn2_w_up, ffn2_w_down,
                 mix=(y_ret, y_diff, w_out),
                 final_g=final_norm.reshape(1, D_MODEL) if last else None)

    return h[0].reshape(BATCH, SEQ, D_MODEL).astype(x.dtype)
```

```python
import functools
import math

import numpy as np
import jax
import jax.numpy as jnp
from jax import lax
from jax.experimental import pallas as pl
from jax.experimental.pallas import tpu as pltpu

D_MODEL = 1024
BATCH = 8
SEQ = 2048
DEPTH = 2
N_META = 16
CHUNK = 128
RET_HEADS = 4
RET_DK = 128
RET_WIDTH = 512
RET_THETA = 10000.0
DIFF_HEADS = 4
DIFF_D = 64
DIFF_DV = 128
DIFF_WIDTH = 512
ROPE_THETA = 500000.0
ROPE_DIMS = 16
D_FF = 2816
EPS = 1e-6
D_IN = 3584

PAD = CHUNK - N_META
T_PAD = CHUNK + SEQ
FRAME_ROWS = BATCH * SEQ
N_TAB = 12
LANES = 128
VMEM_LIMIT = 56 * 1024 * 1024
NEG = -1e30

F32 = jnp.float32
BF16 = jnp.bfloat16


def _rms(x, g):
    return x * lax.rsqrt(jnp.mean(x * x, axis=-1, keepdims=True) + EPS) * g


def _whole(shape):
    return pl.BlockSpec(shape, lambda *_: (0,) * len(shape))


class _DenseGrid:
    def __init__(self, tile, stage, meta):
        self.tile, self.stage, self.meta = tile, stage, meta
        self.n_tiles = FRAME_ROWS // tile
        self.steps = stage + self.n_tiles + int(meta)

    def tile_index(self, i):
        return jnp.clip(i - self.stage, 0, self.n_tiles - 1)

    def row_tile(self, width):
        return pl.BlockSpec((self.tile, width), lambda i: (self.tile_index(i), 0))

    def weight_chunk(self, layer, rows, cols):
        return pl.BlockSpec((None, rows // self.stage, cols),
                            lambda i: (layer, jnp.minimum(i, self.stage - 1), 0))

    def run(self, i, stage, frames, meta):
        pl.when(i < self.stage)(stage)
        pl.when((i >= self.stage) & (i < self.stage + self.n_tiles))(frames)
        if self.meta:
            pl.when(i == self.stage + self.n_tiles)(meta)


def _stage_weight(i, src_ref, dst_ref):
    rows = src_ref.shape[0]
    dst_ref[pl.ds(pl.multiple_of(i * rows, 16), rows), :] = src_ref[...].astype(BF16)


FFN_TILE = 512
FFN_SUB_TILE = 128
STAGE_STEPS = 8
PROJ_TILE = 512


def _ffn_kernel(*refs, grid, layer, mix, final):
    refs = list(refs)
    take = lambda n: [refs.pop(0) for _ in range(n)]
    with_meta = not final
    h_refs = take(2 if with_meta else 1)
    if mix:
        yr_refs, yd_refs = take(len(h_refs)), take(len(h_refs))
        (wo_ref,) = take(1)
    g_ref, wg_ref, wu_ref, wd_ref = take(4)
    if final:
        (fn_ref,) = take(1)
    o_refs = take(len(h_refs))
    wg_s, wu_s, wd_s = take(3)
    if mix:
        (wo_s,) = take(1)
    i = pl.program_id(0)

    def stage():
        _stage_weight(i, wg_ref, wg_s)
        _stage_weight(i, wu_ref, wu_s)
        _stage_weight(i, wd_ref, wd_s)
        if mix:
            _stage_weight(i, wo_ref, wo_s)

    def rows(which):
        n = h_refs[which].shape[0]
        parts = [slice(r, min(r + FFN_SUB_TILE, n)) for r in range(0, n, FFN_SUB_TILE)]
        xs, xns, acts = [], [], []
        for sl in parts:
            x = h_refs[which][sl, :]
            if mix:
                x = (x + jnp.dot(yr_refs[which][sl, :], wo_s[:RET_WIDTH, :],
                                 preferred_element_type=F32)
                     + jnp.dot(yd_refs[which][sl, :], wo_s[RET_WIDTH:, :],
                               preferred_element_type=F32))
            xs.append(x)
            xns.append(_rms(x, g_ref[layer:layer + 1, :]).astype(BF16))
        for xn in xns:
            gate = jnp.dot(xn, wg_s[...], preferred_element_type=F32)
            up = jnp.dot(xn, wu_s[...], preferred_element_type=F32)
            acts.append((jax.nn.silu(gate) * up).astype(BF16))
        for sl, x, act in zip(parts, xs, acts):
            y = x + 0.5 * jnp.dot(act, wd_s[...], preferred_element_type=F32)
            if final:
                y = _rms(y, fn_ref[...])
            o_refs[which][sl, :] = y

    grid.run(i, stage, functools.partial(rows, 0), functools.partial(rows, 1))


def _ffn(layer, h, norm_g, wg, wu, wd, mix=None, final_g=None):
    with_meta = final_g is None
    grid = _DenseGrid(tile=FFN_TILE, stage=STAGE_STEPS, meta=with_meta)
    _row_tile, _weight_chunk = grid.row_tile, grid.weight_chunk
    pair = lambda width: [_row_tile(width)] + ([_whole((CHUNK, width))] if with_meta else [])
    keep = lambda arrays: list(arrays) if with_meta else [arrays[0]]
    args, specs = keep(h), pair(D_MODEL)
    scratch = [pltpu.VMEM((D_MODEL, D_FF), BF16), pltpu.VMEM((D_MODEL, D_FF), BF16),
               pltpu.VMEM((D_FF, D_MODEL), BF16)]
    if mix is not None:
        y_ret, y_diff, w_out = mix
        args += keep(y_ret) + keep(y_diff) + [w_out]
        specs += pair(RET_WIDTH) + pair(DIFF_WIDTH) + [_weight_chunk(layer, D_MODEL, D_MODEL)]
        scratch.append(pltpu.VMEM((D_MODEL, D_MODEL), BF16))
    args += [norm_g, wg, wu, wd]
    specs += [_whole((DEPTH, D_MODEL)), _weight_chunk(layer, D_MODEL, D_FF),
              _weight_chunk(layer, D_MODEL, D_FF), _weight_chunk(layer, D_FF, D_MODEL)]
    if final_g is not None:
        args.append(final_g)
        specs.append(_whole((1, D_MODEL)))
    out_shape = [jax.ShapeDtypeStruct((FRAME_ROWS, D_MODEL), F32)]
    if with_meta:
        out_shape.append(jax.ShapeDtypeStruct((CHUNK, D_MODEL), F32))
    return pl.pallas_call(
        functools.partial(_ffn_kernel, grid=grid, layer=layer, mix=mix is not None,
                          final=final_g is not None),
        grid=(grid.steps,),
        in_specs=specs,
        out_specs=pair(D_MODEL),
        out_shape=out_shape,
        scratch_shapes=scratch,
        compiler_params=pltpu.CompilerParams(
            dimension_semantics=("arbitrary",), vmem_limit_bytes=VMEM_LIMIT),
        name="ffn",
    )(*args)


_ROTATIONS = (
    (0, 0, LANES - 1, 1),
    (RET_WIDTH, 3, LANES - 1, 1),
    (4 * RET_WIDTH, 6, 8, LANES - 8),
    (4 * RET_WIDTH + DIFF_WIDTH, 9, 8, LANES - 8),
)


RET_COLS = 4 * RET_WIDTH
DIFF_COLS = D_IN - RET_COLS
N_PROJ_TILES = FRAME_ROWS // PROJ_TILE
TILES_PER_BATCH = SEQ // PROJ_TILE


def _inproj_retention_kernel(cd_ref, hf_ref, hm_ref, g_ref, w_ref, tabf_ref, tabm_ref,
                             dintra_ref, qdec_ref, kdec_ref, gnw_ref,
                             pf_ref, pm_ref, yf_ref, ym_ref,
                             w_s, ring_ref, state_ref, meta_state_ref, *, layer):
    nt = (((1,), (1,)), ((), ()))
    tn = (((0,), (0,)), ((), ()))
    i = pl.program_id(0)
    t = i - (STAGE_STEPS + 1)

    def project_steps(h_ref, tab_ref, ret_ref, diff_ref):
        rotation = {base: rest for base, *rest in _ROTATIONS}
        cell = {}

        def tab(k):
            return tab_ref[:, k * LANES:(k + 1) * LANES]

        def group(base):
            if not cell:
                cell["xn"] = _rms(h_ref[...], g_ref[layer:layer + 1, :]).astype(BF16)
            p = jnp.dot(cell["xn"], w_s[:, base:base + RET_WIDTH], preferred_element_type=F32)
            dst, lo = (ret_ref, base) if base < RET_COLS else (diff_ref, base - RET_COLS)
            if base in rotation:
                t0, roll_a, roll_b = rotation[base]
                for hd in range(RET_HEADS):
                    xs = p[:, hd * LANES:(hd + 1) * LANES]
                    rot = (xs * tab(t0) + pltpu.roll(xs, roll_a, 1) * tab(t0 + 1)
                           + pltpu.roll(xs, roll_b, 1) * tab(t0 + 2))
                    dst[:, lo + hd * LANES:lo + (hd + 1) * LANES] = rot.astype(BF16)
            else:
                dst[:, lo:lo + RET_WIDTH] = p.astype(BF16)

        return [functools.partial(group, base) for base in range(0, D_IN, RET_WIDTH)]

    def retention_steps(src_ref, y_ref, state_in, state_out_refs):
        n_chunks = src_ref.shape[0] // CHUNK
        units = [(c, hd) for c in range(n_chunks) for hd in range(RET_HEADS)]
        scores, incs, outs, cell = {}, {}, {}, {}

        def col(c, hd, which):
            return src_ref[c * CHUNK:(c + 1) * CHUNK,
                           which * RET_WIDTH + hd * LANES:which * RET_WIDTH + (hd + 1) * LANES]

        def score(c, hd):
            if hd % 2:
                return
            pair = slice(hd * LANES, (hd + 2) * LANES)
            q2 = src_ref[c * CHUNK:(c + 1) * CHUNK, pair]
            k2 = src_ref[c * CHUNK:(c + 1) * CHUNK, RET_WIDTH + pair.start:RET_WIDTH + pair.stop]
            lane = lax.broadcasted_iota(jnp.int32, k2.shape, 1)
            zero = jnp.zeros_like(k2)
            kk = jnp.concatenate([jnp.where(lane < LANES, k2, zero),
                                  jnp.where(lane >= LANES, k2, zero)], axis=0)
            s2 = lax.dot_general(q2, kk, nt, preferred_element_type=F32)
            for j in range(2):
                s = s2[:, j * LANES:(j + 1) * LANES]
                scores[c, hd + j] = (s * dintra_ref[hd + j]).astype(BF16)

        def increment(c, hd):
            kd = (col(c, hd, 1).astype(F32) * kdec_ref[hd]).astype(BF16)
            incs[c, hd] = lax.dot_general(kd, col(c, hd, 2), tn, preferred_element_type=F32)

        def output(c, hd):
            if not cell:
                cell["state"] = state_in()
            state = cell["state"]
            if state is None:
                outs[c, hd] = jnp.dot(scores[c, hd], col(c, hd, 2), preferred_element_type=F32)
            else:
                qd = (col(c, hd, 0).astype(F32) * qdec_ref[hd]).astype(BF16)
                outs[c, hd] = jnp.dot(jnp.concatenate([scores[c, hd], qd], axis=1),
                                      jnp.concatenate([col(c, hd, 2), state[hd].astype(BF16)],
                                                      axis=0),
                                      preferred_element_type=F32)
            if hd == RET_HEADS - 1:
                if state is None:
                    cell["state"] = [incs[c, h] for h in range(RET_HEADS)]
                else:
                    cell["state"] = [state[h] * cd_ref[h] + incs[c, h] for h in range(RET_HEADS)]

        def store_state():
            for hd in range(RET_HEADS):
                for ref in state_out_refs:
                    ref[hd] = cell["state"][hd]

        def norm_gate(c, hd):
            o = outs[c, hd]
            mu = jnp.mean(o, axis=-1, keepdims=True)
            d = o - mu
            var = jnp.mean(d * d, axis=-1, keepdims=True)
            on = d * lax.rsqrt(var + EPS) * gnw_ref[layer:layer + 1, hd * LANES:(hd + 1) * LANES]
            y_ref[c * CHUNK:(c + 1) * CHUNK, hd * LANES:(hd + 1) * LANES] = (
                jax.nn.silu(col(c, hd, 3).astype(F32)) * on).astype(BF16)

        return ([functools.partial(stage, c, hd) for stage in (score, increment, output)
                 for c, hd in units] + [store_state]
                + [functools.partial(norm_gate, c, hd) for c, hd in units])

    def previous_tile_retention_steps():
        def state_in():
            first = (t + TILES_PER_BATCH - 1) % TILES_PER_BATCH == 0
            return [jnp.where(first, meta_state_ref[hd], state_ref[hd])
                    for hd in range(RET_HEADS)]

        return retention_steps(ring_ref.at[(t + 1) % 2], yf_ref, state_in, [state_ref])

    def emit(major, minor=()):
        per = -(-len(minor) // len(major))
        for k, step in enumerate(major):
            step()
            for extra in minor[k * per:(k + 1) * per]:
                extra()

    @pl.when(i < STAGE_STEPS)
    def _():
        _stage_weight(i, w_ref, w_s)

    @pl.when(i == STAGE_STEPS)
    def _():
        ring_ref[...] = jnp.zeros_like(ring_ref)
        state_ref[...] = jnp.zeros_like(state_ref)
        meta_ring = ring_ref.at[0, :CHUNK]
        emit(project_steps(hm_ref, tabm_ref, meta_ring, pm_ref))
        emit(retention_steps(meta_ring, ym_ref, lambda: None, [meta_state_ref]))

    @pl.when((t >= 0) & (t < N_PROJ_TILES))
    def _():
        steps = previous_tile_retention_steps()
        n_norm = (PROJ_TILE // CHUNK) * RET_HEADS
        emit(steps[:-n_norm])
        emit(project_steps(hf_ref, tabf_ref, ring_ref.at[t % 2], pf_ref), steps[-n_norm:])

    @pl.when(t == N_PROJ_TILES)
    def _():
        emit(previous_tile_retention_steps())


def _inproj_retention(layer, h, norm_g, w_in, tab_frames, tab_meta, cd, dintra, qdec, kdec, gn_w):
    steps = STAGE_STEPS + 1 + N_PROJ_TILES + 1
    tile = lambda i: jnp.clip(i - (STAGE_STEPS + 1), 0, N_PROJ_TILES - 1)
    lag = lambda i: jnp.clip(i - (STAGE_STEPS + 2), 0, N_PROJ_TILES - 1)
    const3 = _whole((RET_HEADS, CHUNK, LANES))
    outs = pl.pallas_call(
        functools.partial(_inproj_retention_kernel, layer=layer),
        grid=(steps,),
        in_specs=[pl.BlockSpec(memory_space=pltpu.SMEM),
                  pl.BlockSpec((PROJ_TILE, D_MODEL), lambda i: (tile(i), 0)),
                  _whole((CHUNK, D_MODEL)), _whole((DEPTH, D_MODEL)),
                  pl.BlockSpec((None, D_MODEL // STAGE_STEPS, D_IN),
                               lambda i: (layer, jnp.minimum(i, STAGE_STEPS - 1), 0)),
                  pl.BlockSpec((PROJ_TILE, N_TAB * LANES),
                               lambda i: (tile(i) % TILES_PER_BATCH, 0)),
                  _whole((CHUNK, N_TAB * LANES)),
                  const3, const3, const3, _whole((DEPTH, RET_WIDTH))],
        out_specs=[pl.BlockSpec((PROJ_TILE, DIFF_COLS), lambda i: (tile(i), 0)),
                   _whole((CHUNK, DIFF_COLS)),
                   pl.BlockSpec((PROJ_TILE, RET_WIDTH), lambda i: (lag(i), 0)),
                   _whole((CHUNK, RET_WIDTH))],
        out_shape=[jax.ShapeDtypeStruct((FRAME_ROWS, DIFF_COLS), BF16),
                   jax.ShapeDtypeStruct((CHUNK, DIFF_COLS), BF16),
                   jax.ShapeDtypeStruct((FRAME_ROWS, RET_WIDTH), BF16),
                   jax.ShapeDtypeStruct((CHUNK, RET_WIDTH), BF16)],
        scratch_shapes=[pltpu.VMEM((D_MODEL, D_IN), BF16),
                        pltpu.VMEM((2, PROJ_TILE, RET_COLS), BF16),
                        pltpu.VMEM((RET_HEADS, RET_DK, LANES), F32),
                        pltpu.VMEM((RET_HEADS, RET_DK, LANES), F32)],
        compiler_params=pltpu.CompilerParams(
            dimension_semantics=("arbitrary",), vmem_limit_bytes=VMEM_LIMIT),
        name="inproj_retention",
    )(cd, h[0], h[1], norm_g, w_in, tab_frames, tab_meta, dintra, qdec, kdec, gn_w)
    return (outs[0], outs[1]), (outs[2], outs[3])


_QUERY_BLOCKS = ((0, CHUNK),) + tuple((r, 2 * CHUNK) for r in range(CHUNK, T_PAD, 2 * CHUNK))


def _key_tiles(nk):
    full = tuple((r, 2 * CHUNK) for r in range(CHUNK, nk - 2 * CHUNK, 2 * CHUNK))
    diagonal = ((nk - 2 * CHUNK, CHUNK), (nk - CHUNK, CHUNK)) if nk > CHUNK else ()
    return ((PAD, N_META),) + full + diagonal


def _diffattn_kernel(qf_ref, kf_ref, vf_ref, qm_ref, km_ref, vm_ref, w_ref, lq1_ref, lk1_ref,
                     lq2_ref, lk2_ref, of_ref, om_ref, s_ref, p_ref, vt_ref, *, layer, lambda_init):
    row = slice(layer, layer + 1)
    lam = (jnp.exp(jnp.sum(lq1_ref[row, :] * lk1_ref[row, :], axis=-1, keepdims=True))
           - jnp.exp(jnp.sum(lq2_ref[row, :] * lk2_ref[row, :], axis=-1, keepdims=True))
           + lambda_init)
    w_out = w_ref[row, :] * (1.0 - lambda_init)
    tn = (((0,), (0,)), ((), ()))

    def rows(meta_ref, frames_ref, r0, n):
        if r0 < CHUNK:
            return meta_ref[r0:r0 + n, :]
        return frames_ref[0, r0 - CHUNK:r0 - CHUNK + n, :]

    def fold(state, key, value, op):
        state[key] = value if key not in state else op(state[key], value)

    def score_tiles(blk, slot, state):
        r0, nq = _QUERY_BLOCKS[blk]
        nk = r0 + nq
        qt = rows(qm_ref, qf_ref, r0, nq).astype(F32).T
        feat = lax.broadcasted_iota(jnp.int32, (LANES, nq), 0)
        qq = jnp.concatenate([jnp.where(feat < DIFF_D, qt, 0.0),
                              jnp.where(feat >= DIFF_D, qt, 0.0)], axis=1).astype(BF16)

        def tile(t0, tk):
            keys = rows(km_ref, kf_ref, t0, tk)
            if r0 >= CHUNK and t0 > r0:
                late = jnp.concatenate([qq[:, nq // 2:nq], qq[:, nq + nq // 2:]], axis=1)
                s = jnp.dot(keys, late, preferred_element_type=F32)
                krow = lax.broadcasted_iota(jnp.int32, (tk, nq), 0)
                qcol = lax.broadcasted_iota(jnp.int32, (tk, nq), 1) % (nq // 2)
                s = jnp.where(krow <= qcol, s, NEG)
                hidden = jnp.full((tk, nq // 2), NEG, F32)
                s = jnp.concatenate([hidden, s[:, :nq // 2], hidden, s[:, nq // 2:]], axis=1)
            else:
                s = jnp.dot(keys, qq, preferred_element_type=F32)
                if t0 + tk > r0:
                    krow = lax.broadcasted_iota(jnp.int32, (tk, 2 * nq), 0)
                    qcol = lax.broadcasted_iota(jnp.int32, (tk, 2 * nq), 1) % nq
                    s = jnp.where(krow + (t0 - r0) <= qcol, s, NEG)
            s_ref[slot, t0:t0 + tk, :2 * nq] = s
            fold(state, "max", jnp.max(s, axis=0, keepdims=True), jnp.maximum)

        return [functools.partial(tile, t0, tk) for t0, tk in _key_tiles(nk)]

    def prob_tiles(blk, slot, smax, state):
        r0, nq = _QUERY_BLOCKS[blk]
        nk = r0 + nq

        def tile(t0, tk):
            p = jnp.exp2(s_ref[slot, t0:t0 + tk, :2 * nq] - smax)
            fold(state, "pos", jnp.sum(p[:, :nq], axis=0, keepdims=True), jnp.add)
            fold(state, "neg", jnp.sum(p[:, nq:], axis=0, keepdims=True), jnp.add)
            p_ref[slot, t0:t0 + tk, :2 * nq] = p.astype(BF16)

        return [functools.partial(tile, t0, tk) for t0, tk in _key_tiles(nk)]

    def finish(blk, slot, sums):
        r0, nq = _QUERY_BLOCKS[blk]
        nk = r0 + nq
        ot = lax.dot_general(vm_ref[PAD:, :], p_ref[slot, PAD:CHUNK, :2 * nq], tn,
                             preferred_element_type=F32)
        if nk > CHUNK:
            ot = ot + jnp.dot(vt_ref[:, :nk - CHUNK], p_ref[slot, CHUNK:nk, :2 * nq],
                              preferred_element_type=F32)
        o = (ot[:, :nq] * (1.0 / sums["pos"]) - ot[:, nq:] * (lam / sums["neg"])).T
        o = o * lax.rsqrt(jnp.mean(o * o, axis=-1, keepdims=True) + EPS) * w_out
        if r0 == 0:
            orow = lax.broadcasted_iota(jnp.int32, (nq, LANES), 0)
            om_ref[...] = jnp.where(orow >= PAD, o, 0.0).astype(BF16)
        else:
            of_ref[0, r0 - CHUNK:nk - CHUNK, :] = o.astype(BF16)

    n_blocks = len(_QUERY_BLOCKS)
    order = list(range(0, n_blocks, 2)) + list(range(n_blocks - 1 - n_blocks % 2, 0, -2))
    vt_ref[...] = vf_ref[0].astype(F32).T.astype(BF16)
    score_state, pending = {}, None
    for run in score_tiles(order[0], 0, score_state):
        run()
    for r, blk in enumerate(order):
        smax, sums, score_state = score_state["max"], {}, {}
        ahead = score_tiles(order[r + 1], (r + 1) % 2, score_state) if r + 1 < n_blocks else []
        current = prob_tiles(blk, r % 2, smax, sums)
        for n in range(max(len(ahead), len(current))):
            for runs in (current, ahead):
                if n < len(runs):
                    runs[n]()
            if n == 0 and pending is not None:
                finish(*pending)
        pending = (blk, r % 2, sums)
    finish(*pending)


def _diffattn(layer, proj, subln_w, lq1, lk1, lq2, lk2):
    proj_frames, proj_meta = proj
    frames3 = proj_frames.reshape(BATCH, SEQ, DIFF_COLS)
    lambda_init = 0.8 - 0.6 * math.exp(-0.3 * layer)
    fblk = lambda c: pl.BlockSpec((1, SEQ, LANES), lambda hd, b, c=c: (b, 0, c + hd))
    mblk = lambda c: pl.BlockSpec((CHUNK, LANES), lambda hd, b, c=c: (0, c + hd))
    small = lambda n: _whole((DEPTH, n))
    y_frames, y_meta = pl.pallas_call(
        functools.partial(_diffattn_kernel, layer=layer, lambda_init=lambda_init),
        grid=(DIFF_HEADS, BATCH),
        in_specs=[fblk(0), fblk(DIFF_HEADS), fblk(2 * DIFF_HEADS),
                  mblk(0), mblk(DIFF_HEADS), mblk(2 * DIFF_HEADS),
                  small(DIFF_DV), small(DIFF_D), small(DIFF_D), small(DIFF_D), small(DIFF_D)],
        out_specs=[pl.BlockSpec((1, SEQ, LANES), lambda hd, b: (b, 0, hd)),
                   pl.BlockSpec((CHUNK, LANES), lambda hd, b: (0, hd))],
        out_shape=[jax.ShapeDtypeStruct((BATCH, SEQ, DIFF_WIDTH), BF16),
                   jax.ShapeDtypeStruct((CHUNK, DIFF_WIDTH), BF16)],
        scratch_shapes=[pltpu.VMEM((2, T_PAD, 4 * CHUNK), F32),
                        pltpu.VMEM((2, T_PAD, 4 * CHUNK), BF16),
                        pltpu.VMEM((DIFF_DV, SEQ), BF16)],
        compiler_params=pltpu.CompilerParams(
            dimension_semantics=("arbitrary", "arbitrary"), vmem_limit_bytes=VMEM_LIMIT),
        name="diffattn",
    )(frames3, frames3, frames3, proj_meta, proj_meta, proj_meta, subln_w, lq1, lk1, lq2, lk2)
    return y_frames.reshape(FRAME_ROWS, DIFF_WIDTH), y_meta


def _rotary_tables():
    f32 = np.float32
    pos = np.arange(T_PAD, dtype=f32) - f32(PAD)
    angle = (f32(RET_THETA) ** (-np.linspace(0.0, 1.0, RET_DK // 2, dtype=f32))).astype(f32)
    fr = pos[:, None] * angle[None, :]
    c, s = np.cos(fr), np.sin(fr)
    zero = np.zeros_like(s)
    cos_i = np.repeat(c, 2, axis=-1)
    sin_even = np.stack([-s, zero], axis=-1).reshape(T_PAD, RET_DK)
    sin_odd = np.stack([zero, s], axis=-1).reshape(T_PAD, RET_DK)
    ks = f32(RET_DK ** -0.5)
    inv = (f32(ROPE_THETA) ** (-np.arange(0, ROPE_DIMS, 2, dtype=f32) / f32(ROPE_DIMS))).astype(f32)
    fq = pos[:, None] * inv[None, :]
    emb = np.concatenate([fq, fq], axis=-1)
    ce, se = np.cos(emb), np.sin(emb)
    half, rest = ROPE_DIMS // 2, DIFF_D - ROPE_DIMS
    zeros = lambda n: np.zeros((T_PAD, n), f32)
    two = lambda a: np.concatenate([a, a], axis=-1)
    c_d = two(np.concatenate([ce, np.ones((T_PAD, rest), f32)], axis=-1))
    s_lo = two(np.concatenate([zeros(half), se[:, half:], zeros(rest)], axis=-1))
    s_hi = two(np.concatenate([-se[:, :half], zeros(half + rest)], axis=-1))
    qs = f32((DIFF_D ** -0.5) * math.log2(math.e))
    tabs = [cos_i, sin_even, sin_odd, cos_i * ks, sin_even * ks, sin_odd * ks,
            c_d * qs, s_lo * qs, s_hi * qs, c_d, s_lo, s_hi]
    return np.concatenate(tabs, axis=-1).astype(f32)


def _retention_consts():
    f32 = np.float32
    log_gamma = np.log(f32(1.0) - f32(2.0) ** (f32(-5.0) - np.arange(RET_HEADS, dtype=f32))).astype(f32)
    idx = np.arange(CHUNK, dtype=f32)
    rel = idx[:, None] - idx[None, :]
    dintra = np.where(rel >= 0, np.exp(log_gamma[:, None, None] * np.maximum(rel, f32(0.0))), f32(0.0))
    k_decay = np.exp(log_gamma[:, None] * (CHUNK - 1 - idx)[None, :])
    q_decay = np.exp(log_gamma[:, None] * (idx + f32(1.0))[None, :])
    cd = np.exp(log_gamma * f32(CHUNK))
    bc = lambda a: np.ascontiguousarray(np.broadcast_to(a[:, :, None], (RET_HEADS, CHUNK, LANES)))
    return cd.astype(f32), dintra.astype(f32), bc(q_decay).astype(f32), bc(k_decay).astype(f32)


def kernel(x, meta_tokens, ffn1_norm, ffn1_w_gate, ffn1_w_up, ffn1_w_down, mix_norm, w_in, ret_gn_w, diff_subln_w, diff_lambda_q1, diff_lambda_k1, diff_lambda_q2, diff_lambda_k2, w_out, ffn2_norm, ffn2_w_gate, ffn2_w_up, ffn2_w_down, final_norm):
    meta_chunk = jnp.concatenate([jnp.zeros((PAD, D_MODEL), F32), meta_tokens.astype(F32)], axis=0)
    h = (x.astype(F32).reshape(FRAME_ROWS, D_MODEL), meta_chunk)

    tab = _rotary_tables()
    tab_meta, tab_frames = jnp.asarray(tab[:CHUNK]), jnp.asarray(tab[CHUNK:])
    cd, dintra, qdec, kdec = (jnp.asarray(a) for a in _retention_consts())

    for l in range(DEPTH):
        last = l == DEPTH - 1
        h = _ffn(l, h, ffn1_norm, ffn1_w_gate, ffn1_w_up, ffn1_w_down)
        proj_diff, y_ret = _inproj_retention(l, h, mix_norm, w_in, tab_frames, tab_meta,
                                             cd, dintra, qdec, kdec, ret_gn_w)
        y_diff = _diffattn(l, proj_diff, diff_subln_w, diff_lambda_q1, diff_lambda_k1,
                           diff_lambda_q2, diff_lambda_k2)
        h = _ffn(l, h, ffn2_norm, ffn2_w_gate, ffn2_w_up, ffn2_w_down,
                 mix=(y_ret, y_diff, w_out),
                 final_g=final_norm.reshape(1, D_MODEL) if last else None)

    return h[0].reshape(BATCH, SEQ, D_MODEL).astype(x.dtype)
```

```python
import functools
import math

import numpy as np
import jax
import jax.numpy as jnp
from jax import lax
from jax.experimental import pallas as pl
from jax.experimental.pallas import tpu as pltpu

D_MODEL = 1024
BATCH = 8
SEQ = 2048
DEPTH = 2
N_META = 16
CHUNK = 128
RET_HEADS = 4
RET_DK = 128
RET_WIDTH = 512
RET_THETA = 10000.0
DIFF_HEADS = 4
DIFF_D = 64
DIFF_DV = 128
DIFF_WIDTH = 512
ROPE_THETA = 500000.0
ROPE_DIMS = 16
D_FF = 2816
EPS = 1e-6
D_IN = 3584

PAD = CHUNK - N_META
T_PAD = CHUNK + SEQ
FRAME_ROWS = BATCH * SEQ
N_TAB = 12
LANES = 128
VMEM_LIMIT = 56 * 1024 * 1024
NEG = -1e30

F32 = jnp.float32
BF16 = jnp.bfloat16


def _rms(x, g):
    return x * lax.rsqrt(jnp.mean(x * x, axis=-1, keepdims=True) + EPS) * g


def _whole(shape):
    return pl.BlockSpec(shape, lambda *_: (0,) * len(shape))


class _DenseGrid:
    def __init__(self, tile, stage, meta):
        self.tile, self.stage, self.meta = tile, stage, meta
        self.n_tiles = FRAME_ROWS // tile
        self.steps = stage + self.n_tiles + int(meta)

    def tile_index(self, i):
        return jnp.clip(i - self.stage, 0, self.n_tiles - 1)

    def row_tile(self, width):
        return pl.BlockSpec((self.tile, width), lambda i: (self.tile_index(i), 0))

    def weight_chunk(self, layer, rows, cols):
        return pl.BlockSpec((None, rows // self.stage, cols),
                            lambda i: (layer, jnp.minimum(i, self.stage - 1), 0),
                            pipeline_mode=pl.Buffered(1))

    def run(self, i, stage, frames, meta):
        pl.when(i < self.stage)(stage)
        pl.when((i >= self.stage) & (i < self.stage + self.n_tiles))(frames)
        if self.meta:
            pl.when(i == self.stage + self.n_tiles)(meta)


def _stage_weight(i, src_ref, dst_ref):
    rows = src_ref.shape[0]
    dst_ref[pl.ds(pl.multiple_of(i * rows, 16), rows), :] = src_ref[...].astype(BF16)


FFN_TILE = 512
FFN_SUB_TILE = 256
STAGE_STEPS = 4
PROJ_TILE = 512


def _ffn_kernel(*refs, grid, layer, mix, final):
    refs = list(refs)
    take = lambda n: [refs.pop(0) for _ in range(n)]
    with_meta = not final
    h_refs = take(2 if with_meta else 1)
    if mix:
        yr_refs, yd_refs = take(len(h_refs)), take(len(h_refs))
        (wo_ref,) = take(1)
    g_ref, wg_ref, wu_ref, wd_ref = take(4)
    if final:
        (fn_ref,) = take(1)
    o_refs = take(len(h_refs))
    wg_s, wu_s, wd_s = take(3)
    if mix:
        (wo_s,) = take(1)
    i = pl.program_id(0)

    def stage():
        _stage_weight(i, wg_ref, wg_s)
        _stage_weight(i, wu_ref, wu_s)
        _stage_weight(i, wd_ref, wd_s)
        if mix:
            _stage_weight(i, wo_ref, wo_s)

    def rows(which):
        n = h_refs[which].shape[0]
        parts = [slice(r, min(r + FFN_SUB_TILE, n)) for r in range(0, n, FFN_SUB_TILE)]
        xs, xns, acts = [], [], []
        for sl in parts:
            x = h_refs[which][sl, :]
            if mix:
                x = (x + jnp.dot(yr_refs[which][sl, :], wo_s[:RET_WIDTH, :],
                                 preferred_element_type=F32)
                     + jnp.dot(yd_refs[which][sl, :], wo_s[RET_WIDTH:, :],
                               preferred_element_type=F32))
            xs.append(x)
            xns.append(_rms(x, g_ref[layer:layer + 1, :]).astype(BF16))
        for xn in xns:
            gate = jnp.dot(xn, wg_s[...], preferred_element_type=F32)
            up = jnp.dot(xn, wu_s[...], preferred_element_type=F32)
            acts.append((jax.nn.silu(gate) * up).astype(BF16))
        for sl, x, act in zip(parts, xs, acts):
            y = x + 0.5 * jnp.dot(act, wd_s[...], preferred_element_type=F32)
            if final:
                y = _rms(y, fn_ref[...])
            o_refs[which][sl, :] = y

    grid.run(i, stage, functools.partial(rows, 0), functools.partial(rows, 1))


def _ffn(layer, h, norm_g, wg, wu, wd, mix=None, final_g=None):
    with_meta = final_g is None
    grid = _DenseGrid(tile=FFN_TILE, stage=STAGE_STEPS, meta=with_meta)
    _row_tile, _weight_chunk = grid.row_tile, grid.weight_chunk
    pair = lambda width: [_row_tile(width)] + ([_whole((CHUNK, width))] if with_meta else [])
    keep = lambda arrays: list(arrays) if with_meta else [arrays[0]]
    args, specs = keep(h), pair(D_MODEL)
    scratch = [pltpu.VMEM((D_MODEL, D_FF), BF16), pltpu.VMEM((D_MODEL, D_FF), BF16),
               pltpu.VMEM((D_FF, D_MODEL), BF16)]
    if mix is not None:
        y_ret, y_diff, w_out = mix
        args += keep(y_ret) + keep(y_diff) + [w_out]
        specs += pair(RET_WIDTH) + pair(DIFF_WIDTH) + [_weight_chunk(layer, D_MODEL, D_MODEL)]
        scratch.append(pltpu.VMEM((D_MODEL, D_MODEL), BF16))
    args += [norm_g, wg, wu, wd]
    specs += [_whole((DEPTH, D_MODEL)), _weight_chunk(layer, D_MODEL, D_FF),
              _weight_chunk(layer, D_MODEL, D_FF), _weight_chunk(layer, D_FF, D_MODEL)]
    if final_g is not None:
        args.append(final_g)
        specs.append(_whole((1, D_MODEL)))
    out_shape = [jax.ShapeDtypeStruct((FRAME_ROWS, D_MODEL), F32)]
    if with_meta:
        out_shape.append(jax.ShapeDtypeStruct((CHUNK, D_MODEL), F32))
    return pl.pallas_call(
        functools.partial(_ffn_kernel, grid=grid, layer=layer, mix=mix is not None,
                          final=final_g is not None),
        grid=(grid.steps,),
        in_specs=specs,
        out_specs=pair(D_MODEL),
        out_shape=out_shape,
        scratch_shapes=scratch,
        compiler_params=pltpu.CompilerParams(
            dimension_semantics=("arbitrary",), vmem_limit_bytes=VMEM_LIMIT),
        name="ffn",
    )(*args)


_ROTATIONS = (
    (0, 0, LANES - 1, 1),
    (RET_WIDTH, 3, LANES - 1, 1),
    (4 * RET_WIDTH, 6, 8, LANES - 8),
    (4 * RET_WIDTH + DIFF_WIDTH, 9, 8, LANES - 8),
)


RET_COLS = 4 * RET_WIDTH
DIFF_COLS = D_IN - RET_COLS
N_PROJ_TILES = FRAME_ROWS // PROJ_TILE
TILES_PER_BATCH = SEQ // PROJ_TILE


def _inproj_retention_kernel(cd_ref, hf_ref, hm_ref, g_ref, w_ref, tabf_ref, tabm_ref,
                             dintra_ref, qdec_ref, kdec_ref, gnw_ref,
                             pf_ref, pm_ref, yf_ref, ym_ref,
                             w_s, ring_ref, state_ref, meta_state_ref, *, layer):
    nt = (((1,), (1,)), ((), ()))
    tn = (((0,), (0,)), ((), ()))
    i = pl.program_id(0)
    t = i - (STAGE_STEPS + 1)

    def project_steps(h_ref, tab_ref, ret_ref, diff_ref):
        rotation = {base: rest for base, *rest in _ROTATIONS}
        cell = {}

        def tab(k):
            return tab_ref[:, k * LANES:(k + 1) * LANES]

        def group(base):
            if not cell:
                cell["xn"] = _rms(h_ref[...], g_ref[layer:layer + 1, :]).astype(BF16)
            p = jnp.dot(cell["xn"], w_s[:, base:base + RET_WIDTH], preferred_element_type=F32)
            dst, lo = (ret_ref, base) if base < RET_COLS else (diff_ref, base - RET_COLS)
            if base in rotation:
                t0, roll_a, roll_b = rotation[base]
                for hd in range(RET_HEADS):
                    xs = p[:, hd * LANES:(hd + 1) * LANES]
                    rot = (xs * tab(t0) + pltpu.roll(xs, roll_a, 1) * tab(t0 + 1)
                           + pltpu.roll(xs, roll_b, 1) * tab(t0 + 2))
                    dst[:, lo + hd * LANES:lo + (hd + 1) * LANES] = rot.astype(BF16)
            else:
                dst[:, lo:lo + RET_WIDTH] = p.astype(BF16)

        return [functools.partial(group, base) for base in range(0, D_IN, RET_WIDTH)]

    def retention_steps(src_ref, y_ref, state_in, state_out_refs):
        n_chunks = src_ref.shape[0] // CHUNK
        units = [(c, hd) for c in range(n_chunks) for hd in range(RET_HEADS)]
        scores, incs, outs, cell = {}, {}, {}, {}

        def col(c, hd, which):
            return src_ref[c * CHUNK:(c + 1) * CHUNK,
                           which * RET_WIDTH + hd * LANES:which * RET_WIDTH + (hd + 1) * LANES]

        def score(c, hd):
            if hd % 2:
                return
            pair = slice(hd * LANES, (hd + 2) * LANES)
            q2 = src_ref[c * CHUNK:(c + 1) * CHUNK, pair]
            k2 = src_ref[c * CHUNK:(c + 1) * CHUNK, RET_WIDTH + pair.start:RET_WIDTH + pair.stop]
            lane = lax.broadcasted_iota(jnp.int32, k2.shape, 1)
            zero = jnp.zeros_like(k2)
            kk = jnp.concatenate([jnp.where(lane < LANES, k2, zero),
                                  jnp.where(lane >= LANES, k2, zero)], axis=0)
            s2 = lax.dot_general(q2, kk, nt, preferred_element_type=F32)
            for j in range(2):
                s = s2[:, j * LANES:(j + 1) * LANES]
                scores[c, hd + j] = (s * dintra_ref[hd + j]).astype(BF16)

        def increment(c, hd):
            kd = (col(c, hd, 1).astype(F32) * kdec_ref[hd]).astype(BF16)
            incs[c, hd] = lax.dot_general(kd, col(c, hd, 2), tn, preferred_element_type=F32)

        def output(c, hd):
            if not cell:
                cell["state"] = state_in()
            state = cell["state"]
            if state is None:
                outs[c, hd] = jnp.dot(scores[c, hd], col(c, hd, 2), preferred_element_type=F32)
            else:
                qd = (col(c, hd, 0).astype(F32) * qdec_ref[hd]).astype(BF16)
                outs[c, hd] = jnp.dot(jnp.concatenate([scores[c, hd], qd], axis=1),
                                      jnp.concatenate([col(c, hd, 2), state[hd].astype(BF16)],
                                                      axis=0),
                                      preferred_element_type=F32)
            if hd == RET_HEADS - 1:
                if state is None:
                    cell["state"] = [incs[c, h] for h in range(RET_HEADS)]
                else:
                    cell["state"] = [state[h] * cd_ref[h] + incs[c, h] for h in range(RET_HEADS)]

        def store_state():
            for hd in range(RET_HEADS):
                for ref in state_out_refs:
                    ref[hd] = cell["state"][hd]

        def norm_gate(c, hd):
            o = outs[c, hd]
            mu = jnp.mean(o, axis=-1, keepdims=True)
            d = o - mu
            var = jnp.mean(d * d, axis=-1, keepdims=True)
            on = d * lax.rsqrt(var + EPS) * gnw_ref[layer:layer + 1, hd * LANES:(hd + 1) * LANES]
            y_ref[c * CHUNK:(c + 1) * CHUNK, hd * LANES:(hd + 1) * LANES] = (
                jax.nn.silu(col(c, hd, 3).astype(F32)) * on).astype(BF16)

        return ([functools.partial(stage, c, hd) for stage in (score, increment, output)
                 for c, hd in units] + [store_state]
                + [functools.partial(norm_gate, c, hd) for c, hd in units])

    def previous_tile_retention_steps():
        def state_in():
            first = (t + TILES_PER_BATCH - 1) % TILES_PER_BATCH == 0
            return [jnp.where(first, meta_state_ref[hd], state_ref[hd])
                    for hd in range(RET_HEADS)]

        return retention_steps(ring_ref.at[(t + 1) % 2], yf_ref, state_in, [state_ref])

    def emit(major, minor=()):
        per = -(-len(minor) // len(major))
        for k, step in enumerate(major):
            step()
            for extra in minor[k * per:(k + 1) * per]:
                extra()

    @pl.when(i < STAGE_STEPS)
    def _():
        _stage_weight(i, w_ref, w_s)

    @pl.when(i == STAGE_STEPS)
    def _():
        ring_ref[...] = jnp.zeros_like(ring_ref)
        state_ref[...] = jnp.zeros_like(state_ref)
        meta_ring = ring_ref.at[0, :CHUNK]
        emit(project_steps(hm_ref, tabm_ref, meta_ring, pm_ref))
        emit(retention_steps(meta_ring, ym_ref, lambda: None, [meta_state_ref]))

    @pl.when((t >= 0) & (t < N_PROJ_TILES))
    def _():
        steps = previous_tile_retention_steps()
        n_norm = (PROJ_TILE // CHUNK) * RET_HEADS
        emit(steps[:-n_norm])
        emit(project_steps(hf_ref, tabf_ref, ring_ref.at[t % 2], pf_ref), steps[-n_norm:])

    @pl.when(t == N_PROJ_TILES)
    def _():
        emit(previous_tile_retention_steps())


def _inproj_retention(layer, h, norm_g, w_in, tab_frames, tab_meta, cd, dintra, qdec, kdec, gn_w):
    steps = STAGE_STEPS + 1 + N_PROJ_TILES + 1
    tile = lambda i: jnp.clip(i - (STAGE_STEPS + 1), 0, N_PROJ_TILES - 1)
    lag = lambda i: jnp.clip(i - (STAGE_STEPS + 2), 0, N_PROJ_TILES - 1)
    const3 = _whole((RET_HEADS, CHUNK, LANES))
    outs = pl.pallas_call(
        functools.partial(_inproj_retention_kernel, layer=layer),
        grid=(steps,),
        in_specs=[pl.BlockSpec(memory_space=pltpu.SMEM),
                  pl.BlockSpec((PROJ_TILE, D_MODEL), lambda i: (tile(i), 0)),
                  _whole((CHUNK, D_MODEL)), _whole((DEPTH, D_MODEL)),
                  pl.BlockSpec((None, D_MODEL // STAGE_STEPS, D_IN),
                               lambda i: (layer, jnp.minimum(i, STAGE_STEPS - 1), 0),
                               pipeline_mode=pl.Buffered(1)),
                  pl.BlockSpec((PROJ_TILE, N_TAB * LANES),
                               lambda i: (tile(i) % TILES_PER_BATCH, 0)),
                  _whole((CHUNK, N_TAB * LANES)),
                  const3, const3, const3, _whole((DEPTH, RET_WIDTH))],
        out_specs=[pl.BlockSpec((PROJ_TILE, DIFF_COLS), lambda i: (tile(i), 0)),
                   _whole((CHUNK, DIFF_COLS)),
                   pl.BlockSpec((PROJ_TILE, RET_WIDTH), lambda i: (lag(i), 0)),
                   _whole((CHUNK, RET_WIDTH))],
        out_shape=[jax.ShapeDtypeStruct((FRAME_ROWS, DIFF_COLS), BF16),
                   jax.ShapeDtypeStruct((CHUNK, DIFF_COLS), BF16),
                   jax.ShapeDtypeStruct((FRAME_ROWS, RET_WIDTH), BF16),
                   jax.ShapeDtypeStruct((CHUNK, RET_WIDTH), BF16)],
        scratch_shapes=[pltpu.VMEM((D_MODEL, D_IN), BF16),
                        pltpu.VMEM((2, PROJ_TILE, RET_COLS), BF16),
                        pltpu.VMEM((RET_HEADS, RET_DK, LANES), F32),
                        pltpu.VMEM((RET_HEADS, RET_DK, LANES), F32)],
        compiler_params=pltpu.CompilerParams(
            dimension_semantics=("arbitrary",), vmem_limit_bytes=VMEM_LIMIT),
        name="inproj_retention",
    )(cd, h[0], h[1], norm_g, w_in, tab_frames, tab_meta, dintra, qdec, kdec, gn_w)
    return (outs[0], outs[1]), (outs[2], outs[3])


_QUERY_BLOCKS = ((0, CHUNK),) + tuple((r, 2 * CHUNK) for r in range(CHUNK, T_PAD, 2 * CHUNK))


def _key_tiles(nk):
    full = tuple((r, 2 * CHUNK) for r in range(CHUNK, nk - 2 * CHUNK, 2 * CHUNK))
    diagonal = ((nk - 2 * CHUNK, CHUNK), (nk - CHUNK, CHUNK)) if nk > CHUNK else ()
    return ((PAD, N_META),) + full + diagonal


def _diffattn_kernel(qf_ref, kf_ref, vf_ref, qm_ref, km_ref, vm_ref, w_ref, lq1_ref, lk1_ref,
                     lq2_ref, lk2_ref, of_ref, om_ref, s_ref, p_ref, vt_ref, *, layer, lambda_init):
    row = slice(layer, layer + 1)
    lam = (jnp.exp(jnp.sum(lq1_ref[row, :] * lk1_ref[row, :], axis=-1, keepdims=True))
           - jnp.exp(jnp.sum(lq2_ref[row, :] * lk2_ref[row, :], axis=-1, keepdims=True))
           + lambda_init)
    w_out = w_ref[row, :] * (1.0 - lambda_init)
    tn = (((0,), (0,)), ((), ()))

    def rows(meta_ref, frames_ref, r0, n):
        if r0 < CHUNK:
            return meta_ref[r0:r0 + n, :]
        return frames_ref[0, r0 - CHUNK:r0 - CHUNK + n, :]

    def fold(state, key, value, op):
        state[key] = value if key not in state else op(state[key], value)

    def score_tiles(blk, slot, state):
        r0, nq = _QUERY_BLOCKS[blk]
        nk = r0 + nq
        qt = rows(qm_ref, qf_ref, r0, nq).astype(F32).T
        feat = lax.broadcasted_iota(jnp.int32, (LANES, nq), 0)
        qq = jnp.concatenate([jnp.where(feat < DIFF_D, qt, 0.0),
                              jnp.where(feat >= DIFF_D, qt, 0.0)], axis=1).astype(BF16)

        def tile(t0, tk):
            keys = rows(km_ref, kf_ref, t0, tk)
            if r0 >= CHUNK and t0 > r0:
                late = jnp.concatenate([qq[:, nq // 2:nq], qq[:, nq + nq // 2:]], axis=1)
                s = jnp.dot(keys, late, preferred_element_type=F32)
                krow = lax.broadcasted_iota(jnp.int32, (tk, nq), 0)
                qcol = lax.broadcasted_iota(jnp.int32, (tk, nq), 1) % (nq // 2)
                s = jnp.where(krow <= qcol, s, NEG)
                hidden = jnp.full((tk, nq // 2), NEG, F32)
                s = jnp.concatenate([hidden, s[:, :nq // 2], hidden, s[:, nq // 2:]], axis=1)
            else:
                s = jnp.dot(keys, qq, preferred_element_type=F32)
                if t0 + tk > r0:
                    krow = lax.broadcasted_iota(jnp.int32, (tk, 2 * nq), 0)
                    qcol = lax.broadcasted_iota(jnp.int32, (tk, 2 * nq), 1) % nq
                    s = jnp.where(krow + (t0 - r0) <= qcol, s, NEG)
            s_ref[slot, t0:t0 + tk, :2 * nq] = s
            fold(state, "max", jnp.max(s, axis=0, keepdims=True), jnp.maximum)

        return [functools.partial(tile, t0, tk) for t0, tk in _key_tiles(nk)]

    def prob_tiles(blk, slot, smax, state):
        r0, nq = _QUERY_BLOCKS[blk]
        nk = r0 + nq

        def tile(t0, tk):
            p = jnp.exp2(s_ref[slot, t0:t0 + tk, :2 * nq] - smax)
            fold(state, "pos", jnp.sum(p[:, :nq], axis=0, keepdims=True), jnp.add)
            fold(state, "neg", jnp.sum(p[:, nq:], axis=0, keepdims=True), jnp.add)
            p_ref[slot, t0:t0 + tk, :2 * nq] = p.astype(BF16)

        return [functools.partial(tile, t0, tk) for t0, tk in _key_tiles(nk)]

    def finish(blk, slot, sums):
        r0, nq = _QUERY_BLOCKS[blk]
        nk = r0 + nq
        ot = lax.dot_general(vm_ref[PAD:, :], p_ref[slot, PAD:CHUNK, :2 * nq], tn,
                             preferred_element_type=F32)
        if nk > CHUNK:
            ot = ot + jnp.dot(vt_ref[:, :nk - CHUNK], p_ref[slot, CHUNK:nk, :2 * nq],
                              preferred_element_type=F32)
        o = (ot[:, :nq] * (1.0 / sums["pos"]) - ot[:, nq:] * (lam / sums["neg"])).T
        o = o * lax.rsqrt(jnp.mean(o * o, axis=-1, keepdims=True) + EPS) * w_out
        if r0 == 0:
            orow = lax.broadcasted_iota(jnp.int32, (nq, LANES), 0)
            om_ref[...] = jnp.where(orow >= PAD, o, 0.0).astype(BF16)
        else:
            of_ref[0, r0 - CHUNK:nk - CHUNK, :] = o.astype(BF16)

    n_blocks = len(_QUERY_BLOCKS)
    order = list(range(0, n_blocks, 2)) + list(range(n_blocks - 1 - n_blocks % 2, 0, -2))
    vt_ref[...] = vf_ref[0].astype(F32).T.astype(BF16)
    score_state, pending = {}, None
    for run in score_tiles(order[0], 0, score_state):
        run()
    for r, blk in enumerate(order):
        smax, sums, score_state = score_state["max"], {}, {}
        ahead = score_tiles(order[r + 1], (r + 1) % 2, score_state) if r + 1 < n_blocks else []
        current = prob_tiles(blk, r % 2, smax, sums)
        for n in range(max(len(ahead), len(current))):
            for runs in (current, ahead):
                if n < len(runs):
                    runs[n]()
            if n == 0 and pending is not None:
                finish(*pending)
        pending = (blk, r % 2, sums)
    finish(*pending)


def _diffattn(layer, proj, subln_w, lq1, lk1, lq2, lk2):
    proj_frames, proj_meta = proj
    frames3 = proj_frames.reshape(BATCH, SEQ, DIFF_COLS)
    lambda_init = 0.8 - 0.6 * math.exp(-0.3 * layer)
    fblk = lambda c: pl.BlockSpec((1, SEQ, LANES), lambda hd, b, c=c: (b, 0, c + hd))
    mblk = lambda c: pl.BlockSpec((CHUNK, LANES), lambda hd, b, c=c: (0, c + hd))
    small = lambda n: _whole((DEPTH, n))
    y_frames, y_meta = pl.pallas_call(
        functools.partial(_diffattn_kernel, layer=layer, lambda_init=lambda_init),
        grid=(DIFF_HEADS, BATCH),
        in_specs=[fblk(0), fblk(DIFF_HEADS), fblk(2 * DIFF_HEADS),
                  mblk(0), mblk(DIFF_HEADS), mblk(2 * DIFF_HEADS),
                  small(DIFF_DV), small(DIFF_D), small(DIFF_D), small(DIFF_D), small(DIFF_D)],
        out_specs=[pl.BlockSpec((1, SEQ, LANES), lambda hd, b: (b, 0, hd)),
                   pl.BlockSpec((CHUNK, LANES), lambda hd, b: (0, hd))],
        out_shape=[jax.ShapeDtypeStruct((BATCH, SEQ, DIFF_WIDTH), BF16),
                   jax.ShapeDtypeStruct((CHUNK, DIFF_WIDTH), BF16)],
        scratch_shapes=[pltpu.VMEM((2, T_PAD, 4 * CHUNK), F32),
                        pltpu.VMEM((2, T_PAD, 4 * CHUNK), BF16),
                        pltpu.VMEM((DIFF_DV, SEQ), BF16)],
        compiler_params=pltpu.CompilerParams(
            dimension_semantics=("arbitrary", "arbitrary"), vmem_limit_bytes=VMEM_LIMIT),
        name="diffattn",
    )(frames3, frames3, frames3, proj_meta, proj_meta, proj_meta, subln_w, lq1, lk1, lq2, lk2)
    return y_frames.reshape(FRAME_ROWS, DIFF_WIDTH), y_meta


def _rotary_tables():
    f32 = np.float32
    pos = np.arange(T_PAD, dtype=f32) - f32(PAD)
    angle = (f32(RET_THETA) ** (-np.linspace(0.0, 1.0, RET_DK // 2, dtype=f32))).astype(f32)
    fr = pos[:, None] * angle[None, :]
    c, s = np.cos(fr), np.sin(fr)
    zero = np.zeros_like(s)
    cos_i = np.repeat(c, 2, axis=-1)
    sin_even = np.stack([-s, zero], axis=-1).reshape(T_PAD, RET_DK)
    sin_odd = np.stack([zero, s], axis=-1).reshape(T_PAD, RET_DK)
    ks = f32(RET_DK ** -0.5)
    inv = (f32(ROPE_THETA) ** (-np.arange(0, ROPE_DIMS, 2, dtype=f32) / f32(ROPE_DIMS))).astype(f32)
    fq = pos[:, None] * inv[None, :]
    emb = np.concatenate([fq, fq], axis=-1)
    ce, se = np.cos(emb), np.sin(emb)
    half, rest = ROPE_DIMS // 2, DIFF_D - ROPE_DIMS
    zeros = lambda n: np.zeros((T_PAD, n), f32)
    two = lambda a: np.concatenate([a, a], axis=-1)
    c_d = two(np.concatenate([ce, np.ones((T_PAD, rest), f32)], axis=-1))
    s_lo = two(np.concatenate([zeros(half), se[:, half:], zeros(rest)], axis=-1))
    s_hi = two(np.concatenate([-se[:, :half], zeros(half + rest)], axis=-1))
    qs = f32((DIFF_D ** -0.5) * math.log2(math.e))
    tabs = [cos_i, sin_even, sin_odd, cos_i * ks, sin_even * ks, sin_odd * ks,
            c_d * qs, s_lo * qs, s_hi * qs, c_d, s_lo, s_hi]
    return np.concatenate(tabs, axis=-1).astype(f32)


def _retention_consts():
    f32 = np.float32
    log_gamma = np.log(f32(1.0) - f32(2.0) ** (f32(-5.0) - np.arange(RET_HEADS, dtype=f32))).astype(f32)
    idx = np.arange(CHUNK, dtype=f32)
    rel = idx[:, None] - idx[None, :]
    dintra = np.where(rel >= 0, np.exp(log_gamma[:, None, None] * np.maximum(rel, f32(0.0))), f32(0.0))
    k_decay = np.exp(log_gamma[:, None] * (CHUNK - 1 - idx)[None, :])
    q_decay = np.exp(log_gamma[:, None] * (idx + f32(1.0))[None, :])
    cd = np.exp(log_gamma * f32(CHUNK))
    bc = lambda a: np.ascontiguousarray(np.broadcast_to(a[:, :, None], (RET_HEADS, CHUNK, LANES)))
    return cd.astype(f32), dintra.astype(f32), bc(q_decay).astype(f32), bc(k_decay).astype(f32)


def kernel(x, meta_tokens, ffn1_norm, ffn1_w_gate, ffn1_w_up, ffn1_w_down, mix_norm, w_in, ret_gn_w, diff_subln_w, diff_lambda_q1, diff_lambda_k1, diff_lambda_q2, diff_lambda_k2, w_out, ffn2_norm, ffn2_w_gate, ffn2_w_up, ffn2_w_down, final_norm):
    meta_chunk = jnp.concatenate([jnp.zeros((PAD, D_MODEL), F32), meta_tokens.astype(F32)], axis=0)
    h = (x.astype(F32).reshape(FRAME_ROWS, D_MODEL), meta_chunk)

    tab = _rotary_tables()
    tab_meta, tab_frames = jnp.asarray(tab[:CHUNK]), jnp.asarray(tab[CHUNK:])
    cd, dintra, qdec, kdec = (jnp.asarray(a) for a in _retention_consts())

    for l in range(DEPTH):
        last = l == DEPTH - 1
        h = _ffn(l, h, ffn1_norm, ffn1_w_gate, ffn1_w_up, ffn1_w_down)
        proj_diff, y_ret = _inproj_retention(l, h, mix_norm, w_in, tab_frames, tab_meta,
                                             cd, dintra, qdec, kdec, ret_gn_w)
        y_diff = _diffattn(l, proj_diff, diff_subln_w, diff_lambda_q1, diff_lambda_k1,
                           diff_lambda_q2, diff_lambda_k2)
        h = _ffn(l, h, ffn2_norm, ffn2_w_gate, ffn2_w_up, ffn2_w_down,
                 mix=(y_ret, y_diff, w_out),
                 final_g=final_norm.reshape(1, D_MODEL) if last else None)

    return h[0].reshape(BATCH, SEQ, D_MODEL).astype(x.dtype)
```
